```python
import math
import jax
import jax.numpy as jnp
from jax import lax
import numpy as np

D_MODEL = 2048
BATCH = 2
SEQ = 4096
DEPTH = 4
DEC_BATCH = 8
DEC_SEQ = 4
PAST_LEN = 16384
PAGE_SIZE = 128

N_MIXERS = 2
N_NSA_LAYERS = (DEPTH + 1) // 2
N_S5_LAYERS = DEPTH // 2

NSA_HEADS = 16
NSA_KV_HEADS = 4
NSA_GROUP = NSA_HEADS // NSA_KV_HEADS
NSA_HEAD_DIM = D_MODEL // NSA_HEADS
NSA_Q_DIM = NSA_HEADS * NSA_HEAD_DIM
NSA_KV_DIM = NSA_KV_HEADS * NSA_HEAD_DIM
NSA_IN_DIM = NSA_Q_DIM + 6 * NSA_KV_DIM + 3 * NSA_HEADS
NSA_SCALE = NSA_HEAD_DIM ** -0.5
CMP_BLOCK = 32
CMP_STRIDE = 16
SLC_BLOCK = 64
SLC_TOP_N = 16
WINDOW = 512
WIN_QBLOCK = 128
SEL_QBLOCK = 64
FORCE_BONUS = 1.0e4

S5_GROUP_CH = 16
S5_GROUPS = D_MODEL // S5_GROUP_CH
S5_STATE = 64
S5_CHUNK = 128

PEER_HEADS = 8
PEER_N_KEYS = 128
PEER_N_EXPERTS = PEER_N_KEYS ** 2
PEER_KEY_DIM = 256
PEER_HALF = PEER_KEY_DIM // 2
PEER_TOPK = 16
PEER_CHUNK = 128

LN_EPS = 1e-5
NEG_INF = -1.0e30

kernel_name = 'nsa_s5_peer_hybrid_step'


def _layernorm(x, g, b):
    xf = x.astype(jnp.float32)
    xc = xf - xf.mean(-1, keepdims=True)
    var = (xc * xc).mean(-1, keepdims=True)
    y = xc * lax.rsqrt(var + LN_EPS)
    return (y * g.astype(jnp.float32) + b.astype(jnp.float32)).astype(x.dtype)


def _masked_softmax(s, mask):
    s = jnp.where(mask, s.astype(jnp.float32), NEG_INF)
    e = jnp.exp(s - s.max(-1, keepdims=True)) * mask
    return e / jnp.maximum(e.sum(-1, keepdims=True), 1e-30)


def _alibi_slopes():
    h = jnp.arange(1, NSA_HEADS + 1, dtype=jnp.float32)
    return jnp.exp2(-8.0 * h / NSA_HEADS).reshape(NSA_KV_HEADS, NSA_GROUP)


def _nsa_project(x, w_in):
    b, t, _ = x.shape
    h = x @ w_in
    q = h[..., :NSA_Q_DIM].reshape(b, t, NSA_KV_HEADS, NSA_GROUP, NSA_HEAD_DIM)
    kv = h[..., NSA_Q_DIM:NSA_Q_DIM + 6 * NSA_KV_DIM].reshape(b, t, 6, NSA_KV_HEADS, NSA_HEAD_DIM)
    gates = jax.nn.sigmoid(h[..., NSA_Q_DIM + 6 * NSA_KV_DIM:].astype(jnp.float32))
    gates = gates.reshape(b, t, NSA_KV_HEADS, NSA_GROUP, 3)
    return q, kv, gates


def _compress(k, pool, pe, phi):
    b, l, kvh, hd = k.shape
    n_sub = -(-l // CMP_STRIDE)
    k = jnp.pad(k, ((0, 0), (0, n_sub * CMP_STRIDE - l), (0, 0), (0, 0)))
    ksub = k.reshape(b, n_sub, CMP_STRIDE, kvh, hd)
    r = CMP_BLOCK // CMP_STRIDE
    n_blk = n_sub - r + 1
    pooled = jnp.einsum('kj,kjd->kd', pool, pe)
    for i in range(r):
        pooled = pooled + jnp.einsum('bnjkd,kj->bnkd', ksub[:, i:i + n_blk],
                                     pool[:, i * CMP_STRIDE:(i + 1) * CMP_STRIDE])
    out = jnp.einsum('bnkd,kde->bnke', pooled, phi)
    end = CMP_STRIDE * jnp.arange(n_blk) + CMP_BLOCK - 1
    return out, end


def _to_blocks(k):
    b, l, kvh, hd = k.shape
    ns = -(-l // SLC_BLOCK)
    k = jnp.pad(k, ((0, 0), (0, ns * SLC_BLOCK - l), (0, 0), (0, 0)))
    return k.reshape(b, ns, SLC_BLOCK, kvh, hd).transpose(0, 3, 1, 2, 4)


def _cmp_to_slc(pc, ns):
    r = SLC_BLOCK // CMP_STRIDE
    lp = CMP_BLOCK // CMP_STRIDE - 1
    total = r * (ns + 1)
    pc = pc[..., :total - lp]
    pad = [(0, 0)] * (pc.ndim - 1) + [(lp, total - lp - pc.shape[-1])]
    g = jnp.pad(pc, pad).reshape(pc.shape[:-1] + (ns + 1, r))
    return g[..., :ns, :].sum(-1) + g[..., 1:, :lp].sum(-1)


def _cmp_slc(q, qpos, ck, cv, ck_end, kb, vb, slopes):
    b, nq, kvh, g, hd = q.shape
    ns = kb.shape[2]
    dist = qpos[:, None] - ck_end[None, :]
    s = jnp.einsum('bqkgd,bnkd->bkgqn', q, ck) * NSA_SCALE
    s = s - slopes[None, :, :, None, None] * dist
    p = _masked_softmax(s, dist >= 0)
    o_c = jnp.einsum('bkgqn,bnkd->bqkgd', p.astype(cv.dtype), cv)
    p_slc = _cmp_to_slc(p.sum(2), ns)
    blk = jnp.arange(ns)
    cur = qpos // SLC_BLOCK
    avail = blk[None, :] <= cur[:, None]
    forced = (blk[None, :] == 0) | (blk[None, :] == cur[:, None]) | (blk[None, :] == cur[:, None] - 1)
    score = jnp.where(avail, p_slc + jnp.where(forced, FORCE_BONUS, 0.0), -1.0)
    _, idx = lax.top_k(score, min(SLC_TOP_N, ns))
    gather = jax.vmap(jax.vmap(lambda blocks, ix: blocks[ix]))
    n_sel = idx.shape[-1] * SLC_BLOCK
    kg = gather(kb, idx).reshape(b, kvh, nq, n_sel, hd)
    vg = gather(vb, idx).reshape(b, kvh, nq, n_sel, hd)
    spos = (idx[..., None] * SLC_BLOCK + jnp.arange(SLC_BLOCK)).reshape(b, kvh, nq, n_sel)
    d2 = qpos[None, None, :, None] - spos
    s2 = jnp.einsum('bqkgd,bkqsd->bkgqs', q, kg) * NSA_SCALE
    s2 = s2 - slopes[None, :, :, None, None] * d2[:, :, None]
    p2 = _masked_softmax(s2, (d2 >= 0)[:, :, None])
    o_s = jnp.einsum('bkgqs,bkqsd->bqkgd', p2.astype(vg.dtype), vg)
    return o_c, o_s


def _window_banded(q, k, v, slopes):
    b, s, kvh, g, hd = q.shape
    nb = s // WIN_QBLOCK
    nw = WINDOW // WIN_QBLOCK
    pad = ((0, 0), (WINDOW, 0), (0, 0), (0, 0))
    kp = jnp.pad(k, pad).reshape(b, nb + nw, WIN_QBLOCK, kvh, hd)
    vp = jnp.pad(v, pad).reshape(b, nb + nw, WIN_QBLOCK, kvh, hd)
    kwin = jnp.concatenate([kp[:, i:i + nb] for i in range(nw + 1)], axis=2)
    vwin = jnp.concatenate([vp[:, i:i + nb] for i in range(nw + 1)], axis=2)
    qb = q.reshape(b, nb, WIN_QBLOCK, kvh, g, hd)
    qpos = jnp.arange(s).reshape(nb, WIN_QBLOCK)
    kpos = (jnp.arange(nb) * WIN_QBLOCK - WINDOW)[:, None] + jnp.arange((nw + 1) * WIN_QBLOCK)[None, :]
    dist = qpos[:, :, None] - kpos[:, None, :]
    mask = (dist >= 0) & (dist <= WINDOW) & (kpos[:, None, :] >= 0)
    sc = jnp.einsum('bnqkgd,bnskd->bnkgqs', qb, kwin) * NSA_SCALE
    sc = sc - slopes[None, None, :, :, None, None] * dist[None, :, None, None]
    p = _masked_softmax(sc, mask[None, :, None, None])
    o = jnp.einsum('bnkgqs,bnskd->bnqkgd', p.astype(vwin.dtype), vwin)
    return o.reshape(b, s, kvh, g, hd)


def _window_dense(q, qpos, k, v, kpos, slopes):
    dist = qpos[:, None] - kpos[None, :]
    mask = (dist >= 0) & (dist <= WINDOW) & (kpos[None, :] >= 0)
    sc = jnp.einsum('bqkgd,bskd->bkgqs', q, k) * NSA_SCALE - slopes[None, :, :, None, None] * dist
    p = _masked_softmax(sc, mask)
    return jnp.einsum('bkgqs,bskd->bqkgd', p.astype(v.dtype), v)


def _nsa_merge(o_c, o_s, o_w, gates, w_out):
    b, t = o_c.shape[:2]
    o = gates[..., 0, None] * o_c + gates[..., 1, None] * o_s + gates[..., 2, None] * o_w
    return o.astype(w_out.dtype).reshape(b, t, NSA_Q_DIM) @ w_out


def _nsa_prompt(x, w_in, pool, pe, phi, w_out, slopes):
    b, s, _ = x.shape
    q, kv, gates = _nsa_project(x, w_in)
    ck, ck_end = _compress(kv[:, :, 0], pool[0], pe[0], phi[0])
    cv, _ = _compress(kv[:, :, 1], pool[1], pe[1], phi[1])
    kb = _to_blocks(kv[:, :, 2])
    vb = _to_blocks(kv[:, :, 3])
    qb_len = SEL_QBLOCK if s % SEL_QBLOCK == 0 else s
    n_qb = s // qb_len
    qs = jnp.moveaxis(q.reshape(b, n_qb, qb_len, NSA_KV_HEADS, NSA_GROUP, NSA_HEAD_DIM), 1, 0)
    ps = jnp.arange(s).reshape(n_qb, qb_len)
    o_c, o_s = lax.map(lambda a: _cmp_slc(a[0], a[1], ck, cv, ck_end, kb, vb, slopes), (qs, ps))
    o_c = jnp.moveaxis(o_c, 0, 1).reshape(q.shape)
    o_s = jnp.moveaxis(o_s, 0, 1).reshape(q.shape)
    o_w = _window_banded(q, kv[:, :, 4], kv[:, :, 5], slopes)
    y = _nsa_merge(o_c, o_s, o_w, gates, w_out)
    rows = kv[:, :, :4].reshape(b, s // PAGE_SIZE, PAGE_SIZE, 4, NSA_KV_HEADS, NSA_HEAD_DIM)
    win = kv[:, s - min(WINDOW, s):, 4:]
    return y, rows, win


def _nsa_sample(x, cache_l, win_l, page_table, w_in, pool, pe, phi, w_out, slopes):
    db, t, _ = x.shape
    q, kv, gates = _nsa_project(x, w_in)
    past = cache_l[page_table]
    p_len = past.shape[1] * past.shape[2]
    full = jnp.concatenate([past.reshape(db, p_len, 4, NSA_KV_HEADS, NSA_HEAD_DIM), kv[:, :, :4]], axis=1)
    qpos = p_len + jnp.arange(t)
    ck, ck_end = _compress(full[:, :, 0], pool[0], pe[0], phi[0])
    cv, _ = _compress(full[:, :, 1], pool[1], pe[1], phi[1])
    kb = _to_blocks(full[:, :, 2])
    vb = _to_blocks(full[:, :, 3])
    o_c, o_s = _cmp_slc(q, qpos, ck, cv, ck_end, kb, vb, slopes)
    wb = win_l.shape[1]
    wbuf = jnp.concatenate([win_l, kv[:, :, 4:]], axis=1)
    kpos = p_len - wb + jnp.arange(wb + t)
    o_w = _window_dense(q, qpos, wbuf[:, :, 0], wbuf[:, :, 1], kpos, slopes)
    y = _nsa_merge(o_c, o_s, o_w, gates, w_out)
    return y, kv[:, :, :4], wbuf[:, wbuf.shape[1] - wb:]


def _s5_discretize(a_re, a_im, log_dt, b_re, b_im):
    f32 = jnp.float32
    a_re, a_im = a_re.astype(f32), a_im.astype(f32)
    b_re, b_im = b_re.astype(f32), b_im.astype(f32)
    dt = jnp.exp(log_dt.astype(f32))[:, None]
    mag = jnp.exp(a_re * dt)
    ab_re = mag * jnp.cos(a_im * dt)
    ab_im = mag * jnp.sin(a_im * dt)
    den = a_re * a_re + a_im * a_im
    nr = ab_re - 1.0
    f_re = (nr * a_re + ab_im * a_im) / den
    f_im = (ab_im * a_re - nr * a_im) / den
    bb_re = f_re[..., None] * b_re - f_im[..., None] * b_im
    bb_im = f_re[..., None] * b_im + f_im[..., None] * b_re
    return ab_re, ab_im, bb_re, bb_im


def _s5_combine(e1, e2):
    a1r, a1i, b1r, b1i = e1
    a2r, a2i, b2r, b2i = e2
    return (a1r * a2r - a1i * a2i, a1r * a2i + a1i * a2r,
            a2r * b1r - a2i * b1i + b2r, a2r * b1i + a2i * b1r + b2i)


def _s5_scan(bu_re, bu_im, ab_re, ab_im, h_re, h_im):
    b, t, g, p = bu_re.shape
    c = S5_CHUNK if t % S5_CHUNK == 0 else t
    xr = jnp.moveaxis(bu_re.reshape(b, t // c, c, g, p), 1, 0)
    xi = jnp.moveaxis(bu_im.reshape(b, t // c, c, g, p), 1, 0)

    def step(carry, chunk):
        hr, hi = carry
        br, bi = chunk
        br = br.at[:, 0].add(ab_re * hr - ab_im * hi)
        bi = bi.at[:, 0].add(ab_re * hi + ab_im * hr)
        ar = jnp.broadcast_to(ab_re, br.shape)
        ai = jnp.broadcast_to(ab_im, bi.shape)
        _, _, sr, si = lax.associative_scan(_s5_combine, (ar, ai, br, bi), axis=1)
        return (sr[:, -1], si[:, -1]), (sr, si)

    (hr, hi), (sr, si) = lax.scan(step, (h_re, h_im), (xr, xi))
    sr = jnp.moveaxis(sr, 0, 1).reshape(b, t, g, p)
    si = jnp.moveaxis(si, 0, 1).reshape(b, t, g, p)
    return sr, si, hr, hi


def _s5_mixer(x, h0, w_in, a_re, a_im, log_dt, b_re, b_im, c_re, c_im, d, w_glu):
    f32 = jnp.float32
    bsz, t, dm = x.shape
    u = (x @ w_in).astype(f32)
    ug = u.reshape(bsz, t, S5_GROUPS, S5_GROUP_CH)
    ab_re, ab_im, bb_re, bb_im = _s5_discretize(a_re, a_im, log_dt, b_re, b_im)
    bu_re = jnp.einsum('btgc,gpc->btgp', ug, bb_re)
    bu_im = jnp.einsum('btgc,gpc->btgp', ug, bb_im)
    h0f = h0.astype(f32)
    sr, si, hr, hi = _s5_scan(bu_re, bu_im, ab_re, ab_im, h0f[:, 0], h0f[:, 1])
    y = (jnp.einsum('btgp,gcp->btgc', sr, c_re.astype(f32))
         - jnp.einsum('btgp,gcp->btgc', si, c_im.astype(f32)))
    y = y.reshape(bsz, t, dm) + d.astype(f32) * u
    z = jax.nn.gelu(y).astype(w_glu.dtype) @ w_glu
    out = z[..., :dm] * jax.nn.sigmoid(z[..., dm:])
    return out.astype(x.dtype), jnp.stack([hr, hi], axis=1).astype(h0.dtype)


def _peer(x, w_q, sub_keys, u_tab, v_tab):
    b, t, dm = x.shape
    n = b * t
    xf = x.reshape(n, dm)
    q = (xf @ w_q).reshape(n, PEER_HEADS, 2, PEER_HALF)
    s = jnp.einsum('nhcd,hckd->nhck', q, sub_keys).astype(jnp.float32)
    sv, si = lax.top_k(s, PEER_TOPK)
    cand = (sv[:, :, 0, :, None] + sv[:, :, 1, None, :]).reshape(n, PEER_HEADS, PEER_TOPK * PEER_TOPK)
    cs, ci = lax.top_k(cand, PEER_TOPK)
    e1 = jnp.take_along_axis(si[:, :, 0], ci // PEER_TOPK, axis=-1)
    e2 = jnp.take_along_axis(si[:, :, 1], ci % PEER_TOPK, axis=-1)
    experts = (e1 * PEER_N_KEYS + e2).reshape(n, PEER_HEADS * PEER_TOPK)
    gate = jax.nn.softmax(cs, axis=-1).reshape(n, PEER_HEADS * PEER_TOPK).astype(x.dtype)
    c = PEER_CHUNK if n >= PEER_CHUNK else n
    pad = (-n) % c
    xp = jnp.pad(xf, ((0, pad), (0, 0))).reshape(-1, c, dm)
    ep = jnp.pad(experts, ((0, pad), (0, 0))).reshape(-1, c, PEER_HEADS * PEER_TOPK)
    gp = jnp.pad(gate, ((0, pad), (0, 0))).reshape(-1, c, PEER_HEADS * PEER_TOPK)

    def chunk(args):
        xc, ec, gc = args
        act = jax.nn.gelu(jnp.einsum('cd,ced->ce', xc, u_tab[ec]))
        return jnp.einsum('ce,ced->cd', gc * act, v_tab[ec])

    out = lax.map(chunk, (xp, ep, gp)).reshape(-1, dm)[:n]
    return out.reshape(b, t, dm)


def setup_inputs(seed: int = 0) -> dict:
    key = jax.random.key(seed)
    ks = jax.random.split(key, 27)
    f32 = jnp.float32
    beta = (8.0 * DEPTH) ** -0.25

    def nrm(k, shape, scale):
        return jax.random.normal(k, shape, f32) * scale

    n_pages = PAST_LEN // PAGE_SIZE
    n_used = DEC_BATCH * n_pages
    n_phys = n_used + max(1, n_used // 4)
    perm = jax.random.permutation(ks[0], n_phys)
    page_table = perm[:n_used].reshape(DEC_BATCH, n_pages).astype(jnp.int32)
    win_len = min(WINDOW, PAST_LEN)
    glu_scale = jnp.where(jnp.arange(2 * D_MODEL) < D_MODEL, beta, 1.0) * D_MODEL ** -0.5
    a_im = jnp.pi * jnp.arange(S5_STATE, dtype=f32)
    return {
        'x_prompt': nrm(ks[1], (BATCH, SEQ, D_MODEL), 1.0),
        'x_sample': nrm(ks[2], (DEC_BATCH, DEC_SEQ, D_MODEL), 1.0),
        'cache_nsa': nrm(ks[3], (N_NSA_LAYERS, n_phys, PAGE_SIZE, 4, NSA_KV_HEADS, NSA_HEAD_DIM), 1.0),
        'state_win': nrm(ks[4], (N_NSA_LAYERS, DEC_BATCH, win_len, 2, NSA_KV_HEADS, NSA_HEAD_DIM), 1.0),
        'state_s5': nrm(ks[5], (N_S5_LAYERS, DEC_BATCH, 2, S5_GROUPS, S5_STATE), 0.5),
        'page_table': page_table,
        'nsa_w_in': nrm(ks[6], (N_NSA_LAYERS, D_MODEL, NSA_IN_DIM), D_MODEL ** -0.5),
        'nsa_cmp_pool': (1.0 + nrm(ks[7], (N_NSA_LAYERS, 2, NSA_KV_HEADS, CMP_BLOCK), 0.5)) * CMP_BLOCK ** -0.5,
        'nsa_cmp_pe': nrm(ks[8], (N_NSA_LAYERS, 2, NSA_KV_HEADS, CMP_BLOCK, NSA_HEAD_DIM), 0.1),
        'nsa_cmp_phi': nrm(ks[9], (N_NSA_LAYERS, 2, NSA_KV_HEADS, NSA_HEAD_DIM, NSA_HEAD_DIM), NSA_HEAD_DIM ** -0.5),
        'nsa_w_out': nrm(ks[10], (N_NSA_LAYERS, NSA_Q_DIM, D_MODEL), beta * NSA_Q_DIM ** -0.5),
        's5_w_in': nrm(ks[11], (N_S5_LAYERS, D_MODEL, D_MODEL), D_MODEL ** -0.5),
        's5_a_re': -0.5 + nrm(ks[12], (N_S5_LAYERS, S5_GROUPS, S5_STATE), 0.01),
        's5_a_im': a_im + nrm(ks[13], (N_S5_LAYERS, S5_GROUPS, S5_STATE), 0.01),
        's5_log_dt': jax.random.uniform(ks[14], (N_S5_LAYERS, S5_GROUPS), f32, math.log(1e-3), math.log(1e-1)),
        's5_b_re': nrm(ks[15], (N_S5_LAYERS, S5_GROUPS, S5_STATE, S5_GROUP_CH), (2.0 * S5_GROUP_CH) ** -0.5),
        's5_b_im': nrm(ks[16], (N_S5_LAYERS, S5_GROUPS, S5_STATE, S5_GROUP_CH), (2.0 * S5_GROUP_CH) ** -0.5),
        's5_c_re': nrm(ks[17], (N_S5_LAYERS, S5_GROUPS, S5_GROUP_CH, S5_STATE), (2.0 * S5_STATE) ** -0.5),
        's5_c_im': nrm(ks[18], (N_S5_LAYERS, S5_GROUPS, S5_GROUP_CH, S5_STATE), (2.0 * S5_STATE) ** -0.5),
        's5_d': nrm(ks[19], (N_S5_LAYERS, D_MODEL), 0.5),
        's5_w_glu': jax.random.normal(ks[20], (N_S5_LAYERS, D_MODEL, 2 * D_MODEL), f32) * glu_scale,
        'peer_w_q': nrm(ks[21], (DEPTH, D_MODEL, PEER_HEADS * PEER_KEY_DIM), D_MODEL ** -0.5),
        'peer_sub_keys': nrm(ks[22], (DEPTH, PEER_HEADS, 2, PEER_N_KEYS, PEER_HALF), PEER_HALF ** -0.5),
        'peer_u': nrm(ks[23], (DEPTH, PEER_N_EXPERTS, D_MODEL), D_MODEL ** -0.5),
        'peer_v': nrm(ks[24], (DEPTH, PEER_N_EXPERTS, D_MODEL), beta * 0.5),
        'ln_g': 1.0 + nrm(ks[25], (DEPTH, 2, D_MODEL), 0.01),
        'ln_b': nrm(ks[26], (DEPTH, 2, D_MODEL), 0.01),
    }


def reference(x_prompt, x_sample, cache_nsa, state_win, state_s5, page_table,
              nsa_w_in, nsa_cmp_pool, nsa_cmp_pe, nsa_cmp_phi, nsa_w_out,
              s5_w_in, s5_a_re, s5_a_im, s5_log_dt, s5_b_re, s5_b_im, s5_c_re, s5_c_im, s5_d, s5_w_glu,
              peer_w_q, peer_sub_keys, peer_u, peer_v, ln_g, ln_b):
    alpha = (2.0 * DEPTH) ** 0.25
    slopes = _alibi_slopes()
    xp, xs = x_prompt, x_sample
    rows_p, rows_s, win_p, win_s, s5_p, s5_s = [], [], [], [], [], []
    for layer in range(DEPTH):
        j = layer // N_MIXERS
        if layer % N_MIXERS == 0:
            mp, rp, wp = _nsa_prompt(xp, nsa_w_in[j], nsa_cmp_pool[j], nsa_cmp_pe[j], nsa_cmp_phi[j],
                                     nsa_w_out[j], slopes)
            ms, rs, ws = _nsa_sample(xs, cache_nsa[j], state_win[j], page_table, nsa_w_in[j],
                                     nsa_cmp_pool[j], nsa_cmp_pe[j], nsa_cmp_phi[j], nsa_w_out[j], slopes)
            rows_p.append(rp)
            rows_s.append(rs)
            win_p.append(wp)
            win_s.append(ws)
        else:
            h0 = jnp.zeros((xp.shape[0], 2, S5_GROUPS, S5_STATE), state_s5.dtype)
            s5_args = (s5_w_in[j], s5_a_re[j], s5_a_im[j], s5_log_dt[j], s5_b_re[j], s5_b_im[j],
                       s5_c_re[j], s5_c_im[j], s5_d[j], s5_w_glu[j])
            mp, hp = _s5_mixer(xp, h0, *s5_args)
            ms, hs = _s5_mixer(xs, state_s5[j], *s5_args)
            s5_p.append(hp)
            s5_s.append(hs)
        xp = _layernorm(alpha * xp + mp, ln_g[layer, 0], ln_b[layer, 0])
        xs = _layernorm(alpha * xs + ms, ln_g[layer, 0], ln_b[layer, 0])
        xp = _layernorm(alpha * xp + _peer(xp, peer_w_q[layer], peer_sub_keys[layer], peer_u[layer], peer_v[layer]),
                        ln_g[layer, 1], ln_b[layer, 1])
        xs = _layernorm(alpha * xs + _peer(xs, peer_w_q[layer], peer_sub_keys[layer], peer_u[layer], peer_v[layer]),
                        ln_g[layer, 1], ln_b[layer, 1])
    return (xp, xs, jnp.stack(rows_p), jnp.stack(rows_s), jnp.stack(win_p), jnp.stack(win_s),
            jnp.stack(s5_p), jnp.stack(s5_s))
```

```python
import functools
import math

import numpy as np
import jax
import jax.numpy as jnp
from jax import lax
from jax.experimental import pallas as pl
from jax.experimental.pallas import tpu as pltpu

F32 = jnp.float32
BF16 = jnp.bfloat16

D_MODEL = 2048
DEPTH = 4
PAGE_SIZE = 128
N_MIXERS = 2
NSA_HEADS = 16
NSA_KV_HEADS = 4
NSA_GROUP = NSA_HEADS // NSA_KV_HEADS
NSA_HEAD_DIM = D_MODEL // NSA_HEADS
NSA_Q_DIM = NSA_HEADS * NSA_HEAD_DIM
NSA_KV_DIM = NSA_KV_HEADS * NSA_HEAD_DIM
NSA_MAIN_DIM = NSA_Q_DIM + 6 * NSA_KV_DIM
NSA_N_GATES = 3 * NSA_HEADS
NSA_SCALE = NSA_HEAD_DIM ** -0.5
CMP_BLOCK = 32
CMP_STRIDE = 16
SLC_BLOCK = 64
SLC_TOP_N = 16
WINDOW = 512
FORCE_BONUS = 1.0e4
S5_GROUP_CH = 16
S5_GROUPS = D_MODEL // S5_GROUP_CH
S5_STATE = 64
PEER_HEADS = 8
PEER_N_KEYS = 128
PEER_N_EXPERTS = PEER_N_KEYS ** 2
PEER_HALF = 128
PEER_TOPK = 16
LN_EPS = 1e-5
NEG_INF = -1.0e30
ALPHA = (2.0 * DEPTH) ** 0.25

LANES = 128
SUBLANES = 8
VMEM_LIMIT_BYTES = 56 * 1024 * 1024

CMP_CHUNK = 2048
CMP_ROWS = CMP_CHUNK // CMP_STRIDE
SAMPLE_PAGES_PER_STEP = CMP_CHUNK // PAGE_SIZE

_SLOPES = [2.0 ** (-8.0 * (h + 1) / NSA_HEADS) for h in range(NSA_HEADS)]


def _cparams(*sem):
    return pltpu.CompilerParams(dimension_semantics=sem, vmem_limit_bytes=VMEM_LIMIT_BYTES)


def _dot(a, b):
    return jnp.dot(a, b, preferred_element_type=F32)


def _dot_nt(a, b):
    return lax.dot_general(a, b, (((1,), (1,)), ((), ())), preferred_element_type=F32)


def _split3(x):
    hi = x.astype(BF16)
    r1 = x - hi.astype(F32)
    mid = r1.astype(BF16)
    lo = (r1 - mid.astype(F32)).astype(BF16)
    return hi, mid, lo


def _gelu_tanh(x):
    c = math.sqrt(2.0 / math.pi)
    return 0.5 * x * (1.0 + jnp.tanh(c * (x + 0.044715 * (x * x * x))))


def _mm_kernel(a_ref, b_ref, o_ref):
    o_ref[...] = _dot(a_ref[...], b_ref[...]).astype(o_ref.dtype)


def _mm(a, b, tm=512, tn=1024, out_dtype=F32):
    m, k = a.shape
    n = b.shape[1]
    tm = min(tm, m)
    tn = min(tn, n)
    assert m % tm == 0 and n % tn == 0
    return pl.pallas_call(
        _mm_kernel,
        grid=(n // tn, m // tm),
        in_specs=[pl.BlockSpec((tm, k), lambda j, i: (i, 0)),
                  pl.BlockSpec((k, tn), lambda j, i: (0, j))],
        out_specs=pl.BlockSpec((tm, tn), lambda j, i: (i, j)),
        out_shape=jax.ShapeDtypeStruct((m, n), out_dtype),
        compiler_params=_cparams("parallel", "parallel"),
        name="mm",
    )(a, b)


def _ln_body(y, g_ref, b_ref, o_ref):
    mu = jnp.mean(y, axis=-1, keepdims=True)
    yc = y - mu
    var = jnp.mean(yc * yc, axis=-1, keepdims=True)
    o_ref[...] = yc * lax.rsqrt(var + LN_EPS) * g_ref[...] + b_ref[...]


def _ln_kernel(x_ref, m_ref, g_ref, b_ref, o_ref):
    _ln_body(ALPHA * x_ref[...] + m_ref[...], g_ref, b_ref, o_ref)


def _ln_glu_kernel(x_ref, z1_ref, z2_ref, g_ref, b_ref, o_ref):
    z2 = z2_ref[...]
    mix = z1_ref[...] * (1.0 / (1.0 + jnp.exp(-z2)))
    _ln_body(ALPHA * x_ref[...] + mix, g_ref, b_ref, o_ref)


def _ln_res(x, m, g, b, glu=False):
    n, d = x.shape
    tm = min(256, n)
    assert n % tm == 0
    row = pl.BlockSpec((tm, d), lambda i: (i, 0))
    vec = pl.BlockSpec((1, d), lambda i: (0, 0))
    if glu:
        ins = [row, row, pl.BlockSpec((tm, d), lambda i: (i, 1)), vec, vec]
        args = (x, m, m, g.reshape(1, d), b.reshape(1, d))
        kern = _ln_glu_kernel
    else:
        ins = [row, row, vec, vec]
        args = (x, m, g.reshape(1, d), b.reshape(1, d))
        kern = _ln_kernel
    return pl.pallas_call(
        kern, grid=(n // tm,), in_specs=ins, out_specs=row,
        out_shape=jax.ShapeDtypeStruct((n, d), F32),
        compiler_params=_cparams("parallel"), name="ln_res",
    )(*args)


def _compress_consts(pool, pe, phi):
    r = np.arange(CMP_ROWS)[:, None]
    c = np.arange(CMP_CHUNK)[None, :]
    off = c - (CMP_STRIDE * r - CMP_STRIDE)
    band = (off >= 0) & (off < CMP_BLOCK)
    offc = np.clip(off, 0, CMP_BLOCK - 1)
    pb = jnp.where(band[None, None], pool[:, :, offc], 0.0)
    t = np.arange(CMP_STRIDE)[None, :]
    first = (np.arange(CMP_ROWS)[:, None] == 0)
    pbt = jnp.where((first & (t >= 0))[None, None], pool[:, :, None, :CMP_STRIDE], 0.0)
    pe_term = jnp.einsum('skj,skjd->skd', pool, pe)
    return (pb.astype(BF16), pbt.astype(BF16), pe_term.reshape(2 * NSA_KV_HEADS, NSA_HEAD_DIM).astype(F32),
            phi.astype(BF16))


def _compress_chunk(x, tail, pb_ref, pbt_ref, pe_ref, phi_ref):
    outs = []
    for slot in range(2):
        for kv in range(NSA_KV_HEADS):
            lo = slot * NSA_KV_DIM + kv * NSA_HEAD_DIM
            xs = x[:, lo:lo + NSA_HEAD_DIM]
            pooled = _dot(pb_ref[slot, kv], xs) + _dot(pbt_ref[slot, kv], tail[:, lo:lo + NSA_HEAD_DIM])
            pooled = pooled + pe_ref[pl.ds(slot * NSA_KV_HEADS + kv, 1), :]
            outs.append(_dot(pooled.astype(BF16), phi_ref[slot, kv]))
    return jnp.concatenate(outs, axis=1)


def _slope_col(kv, rows, per):
    gi = lax.broadcasted_iota(jnp.int32, (rows, 1), 0) // per
    col = jnp.full((rows, 1), _SLOPES[kv * NSA_GROUP + NSA_GROUP - 1], F32)
    for g in range(NSA_GROUP - 1):
        col = jnp.where(gi == g, _SLOPES[kv * NSA_GROUP + g], col)
    return col


def _select_top(score, axis, n_entries):
    idx = lax.broadcasted_iota(jnp.int32, score.shape, axis).astype(F32)
    sel = jnp.zeros(score.shape, F32)
    cur = score
    for _ in range(SLC_TOP_N):
        m = jnp.max(cur, axis=axis, keepdims=True)
        first = jnp.min(jnp.where(cur == m, idx, float(n_entries)), axis=axis, keepdims=True)
        hit = idx == first
        sel = jnp.where(hit, 1.0, sel)
        cur = jnp.where(hit, -2.0, cur)
    return sel


def _compress_kernel(x_ref, pb_ref, pbt_ref, pe_ref, phi_ref, o_ref, tail_ref):
    @pl.when(pl.program_id(1) == 0)
    def _():
        tail_ref[...] = jnp.zeros_like(tail_ref)

    x = x_ref[0]
    o_ref[0] = _compress_chunk(x, tail_ref[...], pb_ref, pbt_ref, pe_ref, phi_ref).astype(o_ref.dtype)
    tail_ref[...] = x[CMP_CHUNK - CMP_STRIDE:, :]


def _compress(kv01, consts):
    b, s, w = kv01.shape
    n_chunks = s // CMP_CHUNK
    pb, pbt, pe_term, phi = consts
    full = lambda a: pl.BlockSpec(a.shape, lambda i, c: (0,) * a.ndim)
    return pl.pallas_call(
        _compress_kernel,
        grid=(b, n_chunks),
        in_specs=[pl.BlockSpec((1, CMP_CHUNK, w), lambda i, c: (i, c, 0)),
                  full(pb), full(pbt), full(pe_term), full(phi)],
        out_specs=pl.BlockSpec((1, CMP_ROWS, w), lambda i, c: (i, c, 0)),
        out_shape=jax.ShapeDtypeStruct((b, s // CMP_STRIDE, w), BF16),
        scratch_shapes=[pltpu.VMEM((CMP_STRIDE, w), BF16)],
        compiler_params=_cparams("parallel", "arbitrary"),
        name="nsa_compress",
    )(kv01, pb, pbt, pe_term, phi)


CMP_TQ = 128


def _cmp_prompt_kernel(q_ref, ckv_ref, mt_ref, oc_ref, sel_ref):
    tq = CMP_TQ
    n_r = ckv_ref.shape[1]
    n_slc = mt_ref.shape[0]
    q0 = pl.program_id(1) * tq
    qpos = q0 + lax.broadcasted_iota(jnp.int32, (tq, n_r), 0)
    r = lax.broadcasted_iota(jnp.int32, (tq, n_r), 1)
    dist_i = qpos - (CMP_STRIDE * r + CMP_STRIDE - 1)
    mask = (dist_i >= 0) & (r >= 1)
    maskf = mask.astype(F32)
    dist = dist_i.astype(F32)
    q = q_ref[0]
    ckv = ckv_ref[0]
    blk = lax.broadcasted_iota(jnp.int32, (n_slc, tq), 0)
    cur = (q0 + lax.broadcasted_iota(jnp.int32, (n_slc, tq), 1)) // SLC_BLOCK
    avail = blk <= cur
    forced = (blk == 0) | (blk == cur) | (blk == cur - 1)
    for kv in range(NSA_KV_HEADS):
        ck = ckv[:, kv * NSA_HEAD_DIM:(kv + 1) * NSA_HEAD_DIM]
        cv = ckv[:, NSA_KV_DIM + kv * NSA_HEAD_DIM:NSA_KV_DIM + (kv + 1) * NSA_HEAD_DIM]
        qs = jnp.concatenate([q[:, (kv * NSA_GROUP + g) * NSA_HEAD_DIM:(kv * NSA_GROUP + g + 1) * NSA_HEAD_DIM]
                              for g in range(NSA_GROUP)], axis=0)
        s_all = _dot_nt(qs, ck)
        p_sum = jnp.zeros((tq, n_r), F32)
        for g in range(NSA_GROUP):
            h = kv * NSA_GROUP + g
            s = jnp.where(mask, s_all[g * tq:(g + 1) * tq] - _SLOPES[h] * dist, NEG_INF)
            e = jnp.exp(s - jnp.max(s, axis=-1, keepdims=True)) * maskf
            p = e / jnp.maximum(jnp.sum(e, axis=-1, keepdims=True), 1e-30)
            oc_ref[0, :, h * NSA_HEAD_DIM:(h + 1) * NSA_HEAD_DIM] = _dot(p.astype(BF16), cv)
            p_sum = p_sum + p
        mt = mt_ref[...]
        p_slc = sum(_dot_nt(mt, part) for part in _split3(p_sum))
        score = jnp.where(avail, p_slc + jnp.where(forced, FORCE_BONUS, 0.0), -1.0)
        sel_ref[0, kv] = _select_top(score, 0, n_slc)


def _slc_map(n_slc, n_r, n_pad):
    m = np.arange(n_pad)[:, None]
    r = np.arange(n_r)[None, :]
    ratio = SLC_BLOCK // CMP_STRIDE
    return jnp.asarray(((r >= ratio * m) & (r <= ratio * m + ratio) & (m < n_slc)).astype(np.float32), BF16)


def _cmp_prompt(qs, ckv):
    b, s, _ = qs.shape
    n_r = ckv.shape[1]
    n_slc = s // SLC_BLOCK
    mt = _slc_map(n_slc, n_r, n_slc)
    return pl.pallas_call(
        _cmp_prompt_kernel,
        grid=(b, s // CMP_TQ),
        in_specs=[pl.BlockSpec((1, CMP_TQ, NSA_Q_DIM), lambda i, t: (i, t, 0)),
                  pl.BlockSpec((1, n_r, 2 * NSA_KV_DIM), lambda i, t: (i, 0, 0)),
                  pl.BlockSpec(mt.shape, lambda i, t: (0, 0))],
        out_specs=[pl.BlockSpec((1, CMP_TQ, NSA_Q_DIM), lambda i, t: (i, t, 0)),
                   pl.BlockSpec((1, NSA_KV_HEADS, n_slc, CMP_TQ), lambda i, t: (i, 0, 0, t))],
        out_shape=[jax.ShapeDtypeStruct((b, s, NSA_Q_DIM), F32),
                   jax.ShapeDtypeStruct((b, NSA_KV_HEADS, n_slc, s), F32)],
        compiler_params=_cparams("parallel", "parallel"),
        name="nsa_cmp_prompt",
    )(qs, ckv, mt)


SLC_TQ = 256
SLC_TK = 512


def _slc_prompt_kernel(q_ref, k_ref, v_ref, sel_ref, o_ref, m_ref, l_ref, acc_ref):
    tq, tk = SLC_TQ, SLC_TK
    qi = pl.program_id(1)
    kj = pl.program_id(2)
    n_sel = sel_ref.shape[3]

    @pl.when(kj == 0)
    def _():
        m_ref[...] = jnp.full_like(m_ref, NEG_INF)
        l_ref[...] = jnp.zeros_like(l_ref)
        acc_ref[...] = jnp.zeros_like(acc_ref)

    @pl.when(kj * tk <= qi * tq + tq - 1)
    def _():
        q = q_ref[0]
        kt = k_ref[0]
        vt = v_ref[0]
        qpos = qi * tq + lax.broadcasted_iota(jnp.int32, (tq, tk), 0)
        kpos = kj * tk + lax.broadcasted_iota(jnp.int32, (tq, tk), 1)
        dist_i = qpos - kpos
        dist = dist_i.astype(F32)
        eb = lax.broadcasted_iota(jnp.int32, (n_sel, tk), 0)
        ec = (kj * tk + lax.broadcasted_iota(jnp.int32, (n_sel, tk), 1)) // SLC_BLOCK
        expand = jnp.where(eb == ec, 1.0, 0.0).astype(BF16)
        for kv in range(NSA_KV_HEADS):
            picked = _dot(sel_ref[0, kv], expand)
            mask = (picked > 0.5) & (dist_i >= 0)
            maskf = mask.astype(F32)
            qs = jnp.concatenate([q[:, (kv * NSA_GROUP + g) * NSA_HEAD_DIM:(kv * NSA_GROUP + g + 1) * NSA_HEAD_DIM]
                                  for g in range(NSA_GROUP)], axis=0)
            s_all = _dot_nt(qs, kt[:, kv * NSA_HEAD_DIM:(kv + 1) * NSA_HEAD_DIM])
            vv = vt[:, kv * NSA_HEAD_DIM:(kv + 1) * NSA_HEAD_DIM]
            for g in range(NSA_GROUP):
                h = kv * NSA_GROUP + g
                rows = slice(g * tq, (g + 1) * tq)
                s = jnp.where(mask, s_all[rows] - _SLOPES[h] * dist, NEG_INF)
                m_old = m_ref[kv, rows]
                m_new = jnp.maximum(m_old, jnp.max(s, axis=-1, keepdims=True))
                a = jnp.exp(m_old - m_new)
                e = jnp.exp(s - m_new) * maskf
                l_ref[kv, rows] = a * l_ref[kv, rows] + jnp.sum(e, axis=-1, keepdims=True)
                acc_ref[kv, rows] = a * acc_ref[kv, rows] + _dot(e.astype(BF16), vv)
                m_ref[kv, rows] = m_new

    @pl.when(kj == pl.num_programs(2) - 1)
    def _():
        for kv in range(NSA_KV_HEADS):
            for g in range(NSA_GROUP):
                h = kv * NSA_GROUP + g
                rows = slice(g * tq, (g + 1) * tq)
                o_ref[0, :, h * NSA_HEAD_DIM:(h + 1) * NSA_HEAD_DIM] = (
                    acc_ref[kv, rows] / jnp.maximum(l_ref[kv, rows], 1e-30))


def _slc_prompt(qs, k_slc, v_slc, sel):
    b, s, _ = qs.shape
    tq, tk = min(SLC_TQ, s), min(SLC_TK, s)
    assert tq == SLC_TQ and tk == SLC_TK
    n_sel = sel.shape[3]
    last = lambda t: (t * tq + tq - 1) // tk
    return pl.pallas_call(
        _slc_prompt_kernel,
        grid=(b, s // tq, s // tk),
        in_specs=[pl.BlockSpec((1, tq, NSA_Q_DIM), lambda i, t, j: (i, t, 0)),
                  pl.BlockSpec((1, tk, NSA_KV_DIM), lambda i, t, j: (i, jnp.minimum(j, last(t)), 0)),
                  pl.BlockSpec((1, tk, NSA_KV_DIM), lambda i, t, j: (i, jnp.minimum(j, last(t)), 0)),
                  pl.BlockSpec((1, NSA_KV_HEADS, tq, n_sel), lambda i, t, j: (i, 0, t, 0))],
        out_specs=pl.BlockSpec((1, tq, NSA_Q_DIM), lambda i, t, j: (i, t, 0)),
        out_shape=jax.ShapeDtypeStruct((b, s, NSA_Q_DIM), F32),
        scratch_shapes=[pltpu.VMEM((NSA_KV_HEADS, NSA_GROUP * tq, 1), F32),
                        pltpu.VMEM((NSA_KV_HEADS, NSA_GROUP * tq, 1), F32),
                        pltpu.VMEM((NSA_KV_HEADS, NSA_GROUP * tq, NSA_HEAD_DIM), F32)],
        compiler_params=_cparams("parallel", "parallel", "arbitrary"),
        name="nsa_slc_prompt",
    )(qs, k_slc, v_slc, sel)


WIN_TQ = 256
WIN_NT = WINDOW // WIN_TQ + 1


def _win_prompt_kernel(q_ref, *refs):
    tq = WIN_TQ
    k_refs = refs[:WIN_NT]
    v_refs = refs[WIN_NT:2 * WIN_NT]
    o_ref = refs[2 * WIN_NT]
    qi = pl.program_id(1)
    nk = WIN_NT * tq
    row = lax.broadcasted_iota(jnp.int32, (tq, nk), 0)
    col = lax.broadcasted_iota(jnp.int32, (tq, nk), 1)
    dist_i = WINDOW + row - col
    kpos = (qi - (WIN_NT - 1)) * tq + col
    mask = (dist_i >= 0) & (dist_i <= WINDOW) & (kpos >= 0)
    maskf = mask.astype(F32)
    dist = dist_i.astype(F32)
    q = q_ref[0]
    kcat = jnp.concatenate([r[0] for r in k_refs], axis=0)
    vcat = jnp.concatenate([r[0] for r in v_refs], axis=0)
    for kv in range(NSA_KV_HEADS):
        qs = jnp.concatenate([q[:, (kv * NSA_GROUP + g) * NSA_HEAD_DIM:(kv * NSA_GROUP + g + 1) * NSA_HEAD_DIM]
                              for g in range(NSA_GROUP)], axis=0)
        s_all = _dot_nt(qs, kcat[:, kv * NSA_HEAD_DIM:(kv + 1) * NSA_HEAD_DIM])
        vv = vcat[:, kv * NSA_HEAD_DIM:(kv + 1) * NSA_HEAD_DIM]
        for g in range(NSA_GROUP):
            h = kv * NSA_GROUP + g
            s = jnp.where(mask, s_all[g * tq:(g + 1) * tq] - _SLOPES[h] * dist, NEG_INF)
            e = jnp.exp(s - jnp.max(s, axis=-1, keepdims=True)) * maskf
            den = jnp.maximum(jnp.sum(e, axis=-1, keepdims=True), 1e-30)
            o_ref[0, :, h * NSA_HEAD_DIM:(h + 1) * NSA_HEAD_DIM] = _dot(e.astype(BF16), vv) / den


def _win_prompt(qs, k_win, v_win):
    b, s, _ = qs.shape
    tq = WIN_TQ
    kspec = lambda d: pl.BlockSpec((1, tq, NSA_KV_DIM),
                                   lambda i, t: (i, jnp.maximum(t - (WIN_NT - 1) + d, 0), 0))
    return pl.pallas_call(
        _win_prompt_kernel,
        grid=(b, s // tq),
        in_specs=[pl.BlockSpec((1, tq, NSA_Q_DIM), lambda i, t: (i, t, 0))]
                 + [kspec(d) for d in range(WIN_NT)] + [kspec(d) for d in range(WIN_NT)],
        out_specs=pl.BlockSpec((1, tq, NSA_Q_DIM), lambda i, t: (i, t, 0)),
        out_shape=jax.ShapeDtypeStruct((b, s, NSA_Q_DIM), F32),
        compiler_params=_cparams("parallel", "parallel"),
        name="nsa_win_prompt",
    )(qs, *([k_win] * WIN_NT), *([v_win] * WIN_NT))


def _merge_kernel(oc_ref, os_ref, ow_ref, hg_ref, ex_ref, o_ref):
    gate = 1.0 / (1.0 + jnp.exp(-hg_ref[...]))
    parts = _split3(gate)[:2]
    out = None
    for br, ref in enumerate((oc_ref, os_ref, ow_ref)):
        gx = sum(_dot(p, ex_ref[br]) for p in parts)
        term = gx * ref[...]
        out = term if out is None else out + term
    o_ref[...] = out.astype(o_ref.dtype)


def _gate_expand():
    e = np.zeros((3, LANES, NSA_Q_DIM), np.float32)
    for br in range(3):
        for h in range(NSA_HEADS):
            e[br, h * 3 + br, h * NSA_HEAD_DIM:(h + 1) * NSA_HEAD_DIM] = 1.0
    return jnp.asarray(e, BF16)


def _merge(oc, os_, ow, hg):
    n = oc.shape[0]
    tm = min(256, n)
    ex = _gate_expand()
    row = pl.BlockSpec((tm, NSA_Q_DIM), lambda i: (i, 0))
    return pl.pallas_call(
        _merge_kernel, grid=(n // tm,),
        in_specs=[row, row, row, pl.BlockSpec((tm, LANES), lambda i: (i, 0)),
                  pl.BlockSpec(ex.shape, lambda i: (0, 0, 0))],
        out_specs=row,
        out_shape=jax.ShapeDtypeStruct((n, NSA_Q_DIM), BF16),
        compiler_params=_cparams("parallel"), name="nsa_merge",
    )(oc, os_, ow, hg, ex)


def _nsa_weights(w_in, pool, pe, phi, w_out):
    w_main = w_in[:, :NSA_MAIN_DIM].astype(BF16)
    w_gate = jnp.pad(w_in[:, NSA_MAIN_DIM:], ((0, 0), (0, LANES - NSA_N_GATES))).astype(BF16)
    return w_main, w_gate, _compress_consts(pool, pe, phi), w_out.astype(BF16)


def _nsa_prompt(x, weights):
    b, s, _ = x.shape
    w_main, w_gate, consts, w_out = weights
    xb = x.reshape(b * s, D_MODEL).astype(BF16)
    h = _mm(xb, w_main)
    hg = _mm(xb, w_gate)
    h3 = h.reshape(b, s, NSA_MAIN_DIM)
    qs = (h3[..., :NSA_Q_DIM] * NSA_SCALE).astype(BF16)
    kvb = h3[..., NSA_Q_DIM:].astype(BF16)
    ckv = _compress(kvb[..., :2 * NSA_KV_DIM], consts)
    o_c, sel_t = _cmp_prompt(qs, ckv)
    sel = jnp.swapaxes(sel_t, 2, 3).astype(BF16)
    o_s = _slc_prompt(qs, kvb[..., 2 * NSA_KV_DIM:3 * NSA_KV_DIM], kvb[..., 3 * NSA_KV_DIM:4 * NSA_KV_DIM], sel)
    o_w = _win_prompt(qs, kvb[..., 4 * NSA_KV_DIM:5 * NSA_KV_DIM], kvb[..., 5 * NSA_KV_DIM:])
    o = _merge(o_c.reshape(b * s, NSA_Q_DIM), o_s.reshape(b * s, NSA_Q_DIM), o_w.reshape(b * s, NSA_Q_DIM), hg)
    y = _mm(o, w_out).reshape(b, s, D_MODEL)
    kv = h3[..., NSA_Q_DIM:].reshape(b, s, 6, NSA_KV_HEADS, NSA_HEAD_DIM)
    rows = kv[:, :, :4].reshape(b, s // PAGE_SIZE, PAGE_SIZE, 4, NSA_KV_HEADS, NSA_HEAD_DIM)
    win = kv[:, s - min(WINDOW, s):, 4:]
    return y, rows, win


def _gather_pages(page_refs):
    return jnp.concatenate([r[0].astype(BF16) for r in page_refs], axis=0)


def _cmp_sample_kernel(pt_ref, *refs, t_new, p_len):
    del pt_ref
    npg = SAMPLE_PAGES_PER_STEP
    page_refs = refs[:npg]
    q_ref, pb_ref, pbt_ref, pe_ref, phi_ref, mt_ref, oc_ref, sel_ref, tail_ref, s_ref, cv_ref = refs[npg:]
    c = pl.program_id(1)
    rows = NSA_GROUP * t_new
    n_r = s_ref.shape[2]

    @pl.when(c == 0)
    def _():
        tail_ref[...] = jnp.zeros_like(tail_ref)

    x = _gather_pages(page_refs)
    ckv = _compress_chunk(x, tail_ref[...], pb_ref, pbt_ref, pe_ref, phi_ref)
    tail_ref[...] = x[CMP_CHUNK - CMP_STRIDE:, :]
    r0 = pl.multiple_of(c * CMP_ROWS, CMP_ROWS)
    cv_ref[pl.ds(r0, CMP_ROWS), :] = ckv[:, NSA_KV_DIM:].astype(BF16)
    for kv in range(NSA_KV_HEADS):
        ck = ckv[:, kv * NSA_HEAD_DIM:(kv + 1) * NSA_HEAD_DIM].astype(BF16)
        s_ref[kv, :, pl.ds(r0, CMP_ROWS)] = _dot_nt(q_ref[0, kv], ck)

    @pl.when(c == pl.num_programs(1) - 1)
    def _():
        ri = lax.broadcasted_iota(jnp.int32, (rows, n_r), 1)
        ti = lax.broadcasted_iota(jnp.int32, (rows, n_r), 0) % t_new
        dist_i = (p_len + ti) - (CMP_STRIDE * ri + CMP_STRIDE - 1)
        mask = (dist_i >= 0) & (ri >= 1)
        maskf = mask.astype(F32)
        dist = dist_i.astype(F32)
        n_pad = mt_ref.shape[1]
        blk = lax.broadcasted_iota(jnp.int32, (rows, n_pad), 1)
        cur = (p_len + lax.broadcasted_iota(jnp.int32, (rows, n_pad), 0) % t_new) // SLC_BLOCK
        n_slc = -(-(p_len + t_new) // SLC_BLOCK)
        avail = blk <= cur
        forced = (blk == 0) | (blk == cur) | (blk == cur - 1)
        gi = lax.broadcasted_iota(jnp.int32, (rows, rows), 0) % t_new
        gj = lax.broadcasted_iota(jnp.int32, (rows, rows), 1) % t_new
        gsum = jnp.where(gi == gj, 1.0, 0.0).astype(BF16)
        for kv in range(NSA_KV_HEADS):
            s = jnp.where(mask, s_ref[kv] - _slope_col(kv, rows, t_new) * dist, NEG_INF)
            e = jnp.exp(s - jnp.max(s, axis=-1, keepdims=True)) * maskf
            p = e / jnp.maximum(jnp.sum(e, axis=-1, keepdims=True), 1e-30)
            oc_ref[0, kv] = _dot(p.astype(BF16), cv_ref[:, kv * NSA_HEAD_DIM:(kv + 1) * NSA_HEAD_DIM])
            p_sum = sum(_dot(gsum, part) for part in _split3(p))
            p_slc = sum(_dot(part, mt_ref[...]) for part in _split3(p_sum))
            score = jnp.where(avail, p_slc + jnp.where(forced, FORCE_BONUS, 0.0), -1.0)
            score = jnp.where(blk < n_slc, score, -3.0)
            sel_ref[0, kv] = _select_top(score, 1, n_pad)


def _page_spec(half, d):
    npg = SAMPLE_PAGES_PER_STEP
    return pl.BlockSpec((1, PAGE_SIZE, 2 * NSA_KV_DIM), lambda i, c, pt: (pt[i, c * npg + d], 0, half))


def _cmp_sample(cache2d, page_table, q16, consts, t_new):
    db, n_pages = page_table.shape
    p_len = n_pages * PAGE_SIZE
    n_chunks = p_len // CMP_CHUNK
    n_r = p_len // CMP_STRIDE
    n_slc = -(-(p_len + t_new) // SLC_BLOCK)
    n_pad = -(-n_slc // LANES) * LANES
    mt = _slc_map(n_slc, n_r, n_pad).T
    pb, pbt, pe_term, phi = consts
    rows = NSA_GROUP * t_new
    full = lambda a: pl.BlockSpec(a.shape, lambda i, c, pt: (0,) * a.ndim)
    gs = pltpu.PrefetchScalarGridSpec(
        num_scalar_prefetch=1,
        grid=(db, n_chunks),
        in_specs=[_page_spec(0, d) for d in range(SAMPLE_PAGES_PER_STEP)]
                 + [pl.BlockSpec((1, NSA_KV_HEADS, rows, NSA_HEAD_DIM), lambda i, c, pt: (i, 0, 0, 0)),
                    full(pb), full(pbt), full(pe_term), full(phi), full(mt)],
        out_specs=[pl.BlockSpec((1, NSA_KV_HEADS, rows, NSA_HEAD_DIM), lambda i, c, pt: (i, 0, 0, 0)),
                   pl.BlockSpec((1, NSA_KV_HEADS, rows, n_pad), lambda i, c, pt: (i, 0, 0, 0))],
        scratch_shapes=[pltpu.VMEM((CMP_STRIDE, 2 * NSA_KV_DIM), BF16),
                        pltpu.VMEM((NSA_KV_HEADS, rows, n_r), F32),
                        pltpu.VMEM((n_r, NSA_KV_DIM), BF16)],
    )
    return pl.pallas_call(
        functools.partial(_cmp_sample_kernel, t_new=t_new, p_len=p_len),
        grid_spec=gs,
        out_shape=[jax.ShapeDtypeStruct((db, NSA_KV_HEADS, rows, NSA_HEAD_DIM), F32),
                   jax.ShapeDtypeStruct((db, NSA_KV_HEADS, rows, n_pad), F32)],
        compiler_params=_cparams("parallel", "arbitrary"),
        name="nsa_cmp_sample",
    )(page_table, *([cache2d] * SAMPLE_PAGES_PER_STEP), q16, pb, pbt, pe_term, phi, mt)


def _online_update(s, maskf, vv, m_ref, l_ref, acc_ref, kv):
    m_old = m_ref[kv]
    m_new = jnp.maximum(m_old, jnp.max(s, axis=-1, keepdims=True))
    a = jnp.exp(m_old - m_new)
    e = jnp.exp(s - m_new) * maskf
    l_ref[kv] = a * l_ref[kv] + jnp.sum(e, axis=-1, keepdims=True)
    acc_ref[kv] = a * acc_ref[kv] + _dot(e.astype(BF16), vv)
    m_ref[kv] = m_new


def _slc_sample_kernel(pt_ref, *refs, t_new, p_len):
    del pt_ref
    npg = SAMPLE_PAGES_PER_STEP
    page_refs = refs[:npg]
    q_ref, sel_ref, kn_ref, vn_ref, o_ref, m_ref, l_ref, acc_ref = refs[npg:]
    c = pl.program_id(1)
    rows = NSA_GROUP * t_new
    n_pad = sel_ref.shape[3]

    @pl.when(c == 0)
    def _():
        m_ref[...] = jnp.full_like(m_ref, NEG_INF)
        l_ref[...] = jnp.zeros_like(l_ref)
        acc_ref[...] = jnp.zeros_like(acc_ref)

    x = _gather_pages(page_refs)
    eb = lax.broadcasted_iota(jnp.int32, (n_pad, CMP_CHUNK), 0)
    ec = (c * CMP_CHUNK + lax.broadcasted_iota(jnp.int32, (n_pad, CMP_CHUNK), 1)) // SLC_BLOCK
    expand = jnp.where(eb == ec, 1.0, 0.0).astype(BF16)
    ti = lax.broadcasted_iota(jnp.int32, (rows, CMP_CHUNK), 0) % t_new
    kpos = c * CMP_CHUNK + lax.broadcasted_iota(jnp.int32, (rows, CMP_CHUNK), 1)
    dist = ((p_len + ti) - kpos).astype(F32)
    for kv in range(NSA_KV_HEADS):
        mask = _dot(sel_ref[0, kv], expand) > 0.5
        s = _dot_nt(q_ref[0, kv], x[:, kv * NSA_HEAD_DIM:(kv + 1) * NSA_HEAD_DIM])
        s = jnp.where(mask, s - _slope_col(kv, rows, t_new) * dist, NEG_INF)
        _online_update(s, mask.astype(F32), x[:, NSA_KV_DIM + kv * NSA_HEAD_DIM:NSA_KV_DIM + (kv + 1) * NSA_HEAD_DIM],
                       m_ref, l_ref, acc_ref, kv)

    @pl.when(c == pl.num_programs(1) - 1)
    def _():
        n_new = kn_ref.shape[1]
        ti2 = lax.broadcasted_iota(jnp.int32, (rows, n_new), 0) % t_new
        ci = lax.broadcasted_iota(jnp.int32, (rows, n_new), 1)
        mask2 = (ci <= ti2) & (ci < t_new)
        dist2 = (ti2 - ci).astype(F32)
        for kv in range(NSA_KV_HEADS):
            s = _dot_nt(q_ref[0, kv], kn_ref[0, :, kv * NSA_HEAD_DIM:(kv + 1) * NSA_HEAD_DIM])
            s = jnp.where(mask2, s - _slope_col(kv, rows, t_new) * dist2, NEG_INF)
            _online_update(s, mask2.astype(F32), vn_ref[0, :, kv * NSA_HEAD_DIM:(kv + 1) * NSA_HEAD_DIM],
                           m_ref, l_ref, acc_ref, kv)
            o_ref[0, kv] = acc_ref[kv] / jnp.maximum(l_ref[kv], 1e-30)


def _slc_sample(cache2d, page_table, q16, sel16, k_new, v_new, t_new):
    db, n_pages = page_table.shape
    p_len = n_pages * PAGE_SIZE
    assert p_len % SLC_BLOCK == 0
    n_chunks = p_len // CMP_CHUNK
    rows = NSA_GROUP * t_new
    n_pad = sel16.shape[3]
    per_b = lambda a: pl.BlockSpec((1,) + a.shape[1:], lambda i, c, pt: (i,) + (0,) * (a.ndim - 1))
    gs = pltpu.PrefetchScalarGridSpec(
        num_scalar_prefetch=1,
        grid=(db, n_chunks),
        in_specs=[_page_spec(1, d) for d in range(SAMPLE_PAGES_PER_STEP)]
                 + [per_b(q16), per_b(sel16), per_b(k_new), per_b(v_new)],
        out_specs=pl.BlockSpec((1, NSA_KV_HEADS, rows, NSA_HEAD_DIM), lambda i, c, pt: (i, 0, 0, 0)),
        scratch_shapes=[pltpu.VMEM((NSA_KV_HEADS, rows, 1), F32),
                        pltpu.VMEM((NSA_KV_HEADS, rows, 1), F32),
                        pltpu.VMEM((NSA_KV_HEADS, rows, NSA_HEAD_DIM), F32)],
    )
    return pl.pallas_call(
        functools.partial(_slc_sample_kernel, t_new=t_new, p_len=p_len),
        grid_spec=gs,
        out_shape=jax.ShapeDtypeStruct((db, NSA_KV_HEADS, rows, NSA_HEAD_DIM), F32),
        compiler_params=_cparams("parallel", "arbitrary"),
        name="nsa_slc_sample",
    )(page_table, *([cache2d] * SAMPLE_PAGES_PER_STEP), q16, sel16, k_new, v_new)


def _win_sample_kernel(q_ref, wk_ref, wv_ref, kn_ref, vn_ref, o_ref, *, t_new):
    rows = NSA_GROUP * t_new
    wb = wk_ref.shape[1]
    n_new = kn_ref.shape[1]
    t1 = lax.broadcasted_iota(jnp.int32, (rows, wb), 0) % t_new
    c1 = lax.broadcasted_iota(jnp.int32, (rows, wb), 1)
    d1 = wb + t1 - c1
    mask1 = (d1 >= 0) & (d1 <= WINDOW)
    t2 = lax.broadcasted_iota(jnp.int32, (rows, n_new), 0) % t_new
    c2 = lax.broadcasted_iota(jnp.int32, (rows, n_new), 1)
    d2 = t2 - c2
    mask2 = (d2 >= 0) & (d2 <= WINDOW) & (c2 < t_new)
    for kv in range(NSA_KV_HEADS):
        lanes = slice(kv * NSA_HEAD_DIM, (kv + 1) * NSA_HEAD_DIM)
        slope = _slope_col(kv, rows, t_new)
        s1 = jnp.where(mask1, _dot_nt(q_ref[0, kv], wk_ref[0, :, lanes]) - slope * d1.astype(F32), NEG_INF)
        s2 = jnp.where(mask2, _dot_nt(q_ref[0, kv], kn_ref[0, :, lanes]) - slope * d2.astype(F32), NEG_INF)
        m = jnp.maximum(jnp.max(s1, axis=-1, keepdims=True), jnp.max(s2, axis=-1, keepdims=True))
        e1 = jnp.exp(s1 - m) * mask1.astype(F32)
        e2 = jnp.exp(s2 - m) * mask2.astype(F32)
        den = jnp.maximum(jnp.sum(e1, axis=-1, keepdims=True) + jnp.sum(e2, axis=-1, keepdims=True), 1e-30)
        o_ref[0, kv] = (_dot(e1.astype(BF16), wv_ref[0, :, lanes]) + _dot(e2.astype(BF16), vn_ref[0, :, lanes])) / den


def _win_sample(q16, wk, wv, k_new, v_new, t_new):
    db = q16.shape[0]
    rows = NSA_GROUP * t_new
    per_b = lambda a: pl.BlockSpec((1,) + a.shape[1:], lambda i: (i,) + (0,) * (a.ndim - 1))
    return pl.pallas_call(
        functools.partial(_win_sample_kernel, t_new=t_new),
        grid=(db,),
        in_specs=[per_b(q16), per_b(wk), per_b(wv), per_b(k_new), per_b(v_new)],
        out_specs=pl.BlockSpec((1, NSA_KV_HEADS, rows, NSA_HEAD_DIM), lambda i: (i, 0, 0, 0)),
        out_shape=jax.ShapeDtypeStruct((db, NSA_KV_HEADS, rows, NSA_HEAD_DIM), F32),
        compiler_params=_cparams("parallel"),
        name="nsa_win_sample",
    )(q16, wk, wv, k_new, v_new)


def _pad_rows(a, n):
    return jnp.pad(a, ((0, 0), (0, n - a.shape[1]), (0, 0)))


def _nsa_sample(x, cache_l, win_l, page_table, weights):
    db, t, _ = x.shape
    w_main, w_gate, consts, w_out = weights
    xb = x.reshape(db * t, D_MODEL).astype(BF16)
    h = _mm(xb, w_main)
    hg = _mm(xb, w_gate)
    h3 = h.reshape(db, t, NSA_MAIN_DIM)
    q16 = (h3[..., :NSA_Q_DIM] * NSA_SCALE).astype(BF16).reshape(db, t, NSA_KV_HEADS, NSA_GROUP, NSA_HEAD_DIM)
    q16 = q16.transpose(0, 2, 3, 1, 4).reshape(db, NSA_KV_HEADS, NSA_GROUP * t, NSA_HEAD_DIM)
    kvn = h3[..., NSA_Q_DIM:].reshape(db, t, 6, NSA_KV_DIM)
    new = lambda slot: _pad_rows(kvn[:, :, slot].astype(BF16), LANES)
    cache2d = cache_l.reshape(cache_l.shape[0], PAGE_SIZE, 4 * NSA_KV_DIM)
    o_c, sel16 = _cmp_sample(cache2d, page_table, q16, consts, t)
    o_s = _slc_sample(cache2d, page_table, q16, sel16.astype(BF16), new(2), new(3), t)
    wb = win_l.shape[1]
    wkv = win_l.astype(BF16).reshape(db, wb, 2, NSA_KV_DIM)
    o_w = _win_sample(q16, wkv[:, :, 0], wkv[:, :, 1], new(4), new(5), t)
    unrow = lambda o: o.reshape(db, NSA_KV_HEADS, NSA_GROUP, t, NSA_HEAD_DIM).transpose(0, 3, 1, 2, 4).reshape(
        db * t, NSA_Q_DIM)
    o = _merge(unrow(o_c), unrow(o_s), unrow(o_w), hg)
    y = _mm(o, w_out).reshape(db, t, D_MODEL)
    kv = h3[..., NSA_Q_DIM:].reshape(db, t, 6, NSA_KV_HEADS, NSA_HEAD_DIM)
    wbuf = jnp.concatenate([win_l, kv[:, :, 4:]], axis=1)
    return y, kv[:, :, :4], wbuf[:, wbuf.shape[1] - wb:]


S5_SEQS = SUBLANES
S5_CB_GROUPS = 16
S5_CB_STATES = S5_CB_GROUPS * S5_STATE
S5_CB_CH = S5_CB_GROUPS * S5_GROUP_CH
S5_N_CB = S5_GROUPS // S5_CB_GROUPS
S5_MAX_STEPS = 128


def _s5_discretize(a_re, a_im, log_dt, b_re, b_im):
    dt = jnp.exp(log_dt.astype(F32))[:, None]
    mag = jnp.exp(a_re * dt)
    ab_re = mag * jnp.cos(a_im * dt)
    ab_im = mag * jnp.sin(a_im * dt)
    den = a_re * a_re + a_im * a_im
    nr = ab_re - 1.0
    f_re = (nr * a_re + ab_im * a_im) / den
    f_im = (ab_im * a_re - nr * a_im) / den
    bb_re = f_re[..., None] * b_re - f_im[..., None] * b_im
    bb_im = f_re[..., None] * b_im + f_im[..., None] * b_re
    return ab_re, ab_im, bb_re, bb_im


def _s5_weights(w_in, a_re, a_im, log_dt, b_re, b_im, c_re, c_im, d, w_glu):
    ab_re, ab_im, bb_re, bb_im = _s5_discretize(a_re, a_im, log_dt, b_re, b_im)
    eye = jnp.eye(S5_CB_GROUPS, dtype=F32)

    def bd_in(bb):
        x = bb.reshape(S5_N_CB, S5_CB_GROUPS, S5_STATE, S5_GROUP_CH)
        return jnp.einsum('ngpc,gh->ngchp', x, eye).reshape(S5_N_CB, S5_CB_CH, S5_CB_STATES)

    def bd_out(cc):
        x = cc.reshape(S5_N_CB, S5_CB_GROUPS, S5_GROUP_CH, S5_STATE)
        return jnp.einsum('ngcp,gh->ngphc', x, eye).reshape(S5_N_CB, S5_CB_STATES, S5_CB_CH)

    bmat = jnp.concatenate([bd_in(bb_re), bd_in(bb_im)], axis=2).astype(BF16)
    cmat = jnp.concatenate([bd_out(c_re.astype(F32)), -bd_out(c_im.astype(F32))], axis=1).astype(BF16)
    rep = lambda a: jnp.broadcast_to(a.reshape(S5_N_CB, 1, S5_CB_STATES), (S5_N_CB, S5_SEQS, S5_CB_STATES))
    return dict(w_in=w_in.astype(BF16), w_glu=w_glu.astype(BF16), bmat=bmat, cmat=cmat,
                ar=rep(ab_re), ai=rep(ab_im), ab_re=ab_re, ab_im=ab_im, d=d.reshape(1, D_MODEL).astype(F32))


def _s5_scan_kernel(u_ref, b_ref, c_ref, ar_ref, ai_ref, d_ref, h0r_ref, h0i_ref,
                    y_ref, hr_ref, hi_ref, bu_ref, *, steps):
    ns = S5_CB_STATES

    @pl.when(pl.program_id(1) == 0)
    def _():
        hr_ref[...] = h0r_ref[...]
        hi_ref[...] = h0i_ref[...]

    u = u_ref[...]
    bu_ref[...] = _dot(u.astype(BF16), b_ref[0])
    ar = ar_ref[0]
    ai = ai_ref[0]

    def step(s, carry):
        hr, hi = carry
        r0 = pl.multiple_of(s * S5_SEQS, S5_SEQS)
        br = bu_ref[pl.ds(r0, S5_SEQS), :ns]
        bi = bu_ref[pl.ds(r0, S5_SEQS), ns:]
        nr = ar * hr - ai * hi + br
        ni = ar * hi + ai * hr + bi
        bu_ref[pl.ds(r0, S5_SEQS), :ns] = nr
        bu_ref[pl.ds(r0, S5_SEQS), ns:] = ni
        return nr, ni

    hr, hi = lax.fori_loop(0, steps, step, (hr_ref[0], hi_ref[0]))
    hr_ref[0] = hr
    hi_ref[0] = hi
    y_ref[...] = _dot(bu_ref[...].astype(BF16), c_ref[0]) + d_ref[...] * u


def _s5_scan(u_rows, w, h0r, h0i):
    n_rows = u_rows.shape[0]
    n_steps = n_rows // S5_SEQS
    steps = min(S5_MAX_STEPS, n_steps)
    assert n_steps % steps == 0
    tr = steps * S5_SEQS
    cb_spec = lambda a: pl.BlockSpec((1,) + a.shape[1:], lambda cb, t: (cb,) + (0,) * (a.ndim - 1))
    return pl.pallas_call(
        functools.partial(_s5_scan_kernel, steps=steps),
        grid=(S5_N_CB, n_steps // steps),
        in_specs=[pl.BlockSpec((tr, S5_CB_CH), lambda cb, t: (t, cb)),
                  cb_spec(w['bmat']), cb_spec(w['cmat']), cb_spec(w['ar']), cb_spec(w['ai']),
                  pl.BlockSpec((1, S5_CB_CH), lambda cb, t: (0, cb)),
                  cb_spec(h0r), cb_spec(h0i)],
        out_specs=[pl.BlockSpec((tr, S5_CB_CH), lambda cb, t: (t, cb)), cb_spec(h0r), cb_spec(h0i)],
        out_shape=[jax.ShapeDtypeStruct((n_rows, D_MODEL), F32),
                   jax.ShapeDtypeStruct(h0r.shape, F32), jax.ShapeDtypeStruct(h0i.shape, F32)],
        scratch_shapes=[pltpu.VMEM((tr, 2 * S5_CB_STATES), F32)],
        compiler_params=_cparams("parallel", "arbitrary"),
        name="s5_scan",
    )(u_rows, w['bmat'], w['cmat'], w['ar'], w['ai'], w['d'], h0r, h0i)


def _state_to_blocks(h):
    return h.reshape(h.shape[0], S5_N_CB, S5_CB_STATES).transpose(1, 0, 2)


def _blocks_to_state(h):
    return h.transpose(1, 0, 2).reshape(h.shape[1], S5_GROUPS, S5_STATE)


def _cpow2(re, im, n):
    for _ in range(n):
        re, im = re * re - im * im, 2.0 * re * im
    return re, im


def _s5_prompt(x, w):
    b, t, _ = x.shape
    n_seg = S5_SEQS // b
    seg = t // n_seg
    assert n_seg * b == S5_SEQS and seg & (seg - 1) == 0
    xr = x.reshape(b, n_seg, seg, D_MODEL).transpose(2, 0, 1, 3).reshape(t * b, D_MODEL)
    u = _mm(xr.astype(BF16), w['w_in'])
    zero = jnp.zeros((S5_N_CB, S5_SEQS, S5_CB_STATES), F32)
    _, er, ei = _s5_scan(u, w, zero, zero)
    er = _blocks_to_state(er).reshape(b, n_seg, S5_GROUPS, S5_STATE)
    ei = _blocks_to_state(ei).reshape(b, n_seg, S5_GROUPS, S5_STATE)
    pr, pi = _cpow2(w['ab_re'], w['ab_im'], int(math.log2(seg)))
    sr = [jnp.zeros((b, S5_GROUPS, S5_STATE), F32)]
    si = [jnp.zeros((b, S5_GROUPS, S5_STATE), F32)]
    for k in range(n_seg - 1):
        sr.append(er[:, k] + pr * sr[k] - pi * si[k])
        si.append(ei[:, k] + pr * si[k] + pi * sr[k])
    h0r = _state_to_blocks(jnp.stack(sr, axis=1).reshape(S5_SEQS, S5_GROUPS, S5_STATE))
    h0i = _state_to_blocks(jnp.stack(si, axis=1).reshape(S5_SEQS, S5_GROUPS, S5_STATE))
    y, hr, hi = _s5_scan(u, w, h0r, h0i)
    hr = _blocks_to_state(hr).reshape(b, n_seg, S5_GROUPS, S5_STATE)[:, -1]
    hi = _blocks_to_state(hi).reshape(b, n_seg, S5_GROUPS, S5_STATE)[:, -1]
    z = _mm(_gelu_rows(y), w['w_glu'])
    unperm = lambda a: a.reshape(seg, b, n_seg, -1).transpose(1, 2, 0, 3).reshape(b * t, -1)
    return unperm(z), jnp.stack([hr, hi], axis=1)


def _s5_sample(x, h0, w):
    db, t, _ = x.shape
    assert db == S5_SEQS
    xr = x.transpose(1, 0, 2).reshape(t * db, D_MODEL)
    u = _mm(xr.astype(BF16), w['w_in'])
    y, hr, hi = _s5_scan(u, w, _state_to_blocks(h0[:, 0].astype(F32)), _state_to_blocks(h0[:, 1].astype(F32)))
    z = _mm(_gelu_rows(y), w['w_glu'])
    z = z.reshape(t, db, -1).transpose(1, 0, 2).reshape(db * t, -1)
    return z, jnp.stack([_blocks_to_state(hr), _blocks_to_state(hi)], axis=1).astype(h0.dtype)


def _gelu_kernel(y_ref, o_ref):
    o_ref[...] = _gelu_tanh(y_ref[...]).astype(o_ref.dtype)


def _gelu_rows(y):
    n, d = y.shape
    tm = min(256, n)
    row = pl.BlockSpec((tm, d), lambda i: (i, 0))
    return pl.pallas_call(_gelu_kernel, grid=(n // tm,), in_specs=[row], out_specs=row,
                          out_shape=jax.ShapeDtypeStruct((n, d), BF16),
                          compiler_params=_cparams("parallel"), name="gelu")(y)


ROUTER_TN = 256
N_RANKED = PEER_TOPK + 1
_CAND_PAIRS = [(i, j) for i in range(N_RANKED) for j in range(N_RANKED) if (i + 1) * (j + 1) <= N_RANKED]


def _top_values(s, n):
    vals = []
    cur = s
    for _ in range(n):
        m = jnp.max(cur, axis=0, keepdims=True)
        vals.append(m)
        cur = jnp.where(cur == m, -jnp.inf, cur)
    return vals


def _router_kernel(x_ref, wq_ref, keys_ref, w1z_ref, wthr_ref, w2_ref):
    q = _dot(x_ref[...], wq_ref[...])
    for h in range(PEER_HEADS):
        s, top = [], []
        for c in range(2):
            hc = 2 * h + c
            sc = _dot_nt(keys_ref[hc], q[:, hc * PEER_HALF:(hc + 1) * PEER_HALF].astype(BF16))
            s.append(sc)
            top.append(_top_values(sc, N_RANKED))
        cand = jnp.concatenate([top[0][i] + top[1][j] for i, j in _CAND_PAIRS], axis=0)
        ranked = _top_values(cand, N_RANKED)
        tau = 0.5 * (ranked[PEER_TOPK - 1] + ranked[PEER_TOPK])
        m1, m2 = top[0][0], top[1][0]
        z = jnp.sum(jnp.where(cand >= tau, jnp.exp(cand - (m1 + m2)), 0.0), axis=0, keepdims=True)
        w1z_ref[h] = jnp.exp(s[0] - m1) / z
        wthr_ref[h] = jnp.exp((tau - s[0]) - m2)
        w2_ref[h] = jnp.exp(s[1] - m2)


def _router(x, wq, keys):
    n = x.shape[0]
    tn = min(ROUTER_TN, n)
    assert n % tn == 0
    out = jax.ShapeDtypeStruct((PEER_HEADS, PEER_N_KEYS, n), F32)
    ospec = pl.BlockSpec((PEER_HEADS, PEER_N_KEYS, tn), lambda i: (0, 0, i))
    return pl.pallas_call(
        _router_kernel, grid=(n // tn,),
        in_specs=[pl.BlockSpec((tn, D_MODEL), lambda i: (i, 0)),
                  pl.BlockSpec(wq.shape, lambda i: (0, 0)),
                  pl.BlockSpec(keys.shape, lambda i: (0, 0, 0))],
        out_specs=[ospec, ospec, ospec], out_shape=[out, out, out],
        compiler_params=_cparams("parallel"), name="peer_router",
    )(x, wq, keys)


EXPERT_TN = 512
EXPERT_TE = 512


def _expert_kernel(xt_ref, u_ref, vt_ref, w1z_ref, wthr_ref, w2_ref, o_ref):
    j = pl.program_id(1)
    tn = xt_ref.shape[1]

    @pl.when(j == 0)
    def _():
        o_ref[...] = jnp.zeros_like(o_ref)

    ht = _dot(u_ref[...], xt_ref[...])
    parts = []
    for cc in range(EXPERT_TE // PEER_N_KEYS):
        c = j * (EXPERT_TE // PEER_N_KEYS) + cc
        g = jnp.zeros((PEER_N_KEYS, tn), F32)
        for h in range(PEER_HEADS):
            w2 = w2_ref[h]
            g = g + w1z_ref[h, pl.ds(c, 1), :] * jnp.where(w2 >= wthr_ref[h, pl.ds(c, 1), :], w2, 0.0)
        act = _gelu_tanh(ht[cc * PEER_N_KEYS:(cc + 1) * PEER_N_KEYS])
        parts.append((act * g).astype(BF16))
    o_ref[...] += _dot(vt_ref[...], jnp.concatenate(parts, axis=0))


def _expert(xt, u, vt, w1z, wthr, w2):
    n = xt.shape[1]
    tn = min(EXPERT_TN, n)
    assert n % tn == 0
    rspec = pl.BlockSpec((PEER_HEADS, PEER_N_KEYS, tn), lambda i, j: (0, 0, i))
    return pl.pallas_call(
        _expert_kernel, grid=(n // tn, PEER_N_EXPERTS // EXPERT_TE),
        in_specs=[pl.BlockSpec((D_MODEL, tn), lambda i, j: (0, i)),
                  pl.BlockSpec((EXPERT_TE, D_MODEL), lambda i, j: (j, 0)),
                  pl.BlockSpec((D_MODEL, EXPERT_TE), lambda i, j: (0, j)),
                  rspec, rspec, rspec],
        out_specs=pl.BlockSpec((D_MODEL, tn), lambda i, j: (0, i)),
        out_shape=jax.ShapeDtypeStruct((D_MODEL, n), F32),
        compiler_params=_cparams("parallel", "arbitrary"), name="peer_expert",
    )(xt, u, vt, w1z, wthr, w2)


def _peer_weights(w_q, sub_keys, u_tab, v_tab):
    return (w_q.astype(BF16), sub_keys.reshape(2 * PEER_HEADS, PEER_N_KEYS, PEER_HALF).astype(BF16),
            u_tab.astype(BF16), v_tab.T.astype(BF16))


def _peer(x, weights):
    wq, keys, u, vt = weights
    n = x.shape[0]
    n_pad = -(-n // LANES) * LANES
    xb = jnp.pad(x, ((0, n_pad - n), (0, 0))).astype(BF16)
    w1z, wthr, w2 = _router(xb, wq, keys)
    out_t = _expert(xb.T, u, vt, w1z, wthr, w2)
    return out_t.T[:n]


def kernel(x_prompt, x_sample, cache_nsa, state_win, state_s5, page_table, nsa_w_in, nsa_cmp_pool, nsa_cmp_pe,
           nsa_cmp_phi, nsa_w_out, s5_w_in, s5_a_re, s5_a_im, s5_log_dt, s5_b_re, s5_b_im, s5_c_re, s5_c_im, s5_d,
           s5_w_glu, peer_w_q, peer_sub_keys, peer_u, peer_v, ln_g, ln_b):
    b, s, _ = x_prompt.shape
    db, t, _ = x_sample.shape
    xp = x_prompt.reshape(b * s, D_MODEL)
    xs = x_sample.reshape(db * t, D_MODEL)
    rows_p, rows_s, win_p, win_s, s5_p, s5_s = [], [], [], [], [], []
    for layer in range(DEPTH):
        j = layer // N_MIXERS
        if layer % N_MIXERS == 0:
            w = _nsa_weights(nsa_w_in[j], nsa_cmp_pool[j], nsa_cmp_pe[j], nsa_cmp_phi[j], nsa_w_out[j])
            mp, rp, wp = _nsa_prompt(xp.reshape(b, s, D_MODEL), w)
            ms, rs, ws = _nsa_sample(xs.reshape(db, t, D_MODEL), cache_nsa[j], state_win[j], page_table, w)
            rows_p.append(rp)
            rows_s.append(rs)
            win_p.append(wp)
            win_s.append(ws)
            xp = _ln_res(xp, mp.reshape(b * s, D_MODEL), ln_g[layer, 0], ln_b[layer, 0])
            xs = _ln_res(xs, ms.reshape(db * t, D_MODEL), ln_g[layer, 0], ln_b[layer, 0])
        else:
            w = _s5_weights(s5_w_in[j], s5_a_re[j], s5_a_im[j], s5_log_dt[j], s5_b_re[j], s5_b_im[j],
                            s5_c_re[j], s5_c_im[j], s5_d[j], s5_w_glu[j])
            zp, hp = _s5_prompt(xp.reshape(b, s, D_MODEL), w)
            zs, hs = _s5_sample(xs.reshape(db, t, D_MODEL), state_s5[j], w)
            s5_p.append(hp.astype(state_s5.dtype))
            s5_s.append(hs)
            xp = _ln_res(xp, zp, ln_g[layer, 0], ln_b[layer, 0], glu=True)
            xs = _ln_res(xs, zs, ln_g[layer, 0], ln_b[layer, 0], glu=True)
        pw = _peer_weights(peer_w_q[layer], peer_sub_keys[layer], peer_u[layer], peer_v[layer])
        xp = _ln_res(xp, _peer(xp, pw), ln_g[layer, 1], ln_b[layer, 1])
        xs = _ln_res(xs, _peer(xs, pw), ln_g[layer, 1], ln_b[layer, 1])
    return (xp.reshape(b, s, D_MODEL), xs.reshape(db, t, D_MODEL), jnp.stack(rows_p), jnp.stack(rows_s),
            jnp.stack(win_p), jnp.stack(win_s), jnp.stack(s5_p), jnp.stack(s5_s))
```

```python
import functools
import math

import numpy as np
import jax
import jax.numpy as jnp
from jax import lax
from jax.experimental import pallas as pl
from jax.experimental.pallas import tpu as pltpu

F32 = jnp.float32
BF16 = jnp.bfloat16

D_MODEL = 2048
DEPTH = 4
PAGE_SIZE = 128
N_MIXERS = 2
NSA_HEADS = 16
NSA_KV_HEADS = 4
NSA_GROUP = NSA_HEADS // NSA_KV_HEADS
NSA_HEAD_DIM = D_MODEL // NSA_HEADS
NSA_Q_DIM = NSA_HEADS * NSA_HEAD_DIM
NSA_KV_DIM = NSA_KV_HEADS * NSA_HEAD_DIM
NSA_MAIN_DIM = NSA_Q_DIM + 6 * NSA_KV_DIM
NSA_N_GATES = 3 * NSA_HEADS
NSA_SCALE = NSA_HEAD_DIM ** -0.5
CMP_BLOCK = 32
CMP_STRIDE = 16
SLC_BLOCK = 64
SLC_TOP_N = 16
WINDOW = 512
FORCE_BONUS = 1.0e4
S5_GROUP_CH = 16
S5_GROUPS = D_MODEL // S5_GROUP_CH
S5_STATE = 64
PEER_HEADS = 8
PEER_N_KEYS = 128
PEER_N_EXPERTS = PEER_N_KEYS ** 2
PEER_HALF = 128
PEER_TOPK = 16
LN_EPS = 1e-5
NEG_INF = -1.0e30
ALPHA = (2.0 * DEPTH) ** 0.25

LANES = 128
SUBLANES = 8
VMEM_LIMIT_BYTES = 56 * 1024 * 1024

CMP_CHUNK = 2048
CMP_ROWS = CMP_CHUNK // CMP_STRIDE
SAMPLE_PAGES_PER_STEP = CMP_CHUNK // PAGE_SIZE

_SLOPES = [2.0 ** (-8.0 * (h + 1) / NSA_HEADS) for h in range(NSA_HEADS)]


def _cparams(*sem):
    return pltpu.CompilerParams(dimension_semantics=sem, vmem_limit_bytes=VMEM_LIMIT_BYTES)


def _dot(a, b):
    return jnp.dot(a, b, preferred_element_type=F32)


def _dot_nt(a, b):
    return lax.dot_general(a, b, (((1,), (1,)), ((), ())), preferred_element_type=F32)


def _split3(x):
    hi = x.astype(BF16)
    r1 = x - hi.astype(F32)
    mid = r1.astype(BF16)
    lo = (r1 - mid.astype(F32)).astype(BF16)
    return hi, mid, lo


def _gelu_tanh(x):
    c = math.sqrt(2.0 / math.pi)
    return 0.5 * x * (1.0 + jnp.tanh(c * (x + 0.044715 * (x * x * x))))


def _mm_kernel(a_ref, b_ref, o_ref):
    o_ref[...] = _dot(a_ref[...], b_ref[...]).astype(o_ref.dtype)


def _mm(a, b, tm=512, tn=1024, out_dtype=F32):
    m, k = a.shape
    n = b.shape[1]
    tm = min(tm, m)
    tn = min(tn, n)
    assert m % tm == 0 and n % tn == 0
    return pl.pallas_call(
        _mm_kernel,
        grid=(n // tn, m // tm),
        in_specs=[pl.BlockSpec((tm, k), lambda j, i: (i, 0)),
                  pl.BlockSpec((k, tn), lambda j, i: (0, j))],
        out_specs=pl.BlockSpec((tm, tn), lambda j, i: (i, j)),
        out_shape=jax.ShapeDtypeStruct((m, n), out_dtype),
        compiler_params=_cparams("parallel", "parallel"),
        name="mm",
    )(a, b)


def _ln_body(y, g_ref, b_ref, o_ref):
    mu = jnp.mean(y, axis=-1, keepdims=True)
    yc = y - mu
    var = jnp.mean(yc * yc, axis=-1, keepdims=True)
    o_ref[...] = yc * lax.rsqrt(var + LN_EPS) * g_ref[...] + b_ref[...]


def _ln_kernel(x_ref, m_ref, g_ref, b_ref, o_ref):
    _ln_body(ALPHA * x_ref[...] + m_ref[...], g_ref, b_ref, o_ref)


def _ln_glu_kernel(x_ref, z1_ref, z2_ref, g_ref, b_ref, o_ref):
    z2 = z2_ref[...]
    mix = z1_ref[...] * (1.0 / (1.0 + jnp.exp(-z2)))
    _ln_body(ALPHA * x_ref[...] + mix, g_ref, b_ref, o_ref)


def _ln_res(x, m, g, b, glu=False):
    n, d = x.shape
    tm = min(256, n)
    assert n % tm == 0
    row = pl.BlockSpec((tm, d), lambda i: (i, 0))
    vec = pl.BlockSpec((1, d), lambda i: (0, 0))
    if glu:
        ins = [row, row, pl.BlockSpec((tm, d), lambda i: (i, 1)), vec, vec]
        args = (x, m, m, g.reshape(1, d), b.reshape(1, d))
        kern = _ln_glu_kernel
    else:
        ins = [row, row, vec, vec]
        args = (x, m, g.reshape(1, d), b.reshape(1, d))
        kern = _ln_kernel
    return pl.pallas_call(
        kern, grid=(n // tm,), in_specs=ins, out_specs=row,
        out_shape=jax.ShapeDtypeStruct((n, d), F32),
        compiler_params=_cparams("parallel"), name="ln_res",
    )(*args)


def _compress_consts(pool, pe, phi):
    eye_cur = jnp.eye(CMP_ROWS, dtype=F32)
    eye_prev = jnp.eye(CMP_ROWS, k=-1, dtype=F32)
    lo = pool[:, :, None, None, :CMP_STRIDE]
    hi = pool[:, :, None, None, CMP_STRIDE:]
    pb = (eye_prev[None, None, :, :, None] * lo + eye_cur[None, None, :, :, None] * hi).reshape(
        2, NSA_KV_HEADS, CMP_ROWS, CMP_CHUNK)
    first = jnp.asarray((np.arange(CMP_ROWS) == 0).astype(np.float32))[None, None, :, None]
    pbt = first * pool[:, :, None, :CMP_STRIDE]
    pe_term = jnp.einsum('skj,skjd->skd', pool, pe)
    return (pb.astype(BF16), pbt.astype(BF16), pe_term.reshape(2 * NSA_KV_HEADS, NSA_HEAD_DIM).astype(F32),
            phi.astype(BF16))


def _compress_chunk(x, tail, pb_ref, pbt_ref, pe_ref, phi_ref):
    outs = []
    for slot in range(2):
        for kv in range(NSA_KV_HEADS):
            lo = slot * NSA_KV_DIM + kv * NSA_HEAD_DIM
            xs = x[:, lo:lo + NSA_HEAD_DIM]
            pooled = _dot(pb_ref[slot, kv], xs) + _dot(pbt_ref[slot, kv], tail[:, lo:lo + NSA_HEAD_DIM])
            pooled = pooled + pe_ref[pl.ds(slot * NSA_KV_HEADS + kv, 1), :]
            outs.append(_dot(pooled.astype(BF16), phi_ref[slot, kv]))
    return jnp.concatenate(outs, axis=1)


def _slope_col(kv, rows, per):
    gi = lax.broadcasted_iota(jnp.int32, (rows, 1), 0) // per
    col = jnp.full((rows, 1), _SLOPES[kv * NSA_GROUP + NSA_GROUP - 1], F32)
    for g in range(NSA_GROUP - 1):
        col = jnp.where(gi == g, _SLOPES[kv * NSA_GROUP + g], col)
    return col


def _select_top(score, axis, n_entries):
    idx = lax.broadcasted_iota(jnp.int32, score.shape, axis).astype(F32)
    sel = jnp.zeros(score.shape, F32)
    cur = score
    for _ in range(SLC_TOP_N):
        m = jnp.max(cur, axis=axis, keepdims=True)
        first = jnp.min(jnp.where(cur == m, idx, float(n_entries)), axis=axis, keepdims=True)
        hit = idx == first
        sel = jnp.where(hit, 1.0, sel)
        cur = jnp.where(hit, -2.0, cur)
    return sel


def _compress_kernel(x_ref, pb_ref, pbt_ref, pe_ref, phi_ref, o_ref, tail_ref):
    @pl.when(pl.program_id(1) == 0)
    def _():
        tail_ref[...] = jnp.zeros_like(tail_ref)

    x = x_ref[0]
    o_ref[0] = _compress_chunk(x, tail_ref[...], pb_ref, pbt_ref, pe_ref, phi_ref).astype(o_ref.dtype)
    tail_ref[...] = x[CMP_CHUNK - CMP_STRIDE:, :]


def _compress(kv01, consts):
    b, s, w = kv01.shape
    n_chunks = s // CMP_CHUNK
    pb, pbt, pe_term, phi = consts
    full = lambda a: pl.BlockSpec(a.shape, lambda i, c: (0,) * a.ndim)
    return pl.pallas_call(
        _compress_kernel,
        grid=(b, n_chunks),
        in_specs=[pl.BlockSpec((1, CMP_CHUNK, w), lambda i, c: (i, c, 0)),
                  full(pb), full(pbt), full(pe_term), full(phi)],
        out_specs=pl.BlockSpec((1, CMP_ROWS, w), lambda i, c: (i, c, 0)),
        out_shape=jax.ShapeDtypeStruct((b, s // CMP_STRIDE, w), BF16),
        scratch_shapes=[pltpu.VMEM((CMP_STRIDE, w), BF16)],
        compiler_params=_cparams("parallel", "arbitrary"),
        name="nsa_compress",
    )(kv01, pb, pbt, pe_term, phi)


CMP_TQ = 128


def _cmp_prompt_kernel(q_ref, ckv_ref, mt_ref, oc_ref, sel_ref):
    tq = CMP_TQ
    n_r = ckv_ref.shape[1]
    n_slc = mt_ref.shape[0]
    q0 = pl.program_id(1) * tq
    qpos = q0 + lax.broadcasted_iota(jnp.int32, (tq, n_r), 0)
    r = lax.broadcasted_iota(jnp.int32, (tq, n_r), 1)
    dist_i = qpos - (CMP_STRIDE * r + CMP_STRIDE - 1)
    mask = (dist_i >= 0) & (r >= 1)
    maskf = mask.astype(F32)
    dist = dist_i.astype(F32)
    q = q_ref[0]
    ckv = ckv_ref[0]
    blk = lax.broadcasted_iota(jnp.int32, (n_slc, tq), 0)
    cur = (q0 + lax.broadcasted_iota(jnp.int32, (n_slc, tq), 1)) // SLC_BLOCK
    avail = blk <= cur
    forced = (blk == 0) | (blk == cur) | (blk == cur - 1)
    for kv in range(NSA_KV_HEADS):
        ck = ckv[:, kv * NSA_HEAD_DIM:(kv + 1) * NSA_HEAD_DIM]
        cv = ckv[:, NSA_KV_DIM + kv * NSA_HEAD_DIM:NSA_KV_DIM + (kv + 1) * NSA_HEAD_DIM]
        qs = jnp.concatenate([q[:, (kv * NSA_GROUP + g) * NSA_HEAD_DIM:(kv * NSA_GROUP + g + 1) * NSA_HEAD_DIM]
                              for g in range(NSA_GROUP)], axis=0)
        s_all = _dot_nt(qs, ck)
        p_sum = jnp.zeros((tq, n_r), F32)
        for g in range(NSA_GROUP):
            h = kv * NSA_GROUP + g
            s = jnp.where(mask, s_all[g * tq:(g + 1) * tq] - _SLOPES[h] * dist, NEG_INF)
            e = jnp.exp(s - jnp.max(s, axis=-1, keepdims=True)) * maskf
            p = e / jnp.maximum(jnp.sum(e, axis=-1, keepdims=True), 1e-30)
            oc_ref[0, :, h * NSA_HEAD_DIM:(h + 1) * NSA_HEAD_DIM] = _dot(p.astype(BF16), cv)
            p_sum = p_sum + p
        mt = mt_ref[...]
        p_slc = sum(_dot_nt(mt, part) for part in _split3(p_sum))
        score = jnp.where(avail, p_slc + jnp.where(forced, FORCE_BONUS, 0.0), -1.0)
        sel_ref[0, kv] = _select_top(score, 0, n_slc)


def _slc_map(n_slc, n_r, n_pad):
    m = np.arange(n_pad)[:, None]
    r = np.arange(n_r)[None, :]
    ratio = SLC_BLOCK // CMP_STRIDE
    return jnp.asarray(((r >= ratio * m) & (r <= ratio * m + ratio) & (m < n_slc)).astype(np.float32), BF16)


def _cmp_prompt(qs, ckv):
    b, s, _ = qs.shape
    n_r = ckv.shape[1]
    n_slc = s // SLC_BLOCK
    mt = _slc_map(n_slc, n_r, n_slc)
    return pl.pallas_call(
        _cmp_prompt_kernel,
        grid=(b, s // CMP_TQ),
        in_specs=[pl.BlockSpec((1, CMP_TQ, NSA_Q_DIM), lambda i, t: (i, t, 0)),
                  pl.BlockSpec((1, n_r, 2 * NSA_KV_DIM), lambda i, t: (i, 0, 0)),
                  pl.BlockSpec(mt.shape, lambda i, t: (0, 0))],
        out_specs=[pl.BlockSpec((1, CMP_TQ, NSA_Q_DIM), lambda i, t: (i, t, 0)),
                   pl.BlockSpec((1, NSA_KV_HEADS, n_slc, CMP_TQ), lambda i, t: (i, 0, 0, t))],
        out_shape=[jax.ShapeDtypeStruct((b, s, NSA_Q_DIM), F32),
                   jax.ShapeDtypeStruct((b, NSA_KV_HEADS, n_slc, s), F32)],
        compiler_params=_cparams("parallel", "parallel"),
        name="nsa_cmp_prompt",
    )(qs, ckv, mt)


SLC_TQ = 256
SLC_TK = 512


def _slc_prompt_kernel(q_ref, k_ref, v_ref, sel_ref, o_ref, m_ref, l_ref, acc_ref):
    tq, tk = SLC_TQ, SLC_TK
    qi = pl.program_id(1)
    kj = pl.program_id(2)
    n_sel = sel_ref.shape[3]

    @pl.when(kj == 0)
    def _():
        m_ref[...] = jnp.full_like(m_ref, NEG_INF)
        l_ref[...] = jnp.zeros_like(l_ref)
        acc_ref[...] = jnp.zeros_like(acc_ref)

    @pl.when(kj * tk <= qi * tq + tq - 1)
    def _():
        q = q_ref[0]
        kt = k_ref[0]
        vt = v_ref[0]
        qpos = qi * tq + lax.broadcasted_iota(jnp.int32, (tq, tk), 0)
        kpos = kj * tk + lax.broadcasted_iota(jnp.int32, (tq, tk), 1)
        dist_i = qpos - kpos
        dist = dist_i.astype(F32)
        eb = lax.broadcasted_iota(jnp.int32, (n_sel, tk), 0)
        ec = (kj * tk + lax.broadcasted_iota(jnp.int32, (n_sel, tk), 1)) // SLC_BLOCK
        expand = jnp.where(eb == ec, 1.0, 0.0).astype(BF16)
        for kv in range(NSA_KV_HEADS):
            picked = _dot(sel_ref[0, kv], expand)
            mask = (picked > 0.5) & (dist_i >= 0)
            maskf = mask.astype(F32)
            qs = jnp.concatenate([q[:, (kv * NSA_GROUP + g) * NSA_HEAD_DIM:(kv * NSA_GROUP + g + 1) * NSA_HEAD_DIM]
                                  for g in range(NSA_GROUP)], axis=0)
            s_all = _dot_nt(qs, kt[:, kv * NSA_HEAD_DIM:(kv + 1) * NSA_HEAD_DIM])
            vv = vt[:, kv * NSA_HEAD_DIM:(kv + 1) * NSA_HEAD_DIM]
            for g in range(NSA_GROUP):
                h = kv * NSA_GROUP + g
                rows = slice(g * tq, (g + 1) * tq)
                s = jnp.where(mask, s_all[rows] - _SLOPES[h] * dist, NEG_INF)
                m_old = m_ref[kv, rows]
                m_new = jnp.maximum(m_old, jnp.max(s, axis=-1, keepdims=True))
                a = jnp.exp(m_old - m_new)
                e = jnp.exp(s - m_new) * maskf
                l_ref[kv, rows] = a * l_ref[kv, rows] + jnp.sum(e, axis=-1, keepdims=True)
                acc_ref[kv, rows] = a * acc_ref[kv, rows] + _dot(e.astype(BF16), vv)
                m_ref[kv, rows] = m_new

    @pl.when(kj == pl.num_programs(2) - 1)
    def _():
        for kv in range(NSA_KV_HEADS):
            for g in range(NSA_GROUP):
                h = kv * NSA_GROUP + g
                rows = slice(g * tq, (g + 1) * tq)
                o_ref[0, :, h * NSA_HEAD_DIM:(h + 1) * NSA_HEAD_DIM] = (
                    acc_ref[kv, rows] / jnp.maximum(l_ref[kv, rows], 1e-30))


def _slc_prompt(qs, k_slc, v_slc, sel):
    b, s, _ = qs.shape
    tq, tk = min(SLC_TQ, s), min(SLC_TK, s)
    assert tq == SLC_TQ and tk == SLC_TK
    n_sel = sel.shape[3]
    last = lambda t: (t * tq + tq - 1) // tk
    return pl.pallas_call(
        _slc_prompt_kernel,
        grid=(b, s // tq, s // tk),
        in_specs=[pl.BlockSpec((1, tq, NSA_Q_DIM), lambda i, t, j: (i, t, 0)),
                  pl.BlockSpec((1, tk, NSA_KV_DIM), lambda i, t, j: (i, jnp.minimum(j, last(t)), 0)),
                  pl.BlockSpec((1, tk, NSA_KV_DIM), lambda i, t, j: (i, jnp.minimum(j, last(t)), 0)),
                  pl.BlockSpec((1, NSA_KV_HEADS, tq, n_sel), lambda i, t, j: (i, 0, t, 0))],
        out_specs=pl.BlockSpec((1, tq, NSA_Q_DIM), lambda i, t, j: (i, t, 0)),
        out_shape=jax.ShapeDtypeStruct((b, s, NSA_Q_DIM), F32),
        scratch_shapes=[pltpu.VMEM((NSA_KV_HEADS, NSA_GROUP * tq, 1), F32),
                        pltpu.VMEM((NSA_KV_HEADS, NSA_GROUP * tq, 1), F32),
                        pltpu.VMEM((NSA_KV_HEADS, NSA_GROUP * tq, NSA_HEAD_DIM), F32)],
        compiler_params=_cparams("parallel", "parallel", "arbitrary"),
        name="nsa_slc_prompt",
    )(qs, k_slc, v_slc, sel)


WIN_TQ = 256
WIN_NT = WINDOW // WIN_TQ + 1


def _win_prompt_kernel(q_ref, *refs):
    tq = WIN_TQ
    k_refs = refs[:WIN_NT]
    v_refs = refs[WIN_NT:2 * WIN_NT]
    o_ref = refs[2 * WIN_NT]
    qi = pl.program_id(1)
    nk = WIN_NT * tq
    row = lax.broadcasted_iota(jnp.int32, (tq, nk), 0)
    col = lax.broadcasted_iota(jnp.int32, (tq, nk), 1)
    dist_i = WINDOW + row - col
    kpos = (qi - (WIN_NT - 1)) * tq + col
    mask = (dist_i >= 0) & (dist_i <= WINDOW) & (kpos >= 0)
    maskf = mask.astype(F32)
    dist = dist_i.astype(F32)
    q = q_ref[0]
    kcat = jnp.concatenate([r[0] for r in k_refs], axis=0)
    vcat = jnp.concatenate([r[0] for r in v_refs], axis=0)
    for kv in range(NSA_KV_HEADS):
        qs = jnp.concatenate([q[:, (kv * NSA_GROUP + g) * NSA_HEAD_DIM:(kv * NSA_GROUP + g + 1) * NSA_HEAD_DIM]
                              for g in range(NSA_GROUP)], axis=0)
        s_all = _dot_nt(qs, kcat[:, kv * NSA_HEAD_DIM:(kv + 1) * NSA_HEAD_DIM])
        vv = vcat[:, kv * NSA_HEAD_DIM:(kv + 1) * NSA_HEAD_DIM]
        for g in range(NSA_GROUP):
            h = kv * NSA_GROUP + g
            s = jnp.where(mask, s_all[g * tq:(g + 1) * tq] - _SLOPES[h] * dist, NEG_INF)
            e = jnp.exp(s - jnp.max(s, axis=-1, keepdims=True)) * maskf
            den = jnp.maximum(jnp.sum(e, axis=-1, keepdims=True), 1e-30)
            o_ref[0, :, h * NSA_HEAD_DIM:(h + 1) * NSA_HEAD_DIM] = _dot(e.astype(BF16), vv) / den


def _win_prompt(qs, k_win, v_win):
    b, s, _ = qs.shape
    tq = WIN_TQ
    kspec = lambda d: pl.BlockSpec((1, tq, NSA_KV_DIM),
                                   lambda i, t: (i, jnp.maximum(t - (WIN_NT - 1) + d, 0), 0))
    return pl.pallas_call(
        _win_prompt_kernel,
        grid=(b, s // tq),
        in_specs=[pl.BlockSpec((1, tq, NSA_Q_DIM), lambda i, t: (i, t, 0))]
                 + [kspec(d) for d in range(WIN_NT)] + [kspec(d) for d in range(WIN_NT)],
        out_specs=pl.BlockSpec((1, tq, NSA_Q_DIM), lambda i, t: (i, t, 0)),
        out_shape=jax.ShapeDtypeStruct((b, s, NSA_Q_DIM), F32),
        compiler_params=_cparams("parallel", "parallel"),
        name="nsa_win_prompt",
    )(qs, *([k_win] * WIN_NT), *([v_win] * WIN_NT))


def _merge_kernel(oc_ref, os_ref, ow_ref, hg_ref, ex_ref, o_ref):
    gate = 1.0 / (1.0 + jnp.exp(-hg_ref[...]))
    parts = _split3(gate)[:2]
    out = None
    for br, ref in enumerate((oc_ref, os_ref, ow_ref)):
        gx = sum(_dot(p, ex_ref[br]) for p in parts)
        term = gx * ref[...]
        out = term if out is None else out + term
    o_ref[...] = out.astype(o_ref.dtype)


def _gate_expand():
    e = np.zeros((3, LANES, NSA_Q_DIM), np.float32)
    for br in range(3):
        for h in range(NSA_HEADS):
            e[br, h * 3 + br, h * NSA_HEAD_DIM:(h + 1) * NSA_HEAD_DIM] = 1.0
    return jnp.asarray(e, BF16)


def _merge(oc, os_, ow, hg):
    n = oc.shape[0]
    tm = min(256, n)
    ex = _gate_expand()
    row = pl.BlockSpec((tm, NSA_Q_DIM), lambda i: (i, 0))
    return pl.pallas_call(
        _merge_kernel, grid=(n // tm,),
        in_specs=[row, row, row, pl.BlockSpec((tm, LANES), lambda i: (i, 0)),
                  pl.BlockSpec(ex.shape, lambda i: (0, 0, 0))],
        out_specs=row,
        out_shape=jax.ShapeDtypeStruct((n, NSA_Q_DIM), BF16),
        compiler_params=_cparams("parallel"), name="nsa_merge",
    )(oc, os_, ow, hg, ex)


def _nsa_weights(w_in, pool, pe, phi, w_out):
    w_main = w_in[:, :NSA_MAIN_DIM].astype(BF16)
    w_gate = jnp.pad(w_in[:, NSA_MAIN_DIM:], ((0, 0), (0, LANES - NSA_N_GATES))).astype(BF16)
    return w_main, w_gate, _compress_consts(pool, pe, phi), w_out.astype(BF16)


def _nsa_prompt(x, weights):
    b, s, _ = x.shape
    w_main, w_gate, consts, w_out = weights
    xb = x.reshape(b * s, D_MODEL).astype(BF16)
    h = _mm(xb, w_main)
    hg = _mm(xb, w_gate)
    h3 = h.reshape(b, s, NSA_MAIN_DIM)
    qs = (h3[..., :NSA_Q_DIM] * NSA_SCALE).astype(BF16)
    kvb = h3[..., NSA_Q_DIM:].astype(BF16)
    ckv = _compress(kvb[..., :2 * NSA_KV_DIM], consts)
    o_c, sel_t = _cmp_prompt(qs, ckv)
    sel = jnp.swapaxes(sel_t, 2, 3).astype(BF16)
    o_s = _slc_prompt(qs, kvb[..., 2 * NSA_KV_DIM:3 * NSA_KV_DIM], kvb[..., 3 * NSA_KV_DIM:4 * NSA_KV_DIM], sel)
    o_w = _win_prompt(qs, kvb[..., 4 * NSA_KV_DIM:5 * NSA_KV_DIM], kvb[..., 5 * NSA_KV_DIM:])
    o = _merge(o_c.reshape(b * s, NSA_Q_DIM), o_s.reshape(b * s, NSA_Q_DIM), o_w.reshape(b * s, NSA_Q_DIM), hg)
    y = _mm(o, w_out).reshape(b, s, D_MODEL)
    kv = h3[..., NSA_Q_DIM:].reshape(b, s, 6, NSA_KV_HEADS, NSA_HEAD_DIM)
    rows = kv[:, :, :4].reshape(b, s // PAGE_SIZE, PAGE_SIZE, 4, NSA_KV_HEADS, NSA_HEAD_DIM)
    win = kv[:, s - min(WINDOW, s):, 4:]
    return y, rows, win


def _gather_pages(page_refs):
    return jnp.concatenate(
        [jnp.concatenate([r[0, 0, :, s, k, :].astype(BF16) for s in range(2) for k in range(NSA_KV_HEADS)], axis=1)
         for r in page_refs], axis=0)


def _cmp_sample_kernel(pt_ref, *refs, t_new, p_len):
    del pt_ref
    npg = SAMPLE_PAGES_PER_STEP
    page_refs = refs[:npg]
    q_ref, pb_ref, pbt_ref, pe_ref, phi_ref, mt_ref, oc_ref, sel_ref, tail_ref, s_ref, cv_ref = refs[npg:]
    c = pl.program_id(1)
    rows = NSA_GROUP * t_new
    n_r = s_ref.shape[2]

    @pl.when(c == 0)
    def _():
        tail_ref[...] = jnp.zeros_like(tail_ref)

    x = _gather_pages(page_refs)
    ckv = _compress_chunk(x, tail_ref[...], pb_ref, pbt_ref, pe_ref, phi_ref)
    tail_ref[...] = x[CMP_CHUNK - CMP_STRIDE:, :]
    r0 = pl.multiple_of(c * CMP_ROWS, CMP_ROWS)
    cv_ref[pl.ds(r0, CMP_ROWS), :] = ckv[:, NSA_KV_DIM:].astype(BF16)
    for kv in range(NSA_KV_HEADS):
        ck = ckv[:, kv * NSA_HEAD_DIM:(kv + 1) * NSA_HEAD_DIM].astype(BF16)
        s_ref[kv, :, pl.ds(r0, CMP_ROWS)] = _dot_nt(q_ref[0, kv], ck)

    @pl.when(c == pl.num_programs(1) - 1)
    def _():
        ri = lax.broadcasted_iota(jnp.int32, (rows, n_r), 1)
        ti = lax.broadcasted_iota(jnp.int32, (rows, n_r), 0) % t_new
        dist_i = (p_len + ti) - (CMP_STRIDE * ri + CMP_STRIDE - 1)
        mask = (dist_i >= 0) & (ri >= 1)
        maskf = mask.astype(F32)
        dist = dist_i.astype(F32)
        n_pad = mt_ref.shape[1]
        blk = lax.broadcasted_iota(jnp.int32, (rows, n_pad), 1)
        cur = (p_len + lax.broadcasted_iota(jnp.int32, (rows, n_pad), 0) % t_new) // SLC_BLOCK
        n_slc = -(-(p_len + t_new) // SLC_BLOCK)
        avail = blk <= cur
        forced = (blk == 0) | (blk == cur) | (blk == cur - 1)
        gi = lax.broadcasted_iota(jnp.int32, (rows, rows), 0) % t_new
        gj = lax.broadcasted_iota(jnp.int32, (rows, rows), 1) % t_new
        gsum = jnp.where(gi == gj, 1.0, 0.0).astype(BF16)
        for kv in range(NSA_KV_HEADS):
            s = jnp.where(mask, s_ref[kv] - _slope_col(kv, rows, t_new) * dist, NEG_INF)
            e = jnp.exp(s - jnp.max(s, axis=-1, keepdims=True)) * maskf
            p = e / jnp.maximum(jnp.sum(e, axis=-1, keepdims=True), 1e-30)
            oc_ref[0, kv] = _dot(p.astype(BF16), cv_ref[:, kv * NSA_HEAD_DIM:(kv + 1) * NSA_HEAD_DIM])
            p_sum = sum(_dot(gsum, part) for part in _split3(p))
            p_slc = sum(_dot(part, mt_ref[...]) for part in _split3(p_sum))
            score = jnp.where(avail, p_slc + jnp.where(forced, FORCE_BONUS, 0.0), -1.0)
            score = jnp.where(blk < n_slc, score, -3.0)
            sel_ref[0, kv] = _select_top(score, 1, n_pad)


def _page_spec(layer, half, d):
    npg = SAMPLE_PAGES_PER_STEP
    return pl.BlockSpec((1, 1, PAGE_SIZE, 2, NSA_KV_HEADS, NSA_HEAD_DIM),
                        lambda i, c, pt: (layer, pt[i, c * npg + d], 0, half, 0, 0))


def _cmp_sample(cache, layer, page_table, q16, consts, t_new):
    db, n_pages = page_table.shape
    p_len = n_pages * PAGE_SIZE
    n_chunks = p_len // CMP_CHUNK
    n_r = p_len // CMP_STRIDE
    n_slc = -(-(p_len + t_new) // SLC_BLOCK)
    n_pad = -(-n_slc // LANES) * LANES
    mt = _slc_map(n_slc, n_r, n_pad).T
    pb, pbt, pe_term, phi = consts
    rows = NSA_GROUP * t_new
    full = lambda a: pl.BlockSpec(a.shape, lambda i, c, pt: (0,) * a.ndim)
    gs = pltpu.PrefetchScalarGridSpec(
        num_scalar_prefetch=1,
        grid=(db, n_chunks),
        in_specs=[_page_spec(layer, 0, d) for d in range(SAMPLE_PAGES_PER_STEP)]
                 + [pl.BlockSpec((1, NSA_KV_HEADS, rows, NSA_HEAD_DIM), lambda i, c, pt: (i, 0, 0, 0)),
                    full(pb), full(pbt), full(pe_term), full(phi), full(mt)],
        out_specs=[pl.BlockSpec((1, NSA_KV_HEADS, rows, NSA_HEAD_DIM), lambda i, c, pt: (i, 0, 0, 0)),
                   pl.BlockSpec((1, NSA_KV_HEADS, rows, n_pad), lambda i, c, pt: (i, 0, 0, 0))],
        scratch_shapes=[pltpu.VMEM((CMP_STRIDE, 2 * NSA_KV_DIM), BF16),
                        pltpu.VMEM((NSA_KV_HEADS, rows, n_r), F32),
                        pltpu.VMEM((n_r, NSA_KV_DIM), BF16)],
    )
    return pl.pallas_call(
        functools.partial(_cmp_sample_kernel, t_new=t_new, p_len=p_len),
        grid_spec=gs,
        out_shape=[jax.ShapeDtypeStruct((db, NSA_KV_HEADS, rows, NSA_HEAD_DIM), F32),
                   jax.ShapeDtypeStruct((db, NSA_KV_HEADS, rows, n_pad), F32)],
        compiler_params=_cparams("parallel", "arbitrary"),
        name="nsa_cmp_sample",
    )(page_table, *([cache] * SAMPLE_PAGES_PER_STEP), q16, pb, pbt, pe_term, phi, mt)


def _online_update(s, maskf, vv, m_ref, l_ref, acc_ref, kv):
    m_old = m_ref[kv]
    m_new = jnp.maximum(m_old, jnp.max(s, axis=-1, keepdims=True))
    a = jnp.exp(m_old - m_new)
    e = jnp.exp(s - m_new) * maskf
    l_ref[kv] = a * l_ref[kv] + jnp.sum(e, axis=-1, keepdims=True)
    acc_ref[kv] = a * acc_ref[kv] + _dot(e.astype(BF16), vv)
    m_ref[kv] = m_new


def _slc_sample_kernel(pt_ref, *refs, t_new, p_len):
    del pt_ref
    npg = SAMPLE_PAGES_PER_STEP
    page_refs = refs[:npg]
    q_ref, sel_ref, kn_ref, vn_ref, o_ref, m_ref, l_ref, acc_ref = refs[npg:]
    c = pl.program_id(1)
    rows = NSA_GROUP * t_new
    n_pad = sel_ref.shape[3]

    @pl.when(c == 0)
    def _():
        m_ref[...] = jnp.full_like(m_ref, NEG_INF)
        l_ref[...] = jnp.zeros_like(l_ref)
        acc_ref[...] = jnp.zeros_like(acc_ref)

    x = _gather_pages(page_refs)
    eb = lax.broadcasted_iota(jnp.int32, (n_pad, CMP_CHUNK), 0)
    ec = (c * CMP_CHUNK + lax.broadcasted_iota(jnp.int32, (n_pad, CMP_CHUNK), 1)) // SLC_BLOCK
    expand = jnp.where(eb == ec, 1.0, 0.0).astype(BF16)
    ti = lax.broadcasted_iota(jnp.int32, (rows, CMP_CHUNK), 0) % t_new
    kpos = c * CMP_CHUNK + lax.broadcasted_iota(jnp.int32, (rows, CMP_CHUNK), 1)
    dist = ((p_len + ti) - kpos).astype(F32)
    for kv in range(NSA_KV_HEADS):
        mask = _dot(sel_ref[0, kv], expand) > 0.5
        s = _dot_nt(q_ref[0, kv], x[:, kv * NSA_HEAD_DIM:(kv + 1) * NSA_HEAD_DIM])
        s = jnp.where(mask, s - _slope_col(kv, rows, t_new) * dist, NEG_INF)
        _online_update(s, mask.astype(F32), x[:, NSA_KV_DIM + kv * NSA_HEAD_DIM:NSA_KV_DIM + (kv + 1) * NSA_HEAD_DIM],
                       m_ref, l_ref, acc_ref, kv)

    @pl.when(c == pl.num_programs(1) - 1)
    def _():
        n_new = kn_ref.shape[1]
        ti2 = lax.broadcasted_iota(jnp.int32, (rows, n_new), 0) % t_new
        ci = lax.broadcasted_iota(jnp.int32, (rows, n_new), 1)
        mask2 = (ci <= ti2) & (ci < t_new)
        dist2 = (ti2 - ci).astype(F32)
        for kv in range(NSA_KV_HEADS):
            s = _dot_nt(q_ref[0, kv], kn_ref[0, :, kv * NSA_HEAD_DIM:(kv + 1) * NSA_HEAD_DIM])
            s = jnp.where(mask2, s - _slope_col(kv, rows, t_new) * dist2, NEG_INF)
            _online_update(s, mask2.astype(F32), vn_ref[0, :, kv * NSA_HEAD_DIM:(kv + 1) * NSA_HEAD_DIM],
                           m_ref, l_ref, acc_ref, kv)
            o_ref[0, kv] = acc_ref[kv] / jnp.maximum(l_ref[kv], 1e-30)


def _slc_sample(cache, layer, page_table, q16, sel16, k_new, v_new, t_new):
    db, n_pages = page_table.shape
    p_len = n_pages * PAGE_SIZE
    assert p_len % SLC_BLOCK == 0
    n_chunks = p_len // CMP_CHUNK
    rows = NSA_GROUP * t_new
    n_pad = sel16.shape[3]
    per_b = lambda a: pl.BlockSpec((1,) + a.shape[1:], lambda i, c, pt: (i,) + (0,) * (a.ndim - 1))
    gs = pltpu.PrefetchScalarGridSpec(
        num_scalar_prefetch=1,
        grid=(db, n_chunks),
        in_specs=[_page_spec(layer, 1, d) for d in range(SAMPLE_PAGES_PER_STEP)]
                 + [per_b(q16), per_b(sel16), per_b(k_new), per_b(v_new)],
        out_specs=pl.BlockSpec((1, NSA_KV_HEADS, rows, NSA_HEAD_DIM), lambda i, c, pt: (i, 0, 0, 0)),
        scratch_shapes=[pltpu.VMEM((NSA_KV_HEADS, rows, 1), F32),
                        pltpu.VMEM((NSA_KV_HEADS, rows, 1), F32),
                        pltpu.VMEM((NSA_KV_HEADS, rows, NSA_HEAD_DIM), F32)],
    )
    return pl.pallas_call(
        functools.partial(_slc_sample_kernel, t_new=t_new, p_len=p_len),
        grid_spec=gs,
        out_shape=jax.ShapeDtypeStruct((db, NSA_KV_HEADS, rows, NSA_HEAD_DIM), F32),
        compiler_params=_cparams("parallel", "arbitrary"),
        name="nsa_slc_sample",
    )(page_table, *([cache] * SAMPLE_PAGES_PER_STEP), q16, sel16, k_new, v_new)


def _win_sample_kernel(q_ref, wk_ref, wv_ref, kn_ref, vn_ref, o_ref, *, t_new):
    rows = NSA_GROUP * t_new
    wb = wk_ref.shape[1]
    n_new = kn_ref.shape[1]
    t1 = lax.broadcasted_iota(jnp.int32, (rows, wb), 0) % t_new
    c1 = lax.broadcasted_iota(jnp.int32, (rows, wb), 1)
    d1 = wb + t1 - c1
    mask1 = (d1 >= 0) & (d1 <= WINDOW)
    t2 = lax.broadcasted_iota(jnp.int32, (rows, n_new), 0) % t_new
    c2 = lax.broadcasted_iota(jnp.int32, (rows, n_new), 1)
    d2 = t2 - c2
    mask2 = (d2 >= 0) & (d2 <= WINDOW) & (c2 < t_new)
    for kv in range(NSA_KV_HEADS):
        lanes = slice(kv * NSA_HEAD_DIM, (kv + 1) * NSA_HEAD_DIM)
        slope = _slope_col(kv, rows, t_new)
        s1 = jnp.where(mask1, _dot_nt(q_ref[0, kv], wk_ref[0, :, lanes]) - slope * d1.astype(F32), NEG_INF)
        s2 = jnp.where(mask2, _dot_nt(q_ref[0, kv], kn_ref[0, :, lanes]) - slope * d2.astype(F32), NEG_INF)
        m = jnp.maximum(jnp.max(s1, axis=-1, keepdims=True), jnp.max(s2, axis=-1, keepdims=True))
        e1 = jnp.exp(s1 - m) * mask1.astype(F32)
        e2 = jnp.exp(s2 - m) * mask2.astype(F32)
        den = jnp.maximum(jnp.sum(e1, axis=-1, keepdims=True) + jnp.sum(e2, axis=-1, keepdims=True), 1e-30)
        o_ref[0, kv] = (_dot(e1.astype(BF16), wv_ref[0, :, lanes]) + _dot(e2.astype(BF16), vn_ref[0, :, lanes])) / den


def _win_sample(q16, wk, wv, k_new, v_new, t_new):
    db = q16.shape[0]
    rows = NSA_GROUP * t_new
    per_b = lambda a: pl.BlockSpec((1,) + a.shape[1:], lambda i: (i,) + (0,) * (a.ndim - 1))
    return pl.pallas_call(
        functools.partial(_win_sample_kernel, t_new=t_new),
        grid=(db,),
        in_specs=[per_b(q16), per_b(wk), per_b(wv), per_b(k_new), per_b(v_new)],
        out_specs=pl.BlockSpec((1, NSA_KV_HEADS, rows, NSA_HEAD_DIM), lambda i: (i, 0, 0, 0)),
        out_shape=jax.ShapeDtypeStruct((db, NSA_KV_HEADS, rows, NSA_HEAD_DIM), F32),
        compiler_params=_cparams("parallel"),
        name="nsa_win_sample",
    )(q16, wk, wv, k_new, v_new)


def _pad_rows(a, n):
    return jnp.pad(a, ((0, 0), (0, n - a.shape[1]), (0, 0)))


def _nsa_sample(x, cache, layer, win_l, page_table, weights):
    db, t, _ = x.shape
    w_main, w_gate, consts, w_out = weights
    xb = x.reshape(db * t, D_MODEL).astype(BF16)
    h = _mm(xb, w_main)
    hg = _mm(xb, w_gate)
    h3 = h.reshape(db, t, NSA_MAIN_DIM)
    q16 = (h3[..., :NSA_Q_DIM] * NSA_SCALE).astype(BF16).reshape(db, t, NSA_KV_HEADS, NSA_GROUP, NSA_HEAD_DIM)
    q16 = q16.transpose(0, 2, 3, 1, 4).reshape(db, NSA_KV_HEADS, NSA_GROUP * t, NSA_HEAD_DIM)
    kvn = h3[..., NSA_Q_DIM:].reshape(db, t, 6, NSA_KV_DIM)
    new = lambda slot: _pad_rows(kvn[:, :, slot].astype(BF16), LANES)
    o_c, sel16 = _cmp_sample(cache, layer, page_table, q16, consts, t)
    o_s = _slc_sample(cache, layer, page_table, q16, sel16.astype(BF16), new(2), new(3), t)
    wb = win_l.shape[1]
    wkv = win_l.astype(BF16).reshape(db, wb, 2, NSA_KV_DIM)
    o_w = _win_sample(q16, wkv[:, :, 0], wkv[:, :, 1], new(4), new(5), t)
    unrow = lambda o: o.reshape(db, NSA_KV_HEADS, NSA_GROUP, t, NSA_HEAD_DIM).transpose(0, 3, 1, 2, 4).reshape(
        db * t, NSA_Q_DIM)
    o = _merge(unrow(o_c), unrow(o_s), unrow(o_w), hg)
    y = _mm(o, w_out).reshape(db, t, D_MODEL)
    kv = h3[..., NSA_Q_DIM:].reshape(db, t, 6, NSA_KV_HEADS, NSA_HEAD_DIM)
    wbuf = jnp.concatenate([win_l, kv[:, :, 4:]], axis=1)
    return y, kv[:, :, :4], wbuf[:, wbuf.shape[1] - wb:]


S5_SEQS = SUBLANES
S5_CB_GROUPS = 16
S5_CB_STATES = S5_CB_GROUPS * S5_STATE
S5_CB_CH = S5_CB_GROUPS * S5_GROUP_CH
S5_N_CB = S5_GROUPS // S5_CB_GROUPS
S5_MAX_STEPS = 128


def _s5_discretize(a_re, a_im, log_dt, b_re, b_im):
    dt = jnp.exp(log_dt.astype(F32))[:, None]
    mag = jnp.exp(a_re * dt)
    ab_re = mag * jnp.cos(a_im * dt)
    ab_im = mag * jnp.sin(a_im * dt)
    den = a_re * a_re + a_im * a_im
    nr = ab_re - 1.0
    f_re = (nr * a_re + ab_im * a_im) / den
    f_im = (ab_im * a_re - nr * a_im) / den
    bb_re = f_re[..., None] * b_re - f_im[..., None] * b_im
    bb_im = f_re[..., None] * b_im + f_im[..., None] * b_re
    return ab_re, ab_im, bb_re, bb_im


def _s5_weights(w_in, a_re, a_im, log_dt, b_re, b_im, c_re, c_im, d, w_glu):
    ab_re, ab_im, bb_re, bb_im = _s5_discretize(a_re, a_im, log_dt, b_re, b_im)
    eye = jnp.eye(S5_CB_GROUPS, dtype=F32)

    def bd_in(bb):
        x = bb.reshape(S5_N_CB, S5_CB_GROUPS, S5_STATE, S5_GROUP_CH)
        return jnp.einsum('ngpc,gh->ngchp', x, eye).reshape(S5_N_CB, S5_CB_CH, S5_CB_STATES)

    def bd_out(cc):
        x = cc.reshape(S5_N_CB, S5_CB_GROUPS, S5_GROUP_CH, S5_STATE)
        return jnp.einsum('ngcp,gh->ngphc', x, eye).reshape(S5_N_CB, S5_CB_STATES, S5_CB_CH)

    bmat = jnp.concatenate([bd_in(bb_re), bd_in(bb_im)], axis=2).astype(BF16)
    cmat = jnp.concatenate([bd_out(c_re.astype(F32)), -bd_out(c_im.astype(F32))], axis=1).astype(BF16)
    rep = lambda a: jnp.broadcast_to(a.reshape(S5_N_CB, 1, S5_CB_STATES), (S5_N_CB, S5_SEQS, S5_CB_STATES))
    return dict(w_in=w_in.astype(BF16), w_glu=w_glu.astype(BF16), bmat=bmat, cmat=cmat,
                ar=rep(ab_re), ai=rep(ab_im), ab_re=ab_re, ab_im=ab_im, d=d.reshape(1, D_MODEL).astype(F32))


def _s5_scan_kernel(u_ref, b_ref, c_ref, ar_ref, ai_ref, d_ref, h0r_ref, h0i_ref, *rest, steps, with_y):
    if with_y:
        y_ref, hr_ref, hi_ref, bu_ref = rest
    else:
        hr_ref, hi_ref, bu_ref = rest
    ns = S5_CB_STATES

    @pl.when(pl.program_id(1) == 0)
    def _():
        hr_ref[...] = h0r_ref[...]
        hi_ref[...] = h0i_ref[...]

    u = u_ref[...]
    bu_ref[...] = _dot(u.astype(BF16), b_ref[0])
    ar = ar_ref[0]
    ai = ai_ref[0]

    def step(s, carry):
        hr, hi = carry
        r0 = pl.multiple_of(s * S5_SEQS, S5_SEQS)
        br = bu_ref[pl.ds(r0, S5_SEQS), :ns]
        bi = bu_ref[pl.ds(r0, S5_SEQS), ns:]
        nr = ar * hr - ai * hi + br
        ni = ar * hi + ai * hr + bi
        if with_y:
            bu_ref[pl.ds(r0, S5_SEQS), :ns] = nr
            bu_ref[pl.ds(r0, S5_SEQS), ns:] = ni
        return nr, ni

    hr, hi = lax.fori_loop(0, steps, step, (hr_ref[0], hi_ref[0]))
    hr_ref[0] = hr
    hi_ref[0] = hi
    if with_y:
        y_ref[...] = _dot(bu_ref[...].astype(BF16), c_ref[0]) + d_ref[...] * u


def _s5_scan(u_rows, w, h0r, h0i, with_y=True):
    n_rows = u_rows.shape[0]
    n_steps = n_rows // S5_SEQS
    steps = min(S5_MAX_STEPS, n_steps)
    assert n_steps % steps == 0
    tr = steps * S5_SEQS
    cb_spec = lambda a: pl.BlockSpec((1,) + a.shape[1:], lambda cb, t: (cb,) + (0,) * (a.ndim - 1))
    y_spec = [pl.BlockSpec((tr, S5_CB_CH), lambda cb, t: (t, cb))] if with_y else []
    y_shape = [jax.ShapeDtypeStruct((n_rows, D_MODEL), F32)] if with_y else []
    return pl.pallas_call(
        functools.partial(_s5_scan_kernel, steps=steps, with_y=with_y),
        grid=(S5_N_CB, n_steps // steps),
        in_specs=[pl.BlockSpec((tr, S5_CB_CH), lambda cb, t: (t, cb)),
                  cb_spec(w['bmat']), cb_spec(w['cmat']), cb_spec(w['ar']), cb_spec(w['ai']),
                  pl.BlockSpec((1, S5_CB_CH), lambda cb, t: (0, cb)),
                  cb_spec(h0r), cb_spec(h0i)],
        out_specs=y_spec + [cb_spec(h0r), cb_spec(h0i)],
        out_shape=y_shape + [jax.ShapeDtypeStruct(h0r.shape, F32), jax.ShapeDtypeStruct(h0i.shape, F32)],
        scratch_shapes=[pltpu.VMEM((tr, 2 * S5_CB_STATES), F32)],
        compiler_params=_cparams("parallel", "arbitrary"),
        name="s5_scan",
    )(u_rows, w['bmat'], w['cmat'], w['ar'], w['ai'], w['d'], h0r, h0i)


def _state_to_blocks(h):
    return h.reshape(h.shape[0], S5_N_CB, S5_CB_STATES).transpose(1, 0, 2)


def _blocks_to_state(h):
    return h.transpose(1, 0, 2).reshape(h.shape[1], S5_GROUPS, S5_STATE)


def _cpow2(re, im, n):
    for _ in range(n):
        re, im = re * re - im * im, 2.0 * re * im
    return re, im


def _s5_prompt(x, w):
    b, t, _ = x.shape
    n_seg = S5_SEQS // b
    seg = t // n_seg
    assert n_seg * b == S5_SEQS and seg & (seg - 1) == 0
    xr = x.reshape(b, n_seg, seg, D_MODEL).transpose(2, 0, 1, 3).reshape(t * b, D_MODEL)
    u = _mm(xr.astype(BF16), w['w_in'])
    zero = jnp.zeros((S5_N_CB, S5_SEQS, S5_CB_STATES), F32)
    er, ei = _s5_scan(u, w, zero, zero, with_y=False)
    er = _blocks_to_state(er).reshape(b, n_seg, S5_GROUPS, S5_STATE)
    ei = _blocks_to_state(ei).reshape(b, n_seg, S5_GROUPS, S5_STATE)
    pr, pi = _cpow2(w['ab_re'], w['ab_im'], int(math.log2(seg)))
    sr = [jnp.zeros((b, S5_GROUPS, S5_STATE), F32)]
    si = [jnp.zeros((b, S5_GROUPS, S5_STATE), F32)]
    for k in range(n_seg - 1):
        sr.append(er[:, k] + pr * sr[k] - pi * si[k])
        si.append(ei[:, k] + pr * si[k] + pi * sr[k])
    h0r = _state_to_blocks(jnp.stack(sr, axis=1).reshape(S5_SEQS, S5_GROUPS, S5_STATE))
    h0i = _state_to_blocks(jnp.stack(si, axis=1).reshape(S5_SEQS, S5_GROUPS, S5_STATE))
    y, hr, hi = _s5_scan(u, w, h0r, h0i)
    hr = _blocks_to_state(hr).reshape(b, n_seg, S5_GROUPS, S5_STATE)[:, -1]
    hi = _blocks_to_state(hi).reshape(b, n_seg, S5_GROUPS, S5_STATE)[:, -1]
    z = _mm(_gelu_rows(y), w['w_glu'])
    unperm = lambda a: a.reshape(seg, b, n_seg, -1).transpose(1, 2, 0, 3).reshape(b * t, -1)
    return unperm(z), jnp.stack([hr, hi], axis=1)


def _s5_sample(x, h0, w):
    db, t, _ = x.shape
    assert db == S5_SEQS
    xr = x.transpose(1, 0, 2).reshape(t * db, D_MODEL)
    u = _mm(xr.astype(BF16), w['w_in'])
    y, hr, hi = _s5_scan(u, w, _state_to_blocks(h0[:, 0].astype(F32)), _state_to_blocks(h0[:, 1].astype(F32)))
    z = _mm(_gelu_rows(y), w['w_glu'])
    z = z.reshape(t, db, -1).transpose(1, 0, 2).reshape(db * t, -1)
    return z, jnp.stack([_blocks_to_state(hr), _blocks_to_state(hi)], axis=1).astype(h0.dtype)


def _gelu_kernel(y_ref, o_ref):
    o_ref[...] = _gelu_tanh(y_ref[...]).astype(o_ref.dtype)


def _gelu_rows(y):
    n, d = y.shape
    tm = min(256, n)
    row = pl.BlockSpec((tm, d), lambda i: (i, 0))
    return pl.pallas_call(_gelu_kernel, grid=(n // tm,), in_specs=[row], out_specs=row,
                          out_shape=jax.ShapeDtypeStruct((n, d), BF16),
                          compiler_params=_cparams("parallel"), name="gelu")(y)


ROUTER_TN = 256
N_RANKED = PEER_TOPK + 1
_CAND_PAIRS = [(i, j) for i in range(N_RANKED) for j in range(N_RANKED) if (i + 1) * (j + 1) <= N_RANKED]


def _top_values(s, n):
    vals = []
    cur = s
    for _ in range(n):
        m = jnp.max(cur, axis=0, keepdims=True)
        vals.append(m)
        cur = jnp.where(cur == m, -jnp.inf, cur)
    return vals


def _router_kernel(x_ref, wq_ref, keys_ref, w1z_ref, wthr_ref, w2_ref):
    q = _dot(x_ref[...], wq_ref[...])
    for h in range(PEER_HEADS):
        s, top = [], []
        for c in range(2):
            hc = 2 * h + c
            sc = _dot_nt(keys_ref[hc], q[:, hc * PEER_HALF:(hc + 1) * PEER_HALF].astype(BF16))
            s.append(sc)
            top.append(_top_values(sc, N_RANKED))
        cand = jnp.concatenate([top[0][i] + top[1][j] for i, j in _CAND_PAIRS], axis=0)
        ranked = _top_values(cand, N_RANKED)
        tau = 0.5 * (ranked[PEER_TOPK - 1] + ranked[PEER_TOPK])
        m1, m2 = top[0][0], top[1][0]
        z = jnp.sum(jnp.where(cand >= tau, jnp.exp(cand - (m1 + m2)), 0.0), axis=0, keepdims=True)
        w1z_ref[h] = jnp.exp(s[0] - m1) / z
        wthr_ref[h] = jnp.exp((tau - s[0]) - m2)
        w2_ref[h] = jnp.exp(s[1] - m2)


def _router(x, wq, keys):
    n = x.shape[0]
    tn = min(ROUTER_TN, n)
    assert n % tn == 0
    out = jax.ShapeDtypeStruct((PEER_HEADS, PEER_N_KEYS, n), F32)
    ospec = pl.BlockSpec((PEER_HEADS, PEER_N_KEYS, tn), lambda i: (0, 0, i))
    return pl.pallas_call(
        _router_kernel, grid=(n // tn,),
        in_specs=[pl.BlockSpec((tn, D_MODEL), lambda i: (i, 0)),
                  pl.BlockSpec(wq.shape, lambda i: (0, 0)),
                  pl.BlockSpec(keys.shape, lambda i: (0, 0, 0))],
        out_specs=[ospec, ospec, ospec], out_shape=[out, out, out],
        compiler_params=_cparams("parallel"), name="peer_router",
    )(x, wq, keys)


EXPERT_TN = 512
EXPERT_TE = 512


def _expert_kernel(xt_ref, u_ref, vt_ref, w1z_ref, wthr_ref, w2_ref, o_ref):
    j = pl.program_id(1)
    tn = xt_ref.shape[1]

    @pl.when(j == 0)
    def _():
        o_ref[...] = jnp.zeros_like(o_ref)

    ht = _dot(u_ref[...], xt_ref[...])
    parts = []
    for cc in range(EXPERT_TE // PEER_N_KEYS):
        c = j * (EXPERT_TE // PEER_N_KEYS) + cc
        g = jnp.zeros((PEER_N_KEYS, tn), F32)
        for h in range(PEER_HEADS):
            w2 = w2_ref[h]
            g = g + w1z_ref[h, pl.ds(c, 1), :] * jnp.where(w2 >= wthr_ref[h, pl.ds(c, 1), :], w2, 0.0)
        act = _gelu_tanh(ht[cc * PEER_N_KEYS:(cc + 1) * PEER_N_KEYS])
        parts.append((act * g).astype(BF16))
    o_ref[...] += _dot(vt_ref[...], jnp.concatenate(parts, axis=0))


def _expert(xt, u, vt, layer, w1z, wthr, w2):
    n = xt.shape[1]
    tn = min(EXPERT_TN, n)
    assert n % tn == 0
    rspec = pl.BlockSpec((PEER_HEADS, PEER_N_KEYS, tn), lambda i, j: (0, 0, i))
    return pl.pallas_call(
        _expert_kernel, grid=(n // tn, PEER_N_EXPERTS // EXPERT_TE),
        in_specs=[pl.BlockSpec((D_MODEL, tn), lambda i, j: (0, i)),
                  pl.BlockSpec((None, EXPERT_TE, D_MODEL), lambda i, j: (layer, j, 0)),
                  pl.BlockSpec((None, D_MODEL, EXPERT_TE), lambda i, j: (layer, 0, j)),
                  rspec, rspec, rspec],
        out_specs=pl.BlockSpec((D_MODEL, tn), lambda i, j: (0, i)),
        out_shape=jax.ShapeDtypeStruct((D_MODEL, n), F32),
        compiler_params=_cparams("parallel", "arbitrary"), name="peer_expert",
    )(xt, u, vt, w1z, wthr, w2)


def _peer_tables(peer_u, peer_v):
    return peer_u.astype(BF16), jnp.swapaxes(peer_v, 1, 2).astype(BF16)


def _peer(x, w_q, sub_keys, tables, layer):
    u, vt = tables
    n = x.shape[0]
    n_pad = -(-n // LANES) * LANES
    xb = jnp.pad(x, ((0, n_pad - n), (0, 0))).astype(BF16)
    keys = sub_keys.reshape(2 * PEER_HEADS, PEER_N_KEYS, PEER_HALF).astype(BF16)
    w1z, wthr, w2 = _router(xb, w_q.astype(BF16), keys)
    out_t = _expert(xb.T, u, vt, layer, w1z, wthr, w2)
    return out_t.T[:n]


def kernel(x_prompt, x_sample, cache_nsa, state_win, state_s5, page_table, nsa_w_in, nsa_cmp_pool, nsa_cmp_pe,
           nsa_cmp_phi, nsa_w_out, s5_w_in, s5_a_re, s5_a_im, s5_log_dt, s5_b_re, s5_b_im, s5_c_re, s5_c_im, s5_d,
           s5_w_glu, peer_w_q, peer_sub_keys, peer_u, peer_v, ln_g, ln_b):
    b, s, _ = x_prompt.shape
    db, t, _ = x_sample.shape
    xp = x_prompt.reshape(b * s, D_MODEL)
    xs = x_sample.reshape(db * t, D_MODEL)
    rows_p, rows_s, win_p, win_s, s5_p, s5_s = [], [], [], [], [], []
    tables = _peer_tables(peer_u, peer_v)
    for layer in range(DEPTH):
        j = layer // N_MIXERS
        if layer % N_MIXERS == 0:
            w = _nsa_weights(nsa_w_in[j], nsa_cmp_pool[j], nsa_cmp_pe[j], nsa_cmp_phi[j], nsa_w_out[j])
            mp, rp, wp = _nsa_prompt(xp.reshape(b, s, D_MODEL), w)
            ms, rs, ws = _nsa_sample(xs.reshape(db, t, D_MODEL), cache_nsa, j, state_win[j], page_table, w)
            rows_p.append(rp)
            rows_s.append(rs)
            win_p.append(wp)
            win_s.append(ws)
            xp = _ln_res(xp, mp.reshape(b * s, D_MODEL), ln_g[layer, 0], ln_b[layer, 0])
            xs = _ln_res(xs, ms.reshape(db * t, D_MODEL), ln_g[layer, 0], ln_b[layer, 0])
        else:
            w = _s5_weights(s5_w_in[j], s5_a_re[j], s5_a_im[j], s5_log_dt[j], s5_b_re[j], s5_b_im[j],
                            s5_c_re[j], s5_c_im[j], s5_d[j], s5_w_glu[j])
            zp, hp = _s5_prompt(xp.reshape(b, s, D_MODEL), w)
            zs, hs = _s5_sample(xs.reshape(db, t, D_MODEL), state_s5[j], w)
            s5_p.append(hp.astype(state_s5.dtype))
            s5_s.append(hs)
            xp = _ln_res(xp, zp, ln_g[layer, 0], ln_b[layer, 0], glu=True)
            xs = _ln_res(xs, zs, ln_g[layer, 0], ln_b[layer, 0], glu=True)
        xp = _ln_res(xp, _peer(xp, peer_w_q[layer], peer_sub_keys[layer], tables, layer),
                     ln_g[layer, 1], ln_b[layer, 1])
        xs = _ln_res(xs, _peer(xs, peer_w_q[layer], peer_sub_keys[layer], tables, layer),
                     ln_g[layer, 1], ln_b[layer, 1])
    return (xp.reshape(b, s, D_MODEL), xs.reshape(db, t, D_MODEL), jnp.stack(rows_p), jnp.stack(rows_s),
            jnp.stack(win_p), jnp.stack(win_s), jnp.stack(s5_p), jnp.stack(s5_s))
```

```python
import functools
import math

import numpy as np
import jax
import jax.numpy as jnp
from jax import lax
from jax.experimental import pallas as pl
from jax.experimental.pallas import tpu as pltpu

F32 = jnp.float32
BF16 = jnp.bfloat16

D_MODEL = 2048
DEPTH = 4
PAGE_SIZE = 128
N_MIXERS = 2
NSA_HEADS = 16
NSA_KV_HEADS = 4
NSA_GROUP = NSA_HEADS // NSA_KV_HEADS
NSA_HEAD_DIM = D_MODEL // NSA_HEADS
NSA_Q_DIM = NSA_HEADS * NSA_HEAD_DIM
NSA_KV_DIM = NSA_KV_HEADS * NSA_HEAD_DIM
NSA_MAIN_DIM = NSA_Q_DIM + 6 * NSA_KV_DIM
NSA_N_GATES = 3 * NSA_HEADS
NSA_SCALE = NSA_HEAD_DIM ** -0.5
CMP_BLOCK = 32
CMP_STRIDE = 16
SLC_BLOCK = 64
SLC_TOP_N = 16
WINDOW = 512
FORCE_BONUS = 1.0e4
S5_GROUP_CH = 16
S5_GROUPS = D_MODEL // S5_GROUP_CH
S5_STATE = 64
PEER_HEADS = 8
PEER_N_KEYS = 128
PEER_N_EXPERTS = PEER_N_KEYS ** 2
PEER_HALF = 128
PEER_TOPK = 16
LN_EPS = 1e-5
NEG_INF = -1.0e30
ALPHA = (2.0 * DEPTH) ** 0.25

LANES = 128
SUBLANES = 8
VMEM_LIMIT_BYTES = 56 * 1024 * 1024

CMP_CHUNK = 2048
CMP_ROWS = CMP_CHUNK // CMP_STRIDE
SAMPLE_PAGES_PER_STEP = CMP_CHUNK // PAGE_SIZE

_SLOPES = [2.0 ** (-8.0 * (h + 1) / NSA_HEADS) for h in range(NSA_HEADS)]


def _cparams(*sem):
    return pltpu.CompilerParams(dimension_semantics=sem, vmem_limit_bytes=VMEM_LIMIT_BYTES)


def _dot(a, b):
    return jnp.dot(a, b, preferred_element_type=F32)


def _dot_nt(a, b):
    return lax.dot_general(a, b, (((1,), (1,)), ((), ())), preferred_element_type=F32)


def _split3(x):
    hi = x.astype(BF16)
    r1 = x - hi.astype(F32)
    mid = r1.astype(BF16)
    lo = (r1 - mid.astype(F32)).astype(BF16)
    return hi, mid, lo


def _gelu_tanh(x):
    c = math.sqrt(2.0 / math.pi)
    return 0.5 * x * (1.0 + jnp.tanh(c * (x + 0.044715 * (x * x * x))))


def _mm_kernel(a_ref, b_ref, o_ref):
    o_ref[...] = _dot(a_ref[...], b_ref[...]).astype(o_ref.dtype)


def _mm(a, b, tm=512, tn=1024, out_dtype=F32):
    m, k = a.shape
    n = b.shape[1]
    tm = min(tm, m)
    tn = min(tn, n)
    assert m % tm == 0 and n % tn == 0
    return pl.pallas_call(
        _mm_kernel,
        grid=(n // tn, m // tm),
        in_specs=[pl.BlockSpec((tm, k), lambda j, i: (i, 0)),
                  pl.BlockSpec((k, tn), lambda j, i: (0, j))],
        out_specs=pl.BlockSpec((tm, tn), lambda j, i: (i, j)),
        out_shape=jax.ShapeDtypeStruct((m, n), out_dtype),
        compiler_params=_cparams("parallel", "parallel"),
        name="mm",
    )(a, b)


def _ln_body(y, g_ref, b_ref, o_ref):
    mu = jnp.mean(y, axis=-1, keepdims=True)
    yc = y - mu
    var = jnp.mean(yc * yc, axis=-1, keepdims=True)
    o_ref[...] = yc * lax.rsqrt(var + LN_EPS) * g_ref[...] + b_ref[...]


def _ln_kernel(x_ref, m_ref, g_ref, b_ref, o_ref):
    _ln_body(ALPHA * x_ref[...] + m_ref[...], g_ref, b_ref, o_ref)


def _ln_glu_kernel(x_ref, z1_ref, z2_ref, g_ref, b_ref, o_ref):
    z2 = z2_ref[...]
    mix = z1_ref[...] * (1.0 / (1.0 + jnp.exp(-z2)))
    _ln_body(ALPHA * x_ref[...] + mix, g_ref, b_ref, o_ref)


def _ln_res(x, m, g, b, glu=False):
    n, d = x.shape
    tm = min(256, n)
    assert n % tm == 0
    row = pl.BlockSpec((tm, d), lambda i: (i, 0))
    vec = pl.BlockSpec((1, d), lambda i: (0, 0))
    if glu:
        ins = [row, row, pl.BlockSpec((tm, d), lambda i: (i, 1)), vec, vec]
        args = (x, m, m, g.reshape(1, d), b.reshape(1, d))
        kern = _ln_glu_kernel
    else:
        ins = [row, row, vec, vec]
        args = (x, m, g.reshape(1, d), b.reshape(1, d))
        kern = _ln_kernel
    return pl.pallas_call(
        kern, grid=(n // tm,), in_specs=ins, out_specs=row,
        out_shape=jax.ShapeDtypeStruct((n, d), F32),
        compiler_params=_cparams("parallel"), name="ln_res",
    )(*args)


def _compress_consts(pool, pe, phi):
    eye_cur = jnp.eye(CMP_ROWS, dtype=F32)
    eye_prev = jnp.eye(CMP_ROWS, k=-1, dtype=F32)
    lo = pool[:, :, None, None, :CMP_STRIDE]
    hi = pool[:, :, None, None, CMP_STRIDE:]
    pb = (eye_prev[None, None, :, :, None] * lo + eye_cur[None, None, :, :, None] * hi).reshape(
        2, NSA_KV_HEADS, CMP_ROWS, CMP_CHUNK)
    first = jnp.asarray((np.arange(CMP_ROWS) == 0).astype(np.float32))[None, None, :, None]
    pbt = first * pool[:, :, None, :CMP_STRIDE]
    pe_term = jnp.einsum('skj,skjd->skd', pool, pe)
    return (pb.astype(BF16), pbt.astype(BF16), pe_term.reshape(2 * NSA_KV_HEADS, NSA_HEAD_DIM).astype(F32),
            phi.astype(BF16))


def _compress_chunk(x, tail, pb_ref, pbt_ref, pe_ref, phi_ref):
    outs = []
    for slot in range(2):
        for kv in range(NSA_KV_HEADS):
            lo = slot * NSA_KV_DIM + kv * NSA_HEAD_DIM
            xs = x[:, lo:lo + NSA_HEAD_DIM]
            pooled = _dot(pb_ref[slot, kv], xs) + _dot(pbt_ref[slot, kv], tail[:, lo:lo + NSA_HEAD_DIM])
            pooled = pooled + pe_ref[pl.ds(slot * NSA_KV_HEADS + kv, 1), :]
            outs.append(_dot(pooled.astype(BF16), phi_ref[slot, kv]))
    return jnp.concatenate(outs, axis=1)


def _slope_col(kv, rows, per):
    gi = lax.broadcasted_iota(jnp.int32, (rows, 1), 0) // per
    col = jnp.full((rows, 1), _SLOPES[kv * NSA_GROUP + NSA_GROUP - 1], F32)
    for g in range(NSA_GROUP - 1):
        col = jnp.where(gi == g, _SLOPES[kv * NSA_GROUP + g], col)
    return col


def _select_top(score, axis, n_entries):
    idx = lax.broadcasted_iota(jnp.int32, score.shape, axis).astype(F32)
    sel = jnp.zeros(score.shape, F32)
    cur = score
    for _ in range(SLC_TOP_N):
        m = jnp.max(cur, axis=axis, keepdims=True)
        first = jnp.min(jnp.where(cur == m, idx, float(n_entries)), axis=axis, keepdims=True)
        hit = idx == first
        sel = jnp.where(hit, 1.0, sel)
        cur = jnp.where(hit, -2.0, cur)
    return sel


def _compress_kernel(x_ref, pb_ref, pbt_ref, pe_ref, phi_ref, o_ref, tail_ref):
    @pl.when(pl.program_id(1) == 0)
    def _():
        tail_ref[...] = jnp.zeros_like(tail_ref)

    x = x_ref[0]
    o_ref[0] = _compress_chunk(x, tail_ref[...], pb_ref, pbt_ref, pe_ref, phi_ref).astype(o_ref.dtype)
    tail_ref[...] = x[CMP_CHUNK - CMP_STRIDE:, :]


def _compress(kv01, consts):
    b, s, w = kv01.shape
    n_chunks = s // CMP_CHUNK
    pb, pbt, pe_term, phi = consts
    full = lambda a: pl.BlockSpec(a.shape, lambda i, c: (0,) * a.ndim)
    return pl.pallas_call(
        _compress_kernel,
        grid=(b, n_chunks),
        in_specs=[pl.BlockSpec((1, CMP_CHUNK, w), lambda i, c: (i, c, 0)),
                  full(pb), full(pbt), full(pe_term), full(phi)],
        out_specs=pl.BlockSpec((1, CMP_ROWS, w), lambda i, c: (i, c, 0)),
        out_shape=jax.ShapeDtypeStruct((b, s // CMP_STRIDE, w), BF16),
        scratch_shapes=[pltpu.VMEM((CMP_STRIDE, w), BF16)],
        compiler_params=_cparams("parallel", "arbitrary"),
        name="nsa_compress",
    )(kv01, pb, pbt, pe_term, phi)


CMP_TQ = 128


def _cmp_prompt_kernel(q_ref, ckv_ref, mt_ref, oc_ref, sel_ref):
    tq = CMP_TQ
    n_r = ckv_ref.shape[1]
    n_slc = mt_ref.shape[0]
    q0 = pl.program_id(1) * tq
    qpos = q0 + lax.broadcasted_iota(jnp.int32, (tq, n_r), 0)
    r = lax.broadcasted_iota(jnp.int32, (tq, n_r), 1)
    dist_i = qpos - (CMP_STRIDE * r + CMP_STRIDE - 1)
    mask = (dist_i >= 0) & (r >= 1)
    maskf = mask.astype(F32)
    dist = dist_i.astype(F32)
    q = q_ref[0]
    ckv = ckv_ref[0]
    blk = lax.broadcasted_iota(jnp.int32, (n_slc, tq), 0)
    cur = (q0 + lax.broadcasted_iota(jnp.int32, (n_slc, tq), 1)) // SLC_BLOCK
    avail = blk <= cur
    forced = (blk == 0) | (blk == cur) | (blk == cur - 1)
    for kv in range(NSA_KV_HEADS):
        ck = ckv[:, kv * NSA_HEAD_DIM:(kv + 1) * NSA_HEAD_DIM]
        cv = ckv[:, NSA_KV_DIM + kv * NSA_HEAD_DIM:NSA_KV_DIM + (kv + 1) * NSA_HEAD_DIM]
        qs = jnp.concatenate([q[:, (kv * NSA_GROUP + g) * NSA_HEAD_DIM:(kv * NSA_GROUP + g + 1) * NSA_HEAD_DIM]
                              for g in range(NSA_GROUP)], axis=0)
        s_all = _dot_nt(qs, ck)
        p_sum = jnp.zeros((tq, n_r), F32)
        for g in range(NSA_GROUP):
            h = kv * NSA_GROUP + g
            s = jnp.where(mask, s_all[g * tq:(g + 1) * tq] - _SLOPES[h] * dist, NEG_INF)
            e = jnp.exp(s - jnp.max(s, axis=-1, keepdims=True)) * maskf
            p = e / jnp.maximum(jnp.sum(e, axis=-1, keepdims=True), 1e-30)
            oc_ref[0, :, h * NSA_HEAD_DIM:(h + 1) * NSA_HEAD_DIM] = _dot(p.astype(BF16), cv)
            p_sum = p_sum + p
        mt = mt_ref[...]
        p_slc = sum(_dot_nt(mt, part) for part in _split3(p_sum))
        score = jnp.where(avail, p_slc + jnp.where(forced, FORCE_BONUS, 0.0), -1.0)
        sel_ref[0, kv] = _select_top(score, 0, n_slc)


def _slc_map(n_slc, n_r, n_pad):
    m = np.arange(n_pad)[:, None]
    r = np.arange(n_r)[None, :]
    ratio = SLC_BLOCK // CMP_STRIDE
    return jnp.asarray(((r >= ratio * m) & (r <= ratio * m + ratio) & (m < n_slc)).astype(np.float32), BF16)


def _cmp_prompt(qs, ckv):
    b, s, _ = qs.shape
    n_r = ckv.shape[1]
    n_slc = s // SLC_BLOCK
    mt = _slc_map(n_slc, n_r, n_slc)
    return pl.pallas_call(
        _cmp_prompt_kernel,
        grid=(b, s // CMP_TQ),
        in_specs=[pl.BlockSpec((1, CMP_TQ, NSA_Q_DIM), lambda i, t: (i, t, 0)),
                  pl.BlockSpec((1, n_r, 2 * NSA_KV_DIM), lambda i, t: (i, 0, 0)),
                  pl.BlockSpec(mt.shape, lambda i, t: (0, 0))],
        out_specs=[pl.BlockSpec((1, CMP_TQ, NSA_Q_DIM), lambda i, t: (i, t, 0)),
                   pl.BlockSpec((1, NSA_KV_HEADS, n_slc, CMP_TQ), lambda i, t: (i, 0, 0, t))],
        out_shape=[jax.ShapeDtypeStruct((b, s, NSA_Q_DIM), F32),
                   jax.ShapeDtypeStruct((b, NSA_KV_HEADS, n_slc, s), F32)],
        compiler_params=_cparams("parallel", "parallel"),
        name="nsa_cmp_prompt",
    )(qs, ckv, mt)


SLC_TQ = 256
SLC_TK = 512


def _slc_prompt_kernel(q_ref, k_ref, v_ref, sel_ref, o_ref, m_ref, l_ref, acc_ref):
    tq, tk = SLC_TQ, SLC_TK
    qi = pl.program_id(1)
    kj = pl.program_id(2)
    n_sel = sel_ref.shape[3]

    @pl.when(kj == 0)
    def _():
        m_ref[...] = jnp.full_like(m_ref, NEG_INF)
        l_ref[...] = jnp.zeros_like(l_ref)
        acc_ref[...] = jnp.zeros_like(acc_ref)

    @pl.when(kj * tk <= qi * tq + tq - 1)
    def _():
        q = q_ref[0]
        kt = k_ref[0]
        vt = v_ref[0]
        qpos = qi * tq + lax.broadcasted_iota(jnp.int32, (tq, tk), 0)
        kpos = kj * tk + lax.broadcasted_iota(jnp.int32, (tq, tk), 1)
        dist_i = qpos - kpos
        dist = dist_i.astype(F32)
        eb = lax.broadcasted_iota(jnp.int32, (n_sel, tk), 0)
        ec = (kj * tk + lax.broadcasted_iota(jnp.int32, (n_sel, tk), 1)) // SLC_BLOCK
        expand = jnp.where(eb == ec, 1.0, 0.0).astype(BF16)
        for kv in range(NSA_KV_HEADS):
            picked = _dot(sel_ref[0, kv], expand)
            mask = (picked > 0.5) & (dist_i >= 0)
            qs = jnp.concatenate([q[:, (kv * NSA_GROUP + g) * NSA_HEAD_DIM:(kv * NSA_GROUP + g + 1) * NSA_HEAD_DIM]
                                  for g in range(NSA_GROUP)], axis=0)
            s_all = _dot_nt(qs, kt[:, kv * NSA_HEAD_DIM:(kv + 1) * NSA_HEAD_DIM])
            vv = vt[:, kv * NSA_HEAD_DIM:(kv + 1) * NSA_HEAD_DIM]
            for g in range(NSA_GROUP):
                h = kv * NSA_GROUP + g
                rows = slice(g * tq, (g + 1) * tq)
                s = jnp.where(mask, s_all[rows] - _SLOPES[h] * dist, NEG_INF)
                m_old = m_ref[kv, rows]
                m_new = jnp.maximum(m_old, jnp.max(s, axis=-1, keepdims=True))
                a = jnp.exp(m_old - m_new)
                e = jnp.exp(s - m_new)
                l_ref[kv, rows] = a * l_ref[kv, rows] + jnp.sum(e, axis=-1, keepdims=True)
                acc_ref[kv, rows] = a * acc_ref[kv, rows] + _dot(e.astype(BF16), vv)
                m_ref[kv, rows] = m_new

    @pl.when(kj == pl.num_programs(2) - 1)
    def _():
        for kv in range(NSA_KV_HEADS):
            for g in range(NSA_GROUP):
                h = kv * NSA_GROUP + g
                rows = slice(g * tq, (g + 1) * tq)
                o_ref[0, :, h * NSA_HEAD_DIM:(h + 1) * NSA_HEAD_DIM] = (
                    acc_ref[kv, rows] / jnp.maximum(l_ref[kv, rows], 1e-30))


def _slc_prompt(qs, k_slc, v_slc, sel):
    b, s, _ = qs.shape
    tq, tk = min(SLC_TQ, s), min(SLC_TK, s)
    assert tq == SLC_TQ and tk == SLC_TK
    n_sel = sel.shape[3]
    last = lambda t: (t * tq + tq - 1) // tk
    return pl.pallas_call(
        _slc_prompt_kernel,
        grid=(b, s // tq, s // tk),
        in_specs=[pl.BlockSpec((1, tq, NSA_Q_DIM), lambda i, t, j: (i, t, 0)),
                  pl.BlockSpec((1, tk, NSA_KV_DIM), lambda i, t, j: (i, jnp.minimum(j, last(t)), 0)),
                  pl.BlockSpec((1, tk, NSA_KV_DIM), lambda i, t, j: (i, jnp.minimum(j, last(t)), 0)),
                  pl.BlockSpec((1, NSA_KV_HEADS, tq, n_sel), lambda i, t, j: (i, 0, t, 0))],
        out_specs=pl.BlockSpec((1, tq, NSA_Q_DIM), lambda i, t, j: (i, t, 0)),
        out_shape=jax.ShapeDtypeStruct((b, s, NSA_Q_DIM), F32),
        scratch_shapes=[pltpu.VMEM((NSA_KV_HEADS, NSA_GROUP * tq, 1), F32),
                        pltpu.VMEM((NSA_KV_HEADS, NSA_GROUP * tq, 1), F32),
                        pltpu.VMEM((NSA_KV_HEADS, NSA_GROUP * tq, NSA_HEAD_DIM), F32)],
        compiler_params=_cparams("parallel", "parallel", "arbitrary"),
        name="nsa_slc_prompt",
    )(qs, k_slc, v_slc, sel)


WIN_TQ = 256
WIN_NT = WINDOW // WIN_TQ + 1


def _win_prompt_kernel(q_ref, *refs):
    tq = WIN_TQ
    k_refs = refs[:WIN_NT]
    v_refs = refs[WIN_NT:2 * WIN_NT]
    o_ref = refs[2 * WIN_NT]
    qi = pl.program_id(1)
    nk = WIN_NT * tq
    row = lax.broadcasted_iota(jnp.int32, (tq, nk), 0)
    col = lax.broadcasted_iota(jnp.int32, (tq, nk), 1)
    dist_i = WINDOW + row - col
    kpos = (qi - (WIN_NT - 1)) * tq + col
    mask = (dist_i >= 0) & (dist_i <= WINDOW) & (kpos >= 0)
    dist = dist_i.astype(F32)
    q = q_ref[0]
    kcat = jnp.concatenate([r[0] for r in k_refs], axis=0)
    vcat = jnp.concatenate([r[0] for r in v_refs], axis=0)
    for kv in range(NSA_KV_HEADS):
        qs = jnp.concatenate([q[:, (kv * NSA_GROUP + g) * NSA_HEAD_DIM:(kv * NSA_GROUP + g + 1) * NSA_HEAD_DIM]
                              for g in range(NSA_GROUP)], axis=0)
        s_all = _dot_nt(qs, kcat[:, kv * NSA_HEAD_DIM:(kv + 1) * NSA_HEAD_DIM])
        vv = vcat[:, kv * NSA_HEAD_DIM:(kv + 1) * NSA_HEAD_DIM]
        for g in range(NSA_GROUP):
            h = kv * NSA_GROUP + g
            s = jnp.where(mask, s_all[g * tq:(g + 1) * tq] - _SLOPES[h] * dist, NEG_INF)
            e = jnp.exp(s - jnp.max(s, axis=-1, keepdims=True))
            den = jnp.maximum(jnp.sum(e, axis=-1, keepdims=True), 1e-30)
            o_ref[0, :, h * NSA_HEAD_DIM:(h + 1) * NSA_HEAD_DIM] = _dot(e.astype(BF16), vv) / den


def _win_prompt(qs, k_win, v_win):
    b, s, _ = qs.shape
    tq = WIN_TQ
    kspec = lambda d: pl.BlockSpec((1, tq, NSA_KV_DIM),
                                   lambda i, t: (i, jnp.maximum(t - (WIN_NT - 1) + d, 0), 0))
    return pl.pallas_call(
        _win_prompt_kernel,
        grid=(b, s // tq),
        in_specs=[pl.BlockSpec((1, tq, NSA_Q_DIM), lambda i, t: (i, t, 0))]
                 + [kspec(d) for d in range(WIN_NT)] + [kspec(d) for d in range(WIN_NT)],
        out_specs=pl.BlockSpec((1, tq, NSA_Q_DIM), lambda i, t: (i, t, 0)),
        out_shape=jax.ShapeDtypeStruct((b, s, NSA_Q_DIM), F32),
        compiler_params=_cparams("parallel", "parallel"),
        name="nsa_win_prompt",
    )(qs, *([k_win] * WIN_NT), *([v_win] * WIN_NT))


def _merge_kernel(oc_ref, os_ref, ow_ref, hg_ref, ex_ref, o_ref):
    gate = 1.0 / (1.0 + jnp.exp(-hg_ref[...]))
    parts = _split3(gate)[:2]
    out = None
    for br, ref in enumerate((oc_ref, os_ref, ow_ref)):
        gx = sum(_dot(p, ex_ref[br]) for p in parts)
        term = gx * ref[...]
        out = term if out is None else out + term
    o_ref[...] = out.astype(o_ref.dtype)


def _gate_expand():
    e = np.zeros((3, LANES, NSA_Q_DIM), np.float32)
    for br in range(3):
        for h in range(NSA_HEADS):
            e[br, h * 3 + br, h * NSA_HEAD_DIM:(h + 1) * NSA_HEAD_DIM] = 1.0
    return jnp.asarray(e, BF16)


def _merge(oc, os_, ow, hg):
    n = oc.shape[0]
    tm = min(256, n)
    ex = _gate_expand()
    row = pl.BlockSpec((tm, NSA_Q_DIM), lambda i: (i, 0))
    return pl.pallas_call(
        _merge_kernel, grid=(n // tm,),
        in_specs=[row, row, row, pl.BlockSpec((tm, LANES), lambda i: (i, 0)),
                  pl.BlockSpec(ex.shape, lambda i: (0, 0, 0))],
        out_specs=row,
        out_shape=jax.ShapeDtypeStruct((n, NSA_Q_DIM), BF16),
        compiler_params=_cparams("parallel"), name="nsa_merge",
    )(oc, os_, ow, hg, ex)


def _nsa_weights(w_in, pool, pe, phi, w_out):
    w_main = w_in[:, :NSA_MAIN_DIM].astype(BF16)
    w_gate = jnp.pad(w_in[:, NSA_MAIN_DIM:], ((0, 0), (0, LANES - NSA_N_GATES))).astype(BF16)
    return w_main, w_gate, _compress_consts(pool, pe, phi), w_out.astype(BF16)


def _nsa_prompt(x, weights):
    b, s, _ = x.shape
    w_main, w_gate, consts, w_out = weights
    xb = x.reshape(b * s, D_MODEL).astype(BF16)
    h = _mm(xb, w_main)
    hg = _mm(xb, w_gate)
    h3 = h.reshape(b, s, NSA_MAIN_DIM)
    qs = (h3[..., :NSA_Q_DIM] * NSA_SCALE).astype(BF16)
    kvb = h3[..., NSA_Q_DIM:].astype(BF16)
    ckv = _compress(kvb[..., :2 * NSA_KV_DIM], consts)
    o_c, sel_t = _cmp_prompt(qs, ckv)
    sel = jnp.swapaxes(sel_t, 2, 3).astype(BF16)
    o_s = _slc_prompt(qs, kvb[..., 2 * NSA_KV_DIM:3 * NSA_KV_DIM], kvb[..., 3 * NSA_KV_DIM:4 * NSA_KV_DIM], sel)
    o_w = _win_prompt(qs, kvb[..., 4 * NSA_KV_DIM:5 * NSA_KV_DIM], kvb[..., 5 * NSA_KV_DIM:])
    o = _merge(o_c.reshape(b * s, NSA_Q_DIM), o_s.reshape(b * s, NSA_Q_DIM), o_w.reshape(b * s, NSA_Q_DIM), hg)
    y = _mm(o, w_out).reshape(b, s, D_MODEL)
    kv = h3[..., NSA_Q_DIM:].reshape(b, s, 6, NSA_KV_HEADS, NSA_HEAD_DIM)
    rows = kv[:, :, :4].reshape(b, s // PAGE_SIZE, PAGE_SIZE, 4, NSA_KV_HEADS, NSA_HEAD_DIM)
    win = kv[:, s - min(WINDOW, s):, 4:]
    return y, rows, win


def _gather_pages(page_refs):
    n_sk = 2 * NSA_KV_HEADS
    pages = []
    for r in page_refs:
        rows = r.reshape(PAGE_SIZE * n_sk, NSA_HEAD_DIM)
        pages.append(jnp.concatenate(
            [rows[pl.ds(sk, PAGE_SIZE, stride=n_sk), :].astype(BF16) for sk in range(n_sk)], axis=1))
    return jnp.concatenate(pages, axis=0)


def _cmp_sample_kernel(pt_ref, *refs, t_new, p_len):
    del pt_ref
    npg = SAMPLE_PAGES_PER_STEP
    page_refs = refs[:npg]
    q_ref, pb_ref, pbt_ref, pe_ref, phi_ref, mt_ref, oc_ref, sel_ref, tail_ref, s_ref, cv_ref = refs[npg:]
    c = pl.program_id(1)
    rows = NSA_GROUP * t_new
    n_r = s_ref.shape[2]

    @pl.when(c == 0)
    def _():
        tail_ref[...] = jnp.zeros_like(tail_ref)

    x = _gather_pages(page_refs)
    ckv = _compress_chunk(x, tail_ref[...], pb_ref, pbt_ref, pe_ref, phi_ref)
    tail_ref[...] = x[CMP_CHUNK - CMP_STRIDE:, :]
    r0 = pl.multiple_of(c * CMP_ROWS, CMP_ROWS)
    cv_ref[pl.ds(r0, CMP_ROWS), :] = ckv[:, NSA_KV_DIM:].astype(BF16)
    for kv in range(NSA_KV_HEADS):
        ck = ckv[:, kv * NSA_HEAD_DIM:(kv + 1) * NSA_HEAD_DIM].astype(BF16)
        s_ref[kv, :, pl.ds(r0, CMP_ROWS)] = _dot_nt(q_ref[0, kv], ck)

    @pl.when(c == pl.num_programs(1) - 1)
    def _():
        ri = lax.broadcasted_iota(jnp.int32, (rows, n_r), 1)
        ti = lax.broadcasted_iota(jnp.int32, (rows, n_r), 0) % t_new
        dist_i = (p_len + ti) - (CMP_STRIDE * ri + CMP_STRIDE - 1)
        mask = (dist_i >= 0) & (ri >= 1)
        maskf = mask.astype(F32)
        dist = dist_i.astype(F32)
        n_pad = mt_ref.shape[1]
        blk = lax.broadcasted_iota(jnp.int32, (rows, n_pad), 1)
        cur = (p_len + lax.broadcasted_iota(jnp.int32, (rows, n_pad), 0) % t_new) // SLC_BLOCK
        n_slc = -(-(p_len + t_new) // SLC_BLOCK)
        avail = blk <= cur
        forced = (blk == 0) | (blk == cur) | (blk == cur - 1)
        gi = lax.broadcasted_iota(jnp.int32, (rows, rows), 0) % t_new
        gj = lax.broadcasted_iota(jnp.int32, (rows, rows), 1) % t_new
        gsum = jnp.where(gi == gj, 1.0, 0.0).astype(BF16)
        for kv in range(NSA_KV_HEADS):
            s = jnp.where(mask, s_ref[kv] - _slope_col(kv, rows, t_new) * dist, NEG_INF)
            e = jnp.exp(s - jnp.max(s, axis=-1, keepdims=True)) * maskf
            p = e / jnp.maximum(jnp.sum(e, axis=-1, keepdims=True), 1e-30)
            oc_ref[0, kv] = _dot(p.astype(BF16), cv_ref[:, kv * NSA_HEAD_DIM:(kv + 1) * NSA_HEAD_DIM])
            p_sum = sum(_dot(gsum, part) for part in _split3(p))
            p_slc = sum(_dot(part, mt_ref[...]) for part in _split3(p_sum))
            score = jnp.where(avail, p_slc + jnp.where(forced, FORCE_BONUS, 0.0), -1.0)
            score = jnp.where(blk < n_slc, score, -3.0)
            sel_ref[0, kv] = _select_top(score, 1, n_pad)


def _page_spec(layer, half, d):
    npg = SAMPLE_PAGES_PER_STEP
    return pl.BlockSpec((1, 1, PAGE_SIZE, 2 * NSA_KV_HEADS, NSA_HEAD_DIM),
                        lambda i, c, pt: (layer, pt[i, c * npg + d], 0, half, 0))


def _cmp_sample(cache, layer, page_table, q16, consts, t_new):
    db, n_pages = page_table.shape
    p_len = n_pages * PAGE_SIZE
    n_chunks = p_len // CMP_CHUNK
    n_r = p_len // CMP_STRIDE
    n_slc = -(-(p_len + t_new) // SLC_BLOCK)
    n_pad = -(-n_slc // LANES) * LANES
    mt = _slc_map(n_slc, n_r, n_pad).T
    pb, pbt, pe_term, phi = consts
    rows = NSA_GROUP * t_new
    full = lambda a: pl.BlockSpec(a.shape, lambda i, c, pt: (0,) * a.ndim)
    gs = pltpu.PrefetchScalarGridSpec(
        num_scalar_prefetch=1,
        grid=(db, n_chunks),
        in_specs=[_page_spec(layer, 0, d) for d in range(SAMPLE_PAGES_PER_STEP)]
                 + [pl.BlockSpec((1, NSA_KV_HEADS, rows, NSA_HEAD_DIM), lambda i, c, pt: (i, 0, 0, 0)),
                    full(pb), full(pbt), full(pe_term), full(phi), full(mt)],
        out_specs=[pl.BlockSpec((1, NSA_KV_HEADS, rows, NSA_HEAD_DIM), lambda i, c, pt: (i, 0, 0, 0)),
                   pl.BlockSpec((1, NSA_KV_HEADS, rows, n_pad), lambda i, c, pt: (i, 0, 0, 0))],
        scratch_shapes=[pltpu.VMEM((CMP_STRIDE, 2 * NSA_KV_DIM), BF16),
                        pltpu.VMEM((NSA_KV_HEADS, rows, n_r), F32),
                        pltpu.VMEM((n_r, NSA_KV_DIM), BF16)],
    )
    return pl.pallas_call(
        functools.partial(_cmp_sample_kernel, t_new=t_new, p_len=p_len),
        grid_spec=gs,
        out_shape=[jax.ShapeDtypeStruct((db, NSA_KV_HEADS, rows, NSA_HEAD_DIM), F32),
                   jax.ShapeDtypeStruct((db, NSA_KV_HEADS, rows, n_pad), F32)],
        compiler_params=_cparams("parallel", "arbitrary"),
        name="nsa_cmp_sample",
    )(page_table, *([cache] * SAMPLE_PAGES_PER_STEP), q16, pb, pbt, pe_term, phi, mt)


def _online_update(s, maskf, vv, m_ref, l_ref, acc_ref, kv):
    m_old = m_ref[kv]
    m_new = jnp.maximum(m_old, jnp.max(s, axis=-1, keepdims=True))
    a = jnp.exp(m_old - m_new)
    e = jnp.exp(s - m_new) * maskf
    l_ref[kv] = a * l_ref[kv] + jnp.sum(e, axis=-1, keepdims=True)
    acc_ref[kv] = a * acc_ref[kv] + _dot(e.astype(BF16), vv)
    m_ref[kv] = m_new


def _slc_sample_kernel(pt_ref, *refs, t_new, p_len):
    del pt_ref
    npg = SAMPLE_PAGES_PER_STEP
    page_refs = refs[:npg]
    q_ref, sel_ref, kn_ref, vn_ref, o_ref, m_ref, l_ref, acc_ref = refs[npg:]
    c = pl.program_id(1)
    rows = NSA_GROUP * t_new
    n_pad = sel_ref.shape[3]

    @pl.when(c == 0)
    def _():
        m_ref[...] = jnp.full_like(m_ref, NEG_INF)
        l_ref[...] = jnp.zeros_like(l_ref)
        acc_ref[...] = jnp.zeros_like(acc_ref)

    x = _gather_pages(page_refs)
    eb = lax.broadcasted_iota(jnp.int32, (n_pad, CMP_CHUNK), 0)
    ec = (c * CMP_CHUNK + lax.broadcasted_iota(jnp.int32, (n_pad, CMP_CHUNK), 1)) // SLC_BLOCK
    expand = jnp.where(eb == ec, 1.0, 0.0).astype(BF16)
    ti = lax.broadcasted_iota(jnp.int32, (rows, CMP_CHUNK), 0) % t_new
    kpos = c * CMP_CHUNK + lax.broadcasted_iota(jnp.int32, (rows, CMP_CHUNK), 1)
    dist = ((p_len + ti) - kpos).astype(F32)
    for kv in range(NSA_KV_HEADS):
        mask = _dot(sel_ref[0, kv], expand) > 0.5
        s = _dot_nt(q_ref[0, kv], x[:, kv * NSA_HEAD_DIM:(kv + 1) * NSA_HEAD_DIM])
        s = jnp.where(mask, s - _slope_col(kv, rows, t_new) * dist, NEG_INF)
        _online_update(s, mask.astype(F32), x[:, NSA_KV_DIM + kv * NSA_HEAD_DIM:NSA_KV_DIM + (kv + 1) * NSA_HEAD_DIM],
                       m_ref, l_ref, acc_ref, kv)

    @pl.when(c == pl.num_programs(1) - 1)
    def _():
        n_new = kn_ref.shape[1]
        ti2 = lax.broadcasted_iota(jnp.int32, (rows, n_new), 0) % t_new
        ci = lax.broadcasted_iota(jnp.int32, (rows, n_new), 1)
        mask2 = (ci <= ti2) & (ci < t_new)
        dist2 = (ti2 - ci).astype(F32)
        for kv in range(NSA_KV_HEADS):
            s = _dot_nt(q_ref[0, kv], kn_ref[0, :, kv * NSA_HEAD_DIM:(kv + 1) * NSA_HEAD_DIM])
            s = jnp.where(mask2, s - _slope_col(kv, rows, t_new) * dist2, NEG_INF)
            _online_update(s, mask2.astype(F32), vn_ref[0, :, kv * NSA_HEAD_DIM:(kv + 1) * NSA_HEAD_DIM],
                           m_ref, l_ref, acc_ref, kv)
            o_ref[0, kv] = acc_ref[kv] / jnp.maximum(l_ref[kv], 1e-30)


def _slc_sample(cache, layer, page_table, q16, sel16, k_new, v_new, t_new):
    db, n_pages = page_table.shape
    p_len = n_pages * PAGE_SIZE
    assert p_len % SLC_BLOCK == 0
    n_chunks = p_len // CMP_CHUNK
    rows = NSA_GROUP * t_new
    n_pad = sel16.shape[3]
    per_b = lambda a: pl.BlockSpec((1,) + a.shape[1:], lambda i, c, pt: (i,) + (0,) * (a.ndim - 1))
    gs = pltpu.PrefetchScalarGridSpec(
        num_scalar_prefetch=1,
        grid=(db, n_chunks),
        in_specs=[_page_spec(layer, 1, d) for d in range(SAMPLE_PAGES_PER_STEP)]
                 + [per_b(q16), per_b(sel16), per_b(k_new), per_b(v_new)],
        out_specs=pl.BlockSpec((1, NSA_KV_HEADS, rows, NSA_HEAD_DIM), lambda i, c, pt: (i, 0, 0, 0)),
        scratch_shapes=[pltpu.VMEM((NSA_KV_HEADS, rows, 1), F32),
                        pltpu.VMEM((NSA_KV_HEADS, rows, 1), F32),
                        pltpu.VMEM((NSA_KV_HEADS, rows, NSA_HEAD_DIM), F32)],
    )
    return pl.pallas_call(
        functools.partial(_slc_sample_kernel, t_new=t_new, p_len=p_len),
        grid_spec=gs,
        out_shape=jax.ShapeDtypeStruct((db, NSA_KV_HEADS, rows, NSA_HEAD_DIM), F32),
        compiler_params=_cparams("parallel", "arbitrary"),
        name="nsa_slc_sample",
    )(page_table, *([cache] * SAMPLE_PAGES_PER_STEP), q16, sel16, k_new, v_new)


def _win_sample_kernel(q_ref, wk_ref, wv_ref, kn_ref, vn_ref, o_ref, *, t_new):
    rows = NSA_GROUP * t_new
    wb = wk_ref.shape[1]
    n_new = kn_ref.shape[1]
    t1 = lax.broadcasted_iota(jnp.int32, (rows, wb), 0) % t_new
    c1 = lax.broadcasted_iota(jnp.int32, (rows, wb), 1)
    d1 = wb + t1 - c1
    mask1 = (d1 >= 0) & (d1 <= WINDOW)
    t2 = lax.broadcasted_iota(jnp.int32, (rows, n_new), 0) % t_new
    c2 = lax.broadcasted_iota(jnp.int32, (rows, n_new), 1)
    d2 = t2 - c2
    mask2 = (d2 >= 0) & (d2 <= WINDOW) & (c2 < t_new)
    for kv in range(NSA_KV_HEADS):
        lanes = slice(kv * NSA_HEAD_DIM, (kv + 1) * NSA_HEAD_DIM)
        slope = _slope_col(kv, rows, t_new)
        s1 = jnp.where(mask1, _dot_nt(q_ref[0, kv], wk_ref[0, :, lanes]) - slope * d1.astype(F32), NEG_INF)
        s2 = jnp.where(mask2, _dot_nt(q_ref[0, kv], kn_ref[0, :, lanes]) - slope * d2.astype(F32), NEG_INF)
        m = jnp.maximum(jnp.max(s1, axis=-1, keepdims=True), jnp.max(s2, axis=-1, keepdims=True))
        e1 = jnp.exp(s1 - m) * mask1.astype(F32)
        e2 = jnp.exp(s2 - m) * mask2.astype(F32)
        den = jnp.maximum(jnp.sum(e1, axis=-1, keepdims=True) + jnp.sum(e2, axis=-1, keepdims=True), 1e-30)
        o_ref[0, kv] = (_dot(e1.astype(BF16), wv_ref[0, :, lanes]) + _dot(e2.astype(BF16), vn_ref[0, :, lanes])) / den


def _win_sample(q16, wk, wv, k_new, v_new, t_new):
    db = q16.shape[0]
    rows = NSA_GROUP * t_new
    per_b = lambda a: pl.BlockSpec((1,) + a.shape[1:], lambda i: (i,) + (0,) * (a.ndim - 1))
    return pl.pallas_call(
        functools.partial(_win_sample_kernel, t_new=t_new),
        grid=(db,),
        in_specs=[per_b(q16), per_b(wk), per_b(wv), per_b(k_new), per_b(v_new)],
        out_specs=pl.BlockSpec((1, NSA_KV_HEADS, rows, NSA_HEAD_DIM), lambda i: (i, 0, 0, 0)),
        out_shape=jax.ShapeDtypeStruct((db, NSA_KV_HEADS, rows, NSA_HEAD_DIM), F32),
        compiler_params=_cparams("parallel"),
        name="nsa_win_sample",
    )(q16, wk, wv, k_new, v_new)


def _pad_rows(a, n):
    return jnp.pad(a, ((0, 0), (0, n - a.shape[1]), (0, 0)))


def _nsa_sample(x, cache, layer, win_l, page_table, weights):
    db, t, _ = x.shape
    w_main, w_gate, consts, w_out = weights
    xb = x.reshape(db * t, D_MODEL).astype(BF16)
    h = _mm(xb, w_main)
    hg = _mm(xb, w_gate)
    h3 = h.reshape(db, t, NSA_MAIN_DIM)
    q16 = (h3[..., :NSA_Q_DIM] * NSA_SCALE).astype(BF16).reshape(db, t, NSA_KV_HEADS, NSA_GROUP, NSA_HEAD_DIM)
    q16 = q16.transpose(0, 2, 3, 1, 4).reshape(db, NSA_KV_HEADS, NSA_GROUP * t, NSA_HEAD_DIM)
    kvn = h3[..., NSA_Q_DIM:].reshape(db, t, 6, NSA_KV_DIM)
    new = lambda slot: _pad_rows(kvn[:, :, slot].astype(BF16), LANES)
    cache = cache.reshape(cache.shape[:3] + (4 * NSA_KV_HEADS, NSA_HEAD_DIM))
    o_c, sel16 = _cmp_sample(cache, layer, page_table, q16, consts, t)
    o_s = _slc_sample(cache, layer, page_table, q16, sel16.astype(BF16), new(2), new(3), t)
    wb = win_l.shape[1]
    wkv = win_l.astype(BF16).reshape(db, wb, 2, NSA_KV_DIM)
    o_w = _win_sample(q16, wkv[:, :, 0], wkv[:, :, 1], new(4), new(5), t)
    unrow = lambda o: o.reshape(db, NSA_KV_HEADS, NSA_GROUP, t, NSA_HEAD_DIM).transpose(0, 3, 1, 2, 4).reshape(
        db * t, NSA_Q_DIM)
    o = _merge(unrow(o_c), unrow(o_s), unrow(o_w), hg)
    y = _mm(o, w_out).reshape(db, t, D_MODEL)
    kv = h3[..., NSA_Q_DIM:].reshape(db, t, 6, NSA_KV_HEADS, NSA_HEAD_DIM)
    wbuf = jnp.concatenate([win_l, kv[:, :, 4:]], axis=1)
    return y, kv[:, :, :4], wbuf[:, wbuf.shape[1] - wb:]


S5_SEQS = SUBLANES
S5_CB_GROUPS = 16
S5_CB_STATES = S5_CB_GROUPS * S5_STATE
S5_CB_CH = S5_CB_GROUPS * S5_GROUP_CH
S5_N_CB = S5_GROUPS // S5_CB_GROUPS
S5_MAX_STEPS = 128


def _s5_discretize(a_re, a_im, log_dt, b_re, b_im):
    dt = jnp.exp(log_dt.astype(F32))[:, None]
    mag = jnp.exp(a_re * dt)
    ab_re = mag * jnp.cos(a_im * dt)
    ab_im = mag * jnp.sin(a_im * dt)
    den = a_re * a_re + a_im * a_im
    nr = ab_re - 1.0
    f_re = (nr * a_re + ab_im * a_im) / den
    f_im = (ab_im * a_re - nr * a_im) / den
    bb_re = f_re[..., None] * b_re - f_im[..., None] * b_im
    bb_im = f_re[..., None] * b_im + f_im[..., None] * b_re
    return ab_re, ab_im, bb_re, bb_im


def _s5_weights(w_in, a_re, a_im, log_dt, b_re, b_im, c_re, c_im, d, w_glu):
    ab_re, ab_im, bb_re, bb_im = _s5_discretize(a_re, a_im, log_dt, b_re, b_im)
    eye = jnp.eye(S5_CB_GROUPS, dtype=F32)

    def bd_in(bb):
        x = bb.reshape(S5_N_CB, S5_CB_GROUPS, S5_STATE, S5_GROUP_CH)
        return jnp.einsum('ngpc,gh->ngchp', x, eye).reshape(S5_N_CB, S5_CB_CH, S5_CB_STATES)

    def bd_out(cc):
        x = cc.reshape(S5_N_CB, S5_CB_GROUPS, S5_GROUP_CH, S5_STATE)
        return jnp.einsum('ngcp,gh->ngphc', x, eye).reshape(S5_N_CB, S5_CB_STATES, S5_CB_CH)

    bmat = jnp.concatenate([bd_in(bb_re), bd_in(bb_im)], axis=2).astype(BF16)
    cmat = jnp.concatenate([bd_out(c_re.astype(F32)), -bd_out(c_im.astype(F32))], axis=1).astype(BF16)
    rep = lambda a: jnp.broadcast_to(a.reshape(S5_N_CB, 1, S5_CB_STATES), (S5_N_CB, S5_SEQS, S5_CB_STATES))
    return dict(w_in=w_in.astype(BF16), w_glu=w_glu.astype(BF16), bmat=bmat, cmat=cmat,
                ar=rep(ab_re), ai=rep(ab_im), ab_re=ab_re, ab_im=ab_im, d=d.reshape(1, D_MODEL).astype(F32))


def _s5_scan_kernel(u_ref, b_ref, c_ref, ar_ref, ai_ref, d_ref, h0r_ref, h0i_ref, *rest, steps, with_y):
    if with_y:
        y_ref, hr_ref, hi_ref, bu_ref = rest
    else:
        hr_ref, hi_ref, bu_ref = rest
    ns = S5_CB_STATES

    @pl.when(pl.program_id(1) == 0)
    def _():
        hr_ref[...] = h0r_ref[...]
        hi_ref[...] = h0i_ref[...]

    u = u_ref[...]
    bu_ref[...] = _dot(u.astype(BF16), b_ref[0])
    ar = ar_ref[0]
    ai = ai_ref[0]

    def step(s, carry):
        hr, hi = carry
        r0 = pl.multiple_of(s * S5_SEQS, S5_SEQS)
        br = bu_ref[pl.ds(r0, S5_SEQS), :ns]
        bi = bu_ref[pl.ds(r0, S5_SEQS), ns:]
        nr = ar * hr - ai * hi + br
        ni = ar * hi + ai * hr + bi
        if with_y:
            bu_ref[pl.ds(r0, S5_SEQS), :ns] = nr
            bu_ref[pl.ds(r0, S5_SEQS), ns:] = ni
        return nr, ni

    hr, hi = lax.fori_loop(0, steps, step, (hr_ref[0], hi_ref[0]))
    hr_ref[0] = hr
    hi_ref[0] = hi
    if with_y:
        y_ref[...] = _dot(bu_ref[...].astype(BF16), c_ref[0]) + d_ref[...] * u


def _s5_scan(u_rows, w, h0r, h0i, with_y=True):
    n_rows = u_rows.shape[0]
    n_steps = n_rows // S5_SEQS
    steps = min(S5_MAX_STEPS, n_steps)
    assert n_steps % steps == 0
    tr = steps * S5_SEQS
    cb_spec = lambda a: pl.BlockSpec((1,) + a.shape[1:], lambda cb, t: (cb,) + (0,) * (a.ndim - 1))
    y_spec = [pl.BlockSpec((tr, S5_CB_CH), lambda cb, t: (t, cb))] if with_y else []
    y_shape = [jax.ShapeDtypeStruct((n_rows, D_MODEL), F32)] if with_y else []
    return pl.pallas_call(
        functools.partial(_s5_scan_kernel, steps=steps, with_y=with_y),
        grid=(S5_N_CB, n_steps // steps),
        in_specs=[pl.BlockSpec((tr, S5_CB_CH), lambda cb, t: (t, cb)),
                  cb_spec(w['bmat']), cb_spec(w['cmat']), cb_spec(w['ar']), cb_spec(w['ai']),
                  pl.BlockSpec((1, S5_CB_CH), lambda cb, t: (0, cb)),
                  cb_spec(h0r), cb_spec(h0i)],
        out_specs=y_spec + [cb_spec(h0r), cb_spec(h0i)],
        out_shape=y_shape + [jax.ShapeDtypeStruct(h0r.shape, F32), jax.ShapeDtypeStruct(h0i.shape, F32)],
        scratch_shapes=[pltpu.VMEM((tr, 2 * S5_CB_STATES), F32)],
        compiler_params=_cparams("parallel", "arbitrary"),
        name="s5_scan",
    )(u_rows, w['bmat'], w['cmat'], w['ar'], w['ai'], w['d'], h0r, h0i)


def _state_to_blocks(h):
    return h.reshape(h.shape[0], S5_N_CB, S5_CB_STATES).transpose(1, 0, 2)


def _blocks_to_state(h):
    return h.transpose(1, 0, 2).reshape(h.shape[1], S5_GROUPS, S5_STATE)


def _cpow2(re, im, n):
    for _ in range(n):
        re, im = re * re - im * im, 2.0 * re * im
    return re, im


def _s5_prompt(x, w):
    b, t, _ = x.shape
    n_seg = S5_SEQS // b
    seg = t // n_seg
    assert n_seg * b == S5_SEQS and seg & (seg - 1) == 0
    xr = x.reshape(b, n_seg, seg, D_MODEL).transpose(2, 0, 1, 3).reshape(t * b, D_MODEL)
    u = _mm(xr.astype(BF16), w['w_in'])
    zero = jnp.zeros((S5_N_CB, S5_SEQS, S5_CB_STATES), F32)
    er, ei = _s5_scan(u, w, zero, zero, with_y=False)
    er = _blocks_to_state(er).reshape(b, n_seg, S5_GROUPS, S5_STATE)
    ei = _blocks_to_state(ei).reshape(b, n_seg, S5_GROUPS, S5_STATE)
    pr, pi = _cpow2(w['ab_re'], w['ab_im'], int(math.log2(seg)))
    sr = [jnp.zeros((b, S5_GROUPS, S5_STATE), F32)]
    si = [jnp.zeros((b, S5_GROUPS, S5_STATE), F32)]
    for k in range(n_seg - 1):
        sr.append(er[:, k] + pr * sr[k] - pi * si[k])
        si.append(ei[:, k] + pr * si[k] + pi * sr[k])
    h0r = _state_to_blocks(jnp.stack(sr, axis=1).reshape(S5_SEQS, S5_GROUPS, S5_STATE))
    h0i = _state_to_blocks(jnp.stack(si, axis=1).reshape(S5_SEQS, S5_GROUPS, S5_STATE))
    y, hr, hi = _s5_scan(u, w, h0r, h0i)
    hr = _blocks_to_state(hr).reshape(b, n_seg, S5_GROUPS, S5_STATE)[:, -1]
    hi = _blocks_to_state(hi).reshape(b, n_seg, S5_GROUPS, S5_STATE)[:, -1]
    z = _mm(_gelu_rows(y), w['w_glu'])
    unperm = lambda a: a.reshape(seg, b, n_seg, -1).transpose(1, 2, 0, 3).reshape(b * t, -1)
    return unperm(z), jnp.stack([hr, hi], axis=1)


def _s5_sample(x, h0, w):
    db, t, _ = x.shape
    assert db == S5_SEQS
    xr = x.transpose(1, 0, 2).reshape(t * db, D_MODEL)
    u = _mm(xr.astype(BF16), w['w_in'])
    y, hr, hi = _s5_scan(u, w, _state_to_blocks(h0[:, 0].astype(F32)), _state_to_blocks(h0[:, 1].astype(F32)))
    z = _mm(_gelu_rows(y), w['w_glu'])
    z = z.reshape(t, db, -1).transpose(1, 0, 2).reshape(db * t, -1)
    return z, jnp.stack([_blocks_to_state(hr), _blocks_to_state(hi)], axis=1).astype(h0.dtype)


def _gelu_kernel(y_ref, o_ref):
    o_ref[...] = _gelu_tanh(y_ref[...]).astype(o_ref.dtype)


def _gelu_rows(y):
    n, d = y.shape
    tm = min(256, n)
    row = pl.BlockSpec((tm, d), lambda i: (i, 0))
    return pl.pallas_call(_gelu_kernel, grid=(n // tm,), in_specs=[row], out_specs=row,
                          out_shape=jax.ShapeDtypeStruct((n, d), BF16),
                          compiler_params=_cparams("parallel"), name="gelu")(y)


ROUTER_TN = 256
N_RANKED = PEER_TOPK + 1
_CAND_PAIRS = [(i, j) for i in range(N_RANKED) for j in range(N_RANKED) if (i + 1) * (j + 1) <= N_RANKED]


def _top_values(s, n):
    vals = []
    cur = s
    for _ in range(n):
        m = jnp.max(cur, axis=0, keepdims=True)
        vals.append(m)
        cur = jnp.where(cur == m, -jnp.inf, cur)
    return vals


def _router_kernel(x_ref, wq_ref, keys_ref, w1z_ref, wthr_ref, w2_ref):
    q = _dot(x_ref[...], wq_ref[...])
    for h in range(PEER_HEADS):
        s, top = [], []
        for c in range(2):
            hc = 2 * h + c
            sc = _dot_nt(keys_ref[hc], q[:, hc * PEER_HALF:(hc + 1) * PEER_HALF].astype(BF16))
            s.append(sc)
            top.append(_top_values(sc, N_RANKED))
        cand = jnp.concatenate([top[0][i] + top[1][j] for i, j in _CAND_PAIRS], axis=0)
        ranked = _top_values(cand, N_RANKED)
        tau = 0.5 * (ranked[PEER_TOPK - 1] + ranked[PEER_TOPK])
        m1, m2 = top[0][0], top[1][0]
        z = jnp.sum(jnp.where(cand >= tau, jnp.exp(cand - (m1 + m2)), 0.0), axis=0, keepdims=True)
        w1z_ref[h] = jnp.exp(s[0] - m1) / z
        wthr_ref[h] = jnp.exp((tau - s[0]) - m2)
        w2_ref[h] = jnp.exp(s[1] - m2)


def _router(x, wq, keys):
    n = x.shape[0]
    tn = min(ROUTER_TN, n)
    assert n % tn == 0
    out = jax.ShapeDtypeStruct((PEER_HEADS, PEER_N_KEYS, n), F32)
    ospec = pl.BlockSpec((PEER_HEADS, PEER_N_KEYS, tn), lambda i: (0, 0, i))
    return pl.pallas_call(
        _router_kernel, grid=(n // tn,),
        in_specs=[pl.BlockSpec((tn, D_MODEL), lambda i: (i, 0)),
                  pl.BlockSpec(wq.shape, lambda i: (0, 0)),
                  pl.BlockSpec(keys.shape, lambda i: (0, 0, 0))],
        out_specs=[ospec, ospec, ospec], out_shape=[out, out, out],
        compiler_params=_cparams("parallel"), name="peer_router",
    )(x, wq, keys)


EXPERT_TN = 1024
EXPERT_LC = 256
EXPERT_TE = 512
EXPERT_NC = EXPERT_TE // PEER_N_KEYS
EXPERT_NT = PEER_N_EXPERTS // EXPERT_TE


def _expert_gate(w1z_ref, wthr_ref, w2_ref, c, lanes):
    g = None
    for h in range(PEER_HEADS):
        w2 = w2_ref[h, :, lanes]
        term = w1z_ref[h, pl.ds(c, 1), lanes] * jnp.where(w2 >= wthr_ref[h, pl.ds(c, 1), lanes], w2, 0.0)
        g = term if g is None else g + term
    return g


def _expert_kernel(xt_ref, u_ref, vt_ref, w1z_ref, wthr_ref, w2_ref, o_ref):
    j = pl.program_id(1)

    @pl.when(j == 0)
    def _():
        o_ref[...] = jnp.zeros_like(o_ref)

    ht = _dot(u_ref[...], xt_ref[...])
    parts = []
    for cc in range(EXPERT_NC):
        g = _expert_gate(w1z_ref, wthr_ref, w2_ref, j * EXPERT_NC + cc, slice(None))
        parts.append((_gelu_tanh(ht[cc * PEER_N_KEYS:(cc + 1) * PEER_N_KEYS]) * g).astype(BF16))
    o_ref[...] += _dot(vt_ref[...], jnp.concatenate(parts, axis=0))


def _expert_pipelined_kernel(xt_ref, u_ref, vt_ref, w1z_ref, wthr_ref, w2_ref, o_ref, ha_ref, hb_ref, p_ref):
    j = pl.program_id(1)
    nk = EXPERT_TN // EXPERT_LC

    @pl.when(j == 0)
    def _():
        o_ref[...] = jnp.zeros_like(o_ref)
        p_ref[...] = jnp.zeros_like(p_ref)
        hb_ref[...] = jnp.zeros_like(hb_ref)

    def stage(k, h_in, h_out):
        lanes = pl.ds(pl.multiple_of(k * EXPERT_LC, EXPERT_LC), EXPERT_LC)
        o_ref[:, lanes] += _dot(vt_ref[...], p_ref[:, lanes])
        kp = (k + nk - 1) % nk
        tile = jnp.minimum(jnp.where(k == 0, j - 1, j), EXPERT_NT - 1)
        tile = jnp.maximum(tile, 0)
        lp = pl.ds(pl.multiple_of(kp * EXPERT_LC, EXPERT_LC), EXPERT_LC)
        for cc in range(EXPERT_NC):
            rows = slice(cc * PEER_N_KEYS, (cc + 1) * PEER_N_KEYS)
            g = _expert_gate(w1z_ref, wthr_ref, w2_ref, tile * EXPERT_NC + cc, lp)
            p_ref[rows, lp] = (_gelu_tanh(h_in[rows, :]) * g).astype(BF16)
        h_out[...] = _dot(u_ref[...], xt_ref[:, lanes])

    def two_stages(k2, carry):
        stage(2 * k2, hb_ref, ha_ref)
        stage(2 * k2 + 1, ha_ref, hb_ref)
        return carry

    lax.fori_loop(0, nk // 2, two_stages, 0)


def _expert(xt, u, vt, layer, w1z, wthr, w2):
    n = xt.shape[1]
    out_shape = jax.ShapeDtypeStruct((D_MODEL, n), F32)
    if n % EXPERT_TN == 0:
        tn, nt = EXPERT_TN, EXPERT_NT
        once = dict(pipeline_mode=pl.Buffered(1))
        rspec = pl.BlockSpec((PEER_HEADS, PEER_N_KEYS, tn), lambda i, j: (0, 0, i), **once)
        return pl.pallas_call(
            _expert_pipelined_kernel, grid=(n // tn, nt + 1),
            in_specs=[pl.BlockSpec((D_MODEL, tn), lambda i, j: (0, i), **once),
                      pl.BlockSpec((None, EXPERT_TE, D_MODEL), lambda i, j: (layer, jnp.minimum(j, nt - 1), 0)),
                      pl.BlockSpec((None, D_MODEL, EXPERT_TE), lambda i, j: (layer, 0, jnp.maximum(j - 1, 0))),
                      rspec, rspec, rspec],
            out_specs=pl.BlockSpec((D_MODEL, tn), lambda i, j: (0, i)),
            out_shape=out_shape,
            scratch_shapes=[pltpu.VMEM((EXPERT_TE, EXPERT_LC), F32), pltpu.VMEM((EXPERT_TE, EXPERT_LC), F32),
                            pltpu.VMEM((EXPERT_TE, tn), BF16)],
            compiler_params=_cparams("parallel", "arbitrary"), name="peer_expert",
        )(xt, u, vt, w1z, wthr, w2)
    tn = min(EXPERT_TN // 2, n)
    assert n % tn == 0
    rspec = pl.BlockSpec((PEER_HEADS, PEER_N_KEYS, tn), lambda i, j: (0, 0, i))
    return pl.pallas_call(
        _expert_kernel, grid=(n // tn, EXPERT_NT),
        in_specs=[pl.BlockSpec((D_MODEL, tn), lambda i, j: (0, i)),
                  pl.BlockSpec((None, EXPERT_TE, D_MODEL), lambda i, j: (layer, j, 0)),
                  pl.BlockSpec((None, D_MODEL, EXPERT_TE), lambda i, j: (layer, 0, j)),
                  rspec, rspec, rspec],
        out_specs=pl.BlockSpec((D_MODEL, tn), lambda i, j: (0, i)),
        out_shape=out_shape,
        compiler_params=_cparams("parallel", "arbitrary"), name="peer_expert_small",
    )(xt, u, vt, w1z, wthr, w2)


def _peer_tables(peer_u, peer_v):
    return peer_u.astype(BF16), jnp.swapaxes(peer_v, 1, 2).astype(BF16)


def _peer(x, w_q, sub_keys, tables, layer):
    u, vt = tables
    n = x.shape[0]
    n_pad = -(-n // LANES) * LANES
    xb = jnp.pad(x, ((0, n_pad - n), (0, 0))).astype(BF16)
    keys = sub_keys.reshape(2 * PEER_HEADS, PEER_N_KEYS, PEER_HALF).astype(BF16)
    w1z, wthr, w2 = _router(xb, w_q.astype(BF16), keys)
    out_t = _expert(xb.T, u, vt, layer, w1z, wthr, w2)
    return out_t.T[:n]


def kernel(x_prompt, x_sample, cache_nsa, state_win, state_s5, page_table, nsa_w_in, nsa_cmp_pool, nsa_cmp_pe,
           nsa_cmp_phi, nsa_w_out, s5_w_in, s5_a_re, s5_a_im, s5_log_dt, s5_b_re, s5_b_im, s5_c_re, s5_c_im, s5_d,
           s5_w_glu, peer_w_q, peer_sub_keys, peer_u, peer_v, ln_g, ln_b):
    b, s, _ = x_prompt.shape
    db, t, _ = x_sample.shape
    xp = x_prompt.reshape(b * s, D_MODEL)
    xs = x_sample.reshape(db * t, D_MODEL)
    rows_p, rows_s, win_p, win_s, s5_p, s5_s = [], [], [], [], [], []
    tables = _peer_tables(peer_u, peer_v)
    for layer in range(DEPTH):
        j = layer // N_MIXERS
        if layer % N_MIXERS == 0:
            w = _nsa_weights(nsa_w_in[j], nsa_cmp_pool[j], nsa_cmp_pe[j], nsa_cmp_phi[j], nsa_w_out[j])
            mp, rp, wp = _nsa_prompt(xp.reshape(b, s, D_MODEL), w)
            ms, rs, ws = _nsa_sample(xs.reshape(db, t, D_MODEL), cache_nsa, j, state_win[j], page_table, w)
            rows_p.append(rp)
            rows_s.append(rs)
            win_p.append(wp)
            win_s.append(ws)
            xp = _ln_res(xp, mp.reshape(b * s, D_MODEL), ln_g[layer, 0], ln_b[layer, 0])
            xs = _ln_res(xs, ms.reshape(db * t, D_MODEL), ln_g[layer, 0], ln_b[layer, 0])
        else:
            w = _s5_weights(s5_w_in[j], s5_a_re[j], s5_a_im[j], s5_log_dt[j], s5_b_re[j], s5_b_im[j],
                            s5_c_re[j], s5_c_im[j], s5_d[j], s5_w_glu[j])
            zp, hp = _s5_prompt(xp.reshape(b, s, D_MODEL), w)
            zs, hs = _s5_sample(xs.reshape(db, t, D_MODEL), state_s5[j], w)
            s5_p.append(hp.astype(state_s5.dtype))
            s5_s.append(hs)
            xp = _ln_res(xp, zp, ln_g[layer, 0], ln_b[layer, 0], glu=True)
            xs = _ln_res(xs, zs, ln_g[layer, 0], ln_b[layer, 0], glu=True)
        xp = _ln_res(xp, _peer(xp, peer_w_q[layer], peer_sub_keys[layer], tables, layer),
                     ln_g[layer, 1], ln_b[layer, 1])
        xs = _ln_res(xs, _peer(xs, peer_w_q[layer], peer_sub_keys[layer], tables, layer),
                     ln_g[layer, 1], ln_b[layer, 1])
    return (xp.reshape(b, s, D_MODEL), xs.reshape(db, t, D_MODEL), jnp.stack(rows_p), jnp.stack(rows_s),
            jnp.stack(win_p), jnp.stack(win_s), jnp.stack(s5_p), jnp.stack(s5_s))
```

```python
import functools
import math

import numpy as np
import jax
import jax.numpy as jnp
from jax import lax
from jax.experimental import pallas as pl
from jax.experimental.pallas import tpu as pltpu

F32 = jnp.float32
BF16 = jnp.bfloat16

D_MODEL = 2048
DEPTH = 4
PAGE_SIZE = 128
N_MIXERS = 2
NSA_HEADS = 16
NSA_KV_HEADS = 4
NSA_GROUP = NSA_HEADS // NSA_KV_HEADS
NSA_HEAD_DIM = D_MODEL // NSA_HEADS
NSA_Q_DIM = NSA_HEADS * NSA_HEAD_DIM
NSA_KV_DIM = NSA_KV_HEADS * NSA_HEAD_DIM
NSA_MAIN_DIM = NSA_Q_DIM + 6 * NSA_KV_DIM
NSA_N_GATES = 3 * NSA_HEADS
NSA_SCALE = NSA_HEAD_DIM ** -0.5
CMP_BLOCK = 32
CMP_STRIDE = 16
SLC_BLOCK = 64
SLC_TOP_N = 16
WINDOW = 512
FORCE_BONUS = 1.0e4
S5_GROUP_CH = 16
S5_GROUPS = D_MODEL // S5_GROUP_CH
S5_STATE = 64
PEER_HEADS = 8
PEER_N_KEYS = 128
PEER_N_EXPERTS = PEER_N_KEYS ** 2
PEER_HALF = 128
PEER_TOPK = 16
LN_EPS = 1e-5
NEG_INF = -1.0e30
ALPHA = (2.0 * DEPTH) ** 0.25

LANES = 128
SUBLANES = 8
VMEM_LIMIT_BYTES = 56 * 1024 * 1024

CMP_CHUNK = 2048
CMP_ROWS = CMP_CHUNK // CMP_STRIDE
SAMPLE_PAGES_PER_STEP = CMP_CHUNK // PAGE_SIZE

_SLOPES = [2.0 ** (-8.0 * (h + 1) / NSA_HEADS) for h in range(NSA_HEADS)]


def _cparams(*sem):
    return pltpu.CompilerParams(dimension_semantics=sem, vmem_limit_bytes=VMEM_LIMIT_BYTES)


def _dot(a, b):
    return jnp.dot(a, b, preferred_element_type=F32)


def _dot_nt(a, b):
    return lax.dot_general(a, b, (((1,), (1,)), ((), ())), preferred_element_type=F32)


def _split3(x):
    hi = x.astype(BF16)
    r1 = x - hi.astype(F32)
    mid = r1.astype(BF16)
    lo = (r1 - mid.astype(F32)).astype(BF16)
    return hi, mid, lo


def _gelu_tanh(x):
    c = math.sqrt(2.0 / math.pi)
    return 0.5 * x * (1.0 + jnp.tanh(c * (x + 0.044715 * (x * x * x))))


def _mm_kernel(a_ref, b_ref, o_ref):
    o_ref[...] = _dot(a_ref[...], b_ref[...]).astype(o_ref.dtype)


def _mm(a, b, tm=512, tn=1024, out_dtype=F32):
    m, k = a.shape
    n = b.shape[1]
    tm = min(tm, m)
    tn = min(tn, n)
    assert m % tm == 0 and n % tn == 0
    return pl.pallas_call(
        _mm_kernel,
        grid=(n // tn, m // tm),
        in_specs=[pl.BlockSpec((tm, k), lambda j, i: (i, 0)),
                  pl.BlockSpec((k, tn), lambda j, i: (0, j))],
        out_specs=pl.BlockSpec((tm, tn), lambda j, i: (i, j)),
        out_shape=jax.ShapeDtypeStruct((m, n), out_dtype),
        compiler_params=_cparams("parallel", "parallel"),
        name="mm",
    )(a, b)


def _ln_body(y, g_ref, b_ref, o_ref):
    mu = jnp.mean(y, axis=-1, keepdims=True)
    yc = y - mu
    var = jnp.mean(yc * yc, axis=-1, keepdims=True)
    o_ref[...] = yc * lax.rsqrt(var + LN_EPS) * g_ref[...] + b_ref[...]


def _ln_kernel(x_ref, m_ref, g_ref, b_ref, o_ref):
    _ln_body(ALPHA * x_ref[...] + m_ref[...], g_ref, b_ref, o_ref)


def _ln_glu_kernel(x_ref, z1_ref, z2_ref, g_ref, b_ref, o_ref):
    z2 = z2_ref[...]
    mix = z1_ref[...] * (1.0 / (1.0 + jnp.exp(-z2)))
    _ln_body(ALPHA * x_ref[...] + mix, g_ref, b_ref, o_ref)


def _ln_res(x, m, g, b, glu=False):
    n, d = x.shape
    tm = min(256, n)
    assert n % tm == 0
    row = pl.BlockSpec((tm, d), lambda i: (i, 0))
    vec = pl.BlockSpec((1, d), lambda i: (0, 0))
    if glu:
        ins = [row, row, pl.BlockSpec((tm, d), lambda i: (i, 1)), vec, vec]
        args = (x, m, m, g.reshape(1, d), b.reshape(1, d))
        kern = _ln_glu_kernel
    else:
        ins = [row, row, vec, vec]
        args = (x, m, g.reshape(1, d), b.reshape(1, d))
        kern = _ln_kernel
    return pl.pallas_call(
        kern, grid=(n // tm,), in_specs=ins, out_specs=row,
        out_shape=jax.ShapeDtypeStruct((n, d), F32),
        compiler_params=_cparams("parallel"), name="ln_res",
    )(*args)


def _compress_consts(pool, pe, phi):
    eye_cur = jnp.eye(CMP_ROWS, dtype=F32)
    eye_prev = jnp.eye(CMP_ROWS, k=-1, dtype=F32)
    lo = pool[:, :, None, None, :CMP_STRIDE]
    hi = pool[:, :, None, None, CMP_STRIDE:]
    pb = (eye_prev[None, None, :, :, None] * lo + eye_cur[None, None, :, :, None] * hi).reshape(
        2, NSA_KV_HEADS, CMP_ROWS, CMP_CHUNK)
    first = jnp.asarray((np.arange(CMP_ROWS) == 0).astype(np.float32))[None, None, :, None]
    pbt = first * pool[:, :, None, :CMP_STRIDE]
    pe_term = jnp.einsum('skj,skjd->skd', pool, pe)
    return (pb.astype(BF16), pbt.astype(BF16), pe_term.reshape(2 * NSA_KV_HEADS, NSA_HEAD_DIM).astype(F32),
            phi.astype(BF16))


def _compress_chunk(x, tail, pb_ref, pbt_ref, pe_ref, phi_ref):
    outs = []
    for slot in range(2):
        for kv in range(NSA_KV_HEADS):
            lo = slot * NSA_KV_DIM + kv * NSA_HEAD_DIM
            xs = x[:, lo:lo + NSA_HEAD_DIM]
            pooled = _dot(pb_ref[slot, kv], xs) + _dot(pbt_ref[slot, kv], tail[:, lo:lo + NSA_HEAD_DIM])
            pooled = pooled + pe_ref[pl.ds(slot * NSA_KV_HEADS + kv, 1), :]
            outs.append(_dot(pooled.astype(BF16), phi_ref[slot, kv]))
    return jnp.concatenate(outs, axis=1)


def _slope_col(kv, rows, per):
    gi = lax.broadcasted_iota(jnp.int32, (rows, 1), 0) // per
    col = jnp.full((rows, 1), _SLOPES[kv * NSA_GROUP + NSA_GROUP - 1], F32)
    for g in range(NSA_GROUP - 1):
        col = jnp.where(gi == g, _SLOPES[kv * NSA_GROUP + g], col)
    return col


def _select_top(score, axis, n_entries):
    idx = lax.broadcasted_iota(jnp.int32, score.shape, axis).astype(F32)
    sel = jnp.zeros(score.shape, F32)
    cur = score
    for _ in range(SLC_TOP_N):
        m = jnp.max(cur, axis=axis, keepdims=True)
        first = jnp.min(jnp.where(cur == m, idx, float(n_entries)), axis=axis, keepdims=True)
        hit = idx == first
        sel = jnp.where(hit, 1.0, sel)
        cur = jnp.where(hit, -2.0, cur)
    return sel


def _compress_kernel(x_ref, pb_ref, pbt_ref, pe_ref, phi_ref, o_ref, tail_ref):
    @pl.when(pl.program_id(1) == 0)
    def _():
        tail_ref[...] = jnp.zeros_like(tail_ref)

    x = x_ref[0]
    o_ref[0] = _compress_chunk(x, tail_ref[...], pb_ref, pbt_ref, pe_ref, phi_ref).astype(o_ref.dtype)
    tail_ref[...] = x[CMP_CHUNK - CMP_STRIDE:, :]


def _compress(kv01, consts):
    b, s, w = kv01.shape
    n_chunks = s // CMP_CHUNK
    pb, pbt, pe_term, phi = consts
    full = lambda a: pl.BlockSpec(a.shape, lambda i, c: (0,) * a.ndim)
    return pl.pallas_call(
        _compress_kernel,
        grid=(b, n_chunks),
        in_specs=[pl.BlockSpec((1, CMP_CHUNK, w), lambda i, c: (i, c, 0)),
                  full(pb), full(pbt), full(pe_term), full(phi)],
        out_specs=pl.BlockSpec((1, CMP_ROWS, w), lambda i, c: (i, c, 0)),
        out_shape=jax.ShapeDtypeStruct((b, s // CMP_STRIDE, w), BF16),
        scratch_shapes=[pltpu.VMEM((CMP_STRIDE, w), BF16)],
        compiler_params=_cparams("parallel", "arbitrary"),
        name="nsa_compress",
    )(kv01, pb, pbt, pe_term, phi)


CMP_TQ = 128


def _cmp_prompt_kernel(q_ref, ckv_ref, mt_ref, oc_ref, sel_ref):
    tq = CMP_TQ
    n_r = ckv_ref.shape[1]
    n_slc = mt_ref.shape[0]
    q0 = pl.program_id(1) * tq
    qpos = q0 + lax.broadcasted_iota(jnp.int32, (tq, n_r), 0)
    r = lax.broadcasted_iota(jnp.int32, (tq, n_r), 1)
    dist_i = qpos - (CMP_STRIDE * r + CMP_STRIDE - 1)
    mask = (dist_i >= 0) & (r >= 1)
    maskf = mask.astype(F32)
    dist = dist_i.astype(F32)
    q = q_ref[0]
    ckv = ckv_ref[0]
    blk = lax.broadcasted_iota(jnp.int32, (n_slc, tq), 0)
    cur = (q0 + lax.broadcasted_iota(jnp.int32, (n_slc, tq), 1)) // SLC_BLOCK
    avail = blk <= cur
    forced = (blk == 0) | (blk == cur) | (blk == cur - 1)
    for kv in range(NSA_KV_HEADS):
        ck = ckv[:, kv * NSA_HEAD_DIM:(kv + 1) * NSA_HEAD_DIM]
        cv = ckv[:, NSA_KV_DIM + kv * NSA_HEAD_DIM:NSA_KV_DIM + (kv + 1) * NSA_HEAD_DIM]
        qs = jnp.concatenate([q[:, (kv * NSA_GROUP + g) * NSA_HEAD_DIM:(kv * NSA_GROUP + g + 1) * NSA_HEAD_DIM]
                              for g in range(NSA_GROUP)], axis=0)
        s_all = _dot_nt(qs, ck)
        p_sum = jnp.zeros((tq, n_r), F32)
        for g in range(NSA_GROUP):
            h = kv * NSA_GROUP + g
            s = jnp.where(mask, s_all[g * tq:(g + 1) * tq] - _SLOPES[h] * dist, NEG_INF)
            e = jnp.exp(s - jnp.max(s, axis=-1, keepdims=True)) * maskf
            p = e / jnp.maximum(jnp.sum(e, axis=-1, keepdims=True), 1e-30)
            oc_ref[0, :, h * NSA_HEAD_DIM:(h + 1) * NSA_HEAD_DIM] = _dot(p.astype(BF16), cv)
            p_sum = p_sum + p
        mt = mt_ref[...]
        p_slc = sum(_dot_nt(mt, part) for part in _split3(p_sum))
        score = jnp.where(avail, p_slc + jnp.where(forced, FORCE_BONUS, 0.0), -1.0)
        sel_ref[0, kv] = _select_top(score, 0, n_slc)


def _slc_map(n_slc, n_r, n_pad):
    m = np.arange(n_pad)[:, None]
    r = np.arange(n_r)[None, :]
    ratio = SLC_BLOCK // CMP_STRIDE
    return jnp.asarray(((r >= ratio * m) & (r <= ratio * m + ratio) & (m < n_slc)).astype(np.float32), BF16)


def _cmp_prompt(qs, ckv):
    b, s, _ = qs.shape
    n_r = ckv.shape[1]
    n_slc = s // SLC_BLOCK
    mt = _slc_map(n_slc, n_r, n_slc)
    return pl.pallas_call(
        _cmp_prompt_kernel,
        grid=(b, s // CMP_TQ),
        in_specs=[pl.BlockSpec((1, CMP_TQ, NSA_Q_DIM), lambda i, t: (i, t, 0)),
                  pl.BlockSpec((1, n_r, 2 * NSA_KV_DIM), lambda i, t: (i, 0, 0)),
                  pl.BlockSpec(mt.shape, lambda i, t: (0, 0))],
        out_specs=[pl.BlockSpec((1, CMP_TQ, NSA_Q_DIM), lambda i, t: (i, t, 0)),
                   pl.BlockSpec((1, NSA_KV_HEADS, n_slc, CMP_TQ), lambda i, t: (i, 0, 0, t))],
        out_shape=[jax.ShapeDtypeStruct((b, s, NSA_Q_DIM), F32),
                   jax.ShapeDtypeStruct((b, NSA_KV_HEADS, n_slc, s), F32)],
        compiler_params=_cparams("parallel", "parallel"),
        name="nsa_cmp_prompt",
    )(qs, ckv, mt)


SLC_TQ = 256
SLC_TK = 512


def _slc_prompt_kernel(q_ref, k_ref, v_ref, sel_ref, o_ref, m_ref, l_ref, acc_ref):
    tq, tk = SLC_TQ, SLC_TK
    qi = pl.program_id(1)
    kj = pl.program_id(2)
    n_sel = sel_ref.shape[3]

    @pl.when(kj == 0)
    def _():
        m_ref[...] = jnp.full_like(m_ref, NEG_INF)
        l_ref[...] = jnp.zeros_like(l_ref)
        acc_ref[...] = jnp.zeros_like(acc_ref)

    @pl.when(kj * tk <= qi * tq + tq - 1)
    def _():
        q = q_ref[0]
        kt = k_ref[0]
        vt = v_ref[0]
        qpos = qi * tq + lax.broadcasted_iota(jnp.int32, (tq, tk), 0)
        kpos = kj * tk + lax.broadcasted_iota(jnp.int32, (tq, tk), 1)
        dist_i = qpos - kpos
        dist = dist_i.astype(F32)
        eb = lax.broadcasted_iota(jnp.int32, (n_sel, tk), 0)
        ec = (kj * tk + lax.broadcasted_iota(jnp.int32, (n_sel, tk), 1)) // SLC_BLOCK
        expand = jnp.where(eb == ec, 1.0, 0.0).astype(BF16)
        for kv in range(NSA_KV_HEADS):
            picked = _dot(sel_ref[0, kv], expand)
            mask = (picked > 0.5) & (dist_i >= 0)
            qs = jnp.concatenate([q[:, (kv * NSA_GROUP + g) * NSA_HEAD_DIM:(kv * NSA_GROUP + g + 1) * NSA_HEAD_DIM]
                                  for g in range(NSA_GROUP)], axis=0)
            s_all = _dot_nt(qs, kt[:, kv * NSA_HEAD_DIM:(kv + 1) * NSA_HEAD_DIM])
            vv = vt[:, kv * NSA_HEAD_DIM:(kv + 1) * NSA_HEAD_DIM]
            for g in range(NSA_GROUP):
                h = kv * NSA_GROUP + g
                rows = slice(g * tq, (g + 1) * tq)
                s = jnp.where(mask, s_all[rows] - _SLOPES[h] * dist, NEG_INF)
                m_old = m_ref[kv, rows]
                m_new = jnp.maximum(m_old, jnp.max(s, axis=-1, keepdims=True))
                a = jnp.exp(m_old - m_new)
                e = jnp.exp(s - m_new)
                l_ref[kv, rows] = a * l_ref[kv, rows] + jnp.sum(e, axis=-1, keepdims=True)
                acc_ref[kv, rows] = a * acc_ref[kv, rows] + _dot(e.astype(BF16), vv)
                m_ref[kv, rows] = m_new

    @pl.when(kj == pl.num_programs(2) - 1)
    def _():
        for kv in range(NSA_KV_HEADS):
            for g in range(NSA_GROUP):
                h = kv * NSA_GROUP + g
                rows = slice(g * tq, (g + 1) * tq)
                o_ref[0, :, h * NSA_HEAD_DIM:(h + 1) * NSA_HEAD_DIM] = (
                    acc_ref[kv, rows] / jnp.maximum(l_ref[kv, rows], 1e-30))


def _slc_prompt(qs, k_slc, v_slc, sel):
    b, s, _ = qs.shape
    tq, tk = min(SLC_TQ, s), min(SLC_TK, s)
    assert tq == SLC_TQ and tk == SLC_TK
    n_sel = sel.shape[3]
    last = lambda t: (t * tq + tq - 1) // tk
    return pl.pallas_call(
        _slc_prompt_kernel,
        grid=(b, s // tq, s // tk),
        in_specs=[pl.BlockSpec((1, tq, NSA_Q_DIM), lambda i, t, j: (i, t, 0)),
                  pl.BlockSpec((1, tk, NSA_KV_DIM), lambda i, t, j: (i, jnp.minimum(j, last(t)), 0)),
                  pl.BlockSpec((1, tk, NSA_KV_DIM), lambda i, t, j: (i, jnp.minimum(j, last(t)), 0)),
                  pl.BlockSpec((1, NSA_KV_HEADS, tq, n_sel), lambda i, t, j: (i, 0, t, 0))],
        out_specs=pl.BlockSpec((1, tq, NSA_Q_DIM), lambda i, t, j: (i, t, 0)),
        out_shape=jax.ShapeDtypeStruct((b, s, NSA_Q_DIM), F32),
        scratch_shapes=[pltpu.VMEM((NSA_KV_HEADS, NSA_GROUP * tq, 1), F32),
                        pltpu.VMEM((NSA_KV_HEADS, NSA_GROUP * tq, 1), F32),
                        pltpu.VMEM((NSA_KV_HEADS, NSA_GROUP * tq, NSA_HEAD_DIM), F32)],
        compiler_params=_cparams("parallel", "parallel", "arbitrary"),
        name="nsa_slc_prompt",
    )(qs, k_slc, v_slc, sel)


WIN_TQ = 256
WIN_NT = WINDOW // WIN_TQ + 1


def _win_prompt_kernel(q_ref, *refs):
    tq = WIN_TQ
    k_refs = refs[:WIN_NT]
    v_refs = refs[WIN_NT:2 * WIN_NT]
    o_ref = refs[2 * WIN_NT]
    qi = pl.program_id(1)
    nk = WIN_NT * tq
    row = lax.broadcasted_iota(jnp.int32, (tq, nk), 0)
    col = lax.broadcasted_iota(jnp.int32, (tq, nk), 1)
    dist_i = WINDOW + row - col
    kpos = (qi - (WIN_NT - 1)) * tq + col
    mask = (dist_i >= 0) & (dist_i <= WINDOW) & (kpos >= 0)
    dist = dist_i.astype(F32)
    q = q_ref[0]
    kcat = jnp.concatenate([r[0] for r in k_refs], axis=0)
    vcat = jnp.concatenate([r[0] for r in v_refs], axis=0)
    for kv in range(NSA_KV_HEADS):
        qs = jnp.concatenate([q[:, (kv * NSA_GROUP + g) * NSA_HEAD_DIM:(kv * NSA_GROUP + g + 1) * NSA_HEAD_DIM]
                              for g in range(NSA_GROUP)], axis=0)
        s_all = _dot_nt(qs, kcat[:, kv * NSA_HEAD_DIM:(kv + 1) * NSA_HEAD_DIM])
        vv = vcat[:, kv * NSA_HEAD_DIM:(kv + 1) * NSA_HEAD_DIM]
        for g in range(NSA_GROUP):
            h = kv * NSA_GROUP + g
            s = jnp.where(mask, s_all[g * tq:(g + 1) * tq] - _SLOPES[h] * dist, NEG_INF)
            e = jnp.exp(s - jnp.max(s, axis=-1, keepdims=True))
            den = jnp.maximum(jnp.sum(e, axis=-1, keepdims=True), 1e-30)
            o_ref[0, :, h * NSA_HEAD_DIM:(h + 1) * NSA_HEAD_DIM] = _dot(e.astype(BF16), vv) / den


def _win_prompt(qs, k_win, v_win):
    b, s, _ = qs.shape
    tq = WIN_TQ
    kspec = lambda d: pl.BlockSpec((1, tq, NSA_KV_DIM),
                                   lambda i, t: (i, jnp.maximum(t - (WIN_NT - 1) + d, 0), 0))
    return pl.pallas_call(
        _win_prompt_kernel,
        grid=(b, s // tq),
        in_specs=[pl.BlockSpec((1, tq, NSA_Q_DIM), lambda i, t: (i, t, 0))]
                 + [kspec(d) for d in range(WIN_NT)] + [kspec(d) for d in range(WIN_NT)],
        out_specs=pl.BlockSpec((1, tq, NSA_Q_DIM), lambda i, t: (i, t, 0)),
        out_shape=jax.ShapeDtypeStruct((b, s, NSA_Q_DIM), F32),
        compiler_params=_cparams("parallel", "parallel"),
        name="nsa_win_prompt",
    )(qs, *([k_win] * WIN_NT), *([v_win] * WIN_NT))


def _merge_kernel(oc_ref, os_ref, ow_ref, hg_ref, ex_ref, o_ref):
    gate = 1.0 / (1.0 + jnp.exp(-hg_ref[...]))
    parts = _split3(gate)[:2]
    out = None
    for br, ref in enumerate((oc_ref, os_ref, ow_ref)):
        gx = sum(_dot(p, ex_ref[br]) for p in parts)
        term = gx * ref[...]
        out = term if out is None else out + term
    o_ref[...] = out.astype(o_ref.dtype)


def _gate_expand():
    e = np.zeros((3, LANES, NSA_Q_DIM), np.float32)
    for br in range(3):
        for h in range(NSA_HEADS):
            e[br, h * 3 + br, h * NSA_HEAD_DIM:(h + 1) * NSA_HEAD_DIM] = 1.0
    return jnp.asarray(e, BF16)


def _merge(oc, os_, ow, hg):
    n = oc.shape[0]
    tm = min(256, n)
    ex = _gate_expand()
    row = pl.BlockSpec((tm, NSA_Q_DIM), lambda i: (i, 0))
    return pl.pallas_call(
        _merge_kernel, grid=(n // tm,),
        in_specs=[row, row, row, pl.BlockSpec((tm, LANES), lambda i: (i, 0)),
                  pl.BlockSpec(ex.shape, lambda i: (0, 0, 0))],
        out_specs=row,
        out_shape=jax.ShapeDtypeStruct((n, NSA_Q_DIM), BF16),
        compiler_params=_cparams("parallel"), name="nsa_merge",
    )(oc, os_, ow, hg, ex)


def _nsa_weights(w_in, pool, pe, phi, w_out):
    w_main = w_in[:, :NSA_MAIN_DIM].astype(BF16)
    w_gate = jnp.pad(w_in[:, NSA_MAIN_DIM:], ((0, 0), (0, LANES - NSA_N_GATES))).astype(BF16)
    return w_main, w_gate, _compress_consts(pool, pe, phi), w_out.astype(BF16)


def _nsa_prompt(x, weights):
    b, s, _ = x.shape
    w_main, w_gate, consts, w_out = weights
    xb = x.reshape(b * s, D_MODEL).astype(BF16)
    h = _mm(xb, w_main)
    hg = _mm(xb, w_gate)
    h3 = h.reshape(b, s, NSA_MAIN_DIM)
    qs = (h3[..., :NSA_Q_DIM] * NSA_SCALE).astype(BF16)
    kvb = h3[..., NSA_Q_DIM:].astype(BF16)
    ckv = _compress(kvb[..., :2 * NSA_KV_DIM], consts)
    o_c, sel_t = _cmp_prompt(qs, ckv)
    sel = jnp.swapaxes(sel_t, 2, 3).astype(BF16)
    o_s = _slc_prompt(qs, kvb[..., 2 * NSA_KV_DIM:3 * NSA_KV_DIM], kvb[..., 3 * NSA_KV_DIM:4 * NSA_KV_DIM], sel)
    o_w = _win_prompt(qs, kvb[..., 4 * NSA_KV_DIM:5 * NSA_KV_DIM], kvb[..., 5 * NSA_KV_DIM:])
    o = _merge(o_c.reshape(b * s, NSA_Q_DIM), o_s.reshape(b * s, NSA_Q_DIM), o_w.reshape(b * s, NSA_Q_DIM), hg)
    y = _mm(o, w_out).reshape(b, s, D_MODEL)
    kv = h3[..., NSA_Q_DIM:].reshape(b, s, 6, NSA_KV_HEADS, NSA_HEAD_DIM)
    rows = kv[:, :, :4].reshape(b, s // PAGE_SIZE, PAGE_SIZE, 4, NSA_KV_HEADS, NSA_HEAD_DIM)
    win = kv[:, s - min(WINDOW, s):, 4:]
    return y, rows, win


def _gather_pages(page_refs):
    n_sk = 2 * NSA_KV_HEADS
    pages = []
    for r in page_refs:
        rows = r.reshape(PAGE_SIZE * n_sk, NSA_HEAD_DIM)
        pages.append(jnp.concatenate(
            [rows[pl.ds(sk, PAGE_SIZE, stride=n_sk), :].astype(BF16) for sk in range(n_sk)], axis=1))
    return jnp.concatenate(pages, axis=0)


def _cmp_sample_kernel(pt_ref, *refs, t_new, p_len):
    del pt_ref
    npg = SAMPLE_PAGES_PER_STEP
    page_refs = refs[:npg]
    q_ref, pb_ref, pbt_ref, pe_ref, phi_ref, mt_ref, oc_ref, sel_ref, tail_ref, s_ref, cv_ref = refs[npg:]
    c = pl.program_id(1)
    rows = NSA_GROUP * t_new
    n_r = s_ref.shape[2]

    @pl.when(c == 0)
    def _():
        tail_ref[...] = jnp.zeros_like(tail_ref)

    x = _gather_pages(page_refs)
    ckv = _compress_chunk(x, tail_ref[...], pb_ref, pbt_ref, pe_ref, phi_ref)
    tail_ref[...] = x[CMP_CHUNK - CMP_STRIDE:, :]
    r0 = pl.multiple_of(c * CMP_ROWS, CMP_ROWS)
    cv_ref[pl.ds(r0, CMP_ROWS), :] = ckv[:, NSA_KV_DIM:].astype(BF16)
    for kv in range(NSA_KV_HEADS):
        ck = ckv[:, kv * NSA_HEAD_DIM:(kv + 1) * NSA_HEAD_DIM].astype(BF16)
        s_ref[kv, :, pl.ds(r0, CMP_ROWS)] = _dot_nt(q_ref[0, kv], ck)

    @pl.when(c == pl.num_programs(1) - 1)
    def _():
        ri = lax.broadcasted_iota(jnp.int32, (rows, n_r), 1)
        ti = lax.broadcasted_iota(jnp.int32, (rows, n_r), 0) % t_new
        dist_i = (p_len + ti) - (CMP_STRIDE * ri + CMP_STRIDE - 1)
        mask = (dist_i >= 0) & (ri >= 1)
        maskf = mask.astype(F32)
        dist = dist_i.astype(F32)
        n_pad = mt_ref.shape[1]
        blk = lax.broadcasted_iota(jnp.int32, (rows, n_pad), 1)
        cur = (p_len + lax.broadcasted_iota(jnp.int32, (rows, n_pad), 0) % t_new) // SLC_BLOCK
        n_slc = -(-(p_len + t_new) // SLC_BLOCK)
        avail = blk <= cur
        forced = (blk == 0) | (blk == cur) | (blk == cur - 1)
        gi = lax.broadcasted_iota(jnp.int32, (rows, rows), 0) % t_new
        gj = lax.broadcasted_iota(jnp.int32, (rows, rows), 1) % t_new
        gsum = jnp.where(gi == gj, 1.0, 0.0).astype(BF16)
        for kv in range(NSA_KV_HEADS):
            s = jnp.where(mask, s_ref[kv] - _slope_col(kv, rows, t_new) * dist, NEG_INF)
            e = jnp.exp(s - jnp.max(s, axis=-1, keepdims=True)) * maskf
            p = e / jnp.maximum(jnp.sum(e, axis=-1, keepdims=True), 1e-30)
            oc_ref[0, kv] = _dot(p.astype(BF16), cv_ref[:, kv * NSA_HEAD_DIM:(kv + 1) * NSA_HEAD_DIM])
            p_sum = sum(_dot(gsum, part) for part in _split3(p))
            p_slc = sum(_dot(part, mt_ref[...]) for part in _split3(p_sum))
            score = jnp.where(avail, p_slc + jnp.where(forced, FORCE_BONUS, 0.0), -1.0)
            score = jnp.where(blk < n_slc, score, -3.0)
            sel_ref[0, kv] = _select_top(score, 1, n_pad)


def _page_spec(layer, half, d):
    npg = SAMPLE_PAGES_PER_STEP
    return pl.BlockSpec((1, 1, PAGE_SIZE, 2 * NSA_KV_HEADS, NSA_HEAD_DIM),
                        lambda i, c, pt: (layer, pt[i, c * npg + d], 0, half, 0))


def _cmp_sample(cache, layer, page_table, q16, consts, t_new):
    db, n_pages = page_table.shape
    p_len = n_pages * PAGE_SIZE
    n_chunks = p_len // CMP_CHUNK
    n_r = p_len // CMP_STRIDE
    n_slc = -(-(p_len + t_new) // SLC_BLOCK)
    n_pad = -(-n_slc // LANES) * LANES
    mt = _slc_map(n_slc, n_r, n_pad).T
    pb, pbt, pe_term, phi = consts
    rows = NSA_GROUP * t_new
    full = lambda a: pl.BlockSpec(a.shape, lambda i, c, pt: (0,) * a.ndim)
    gs = pltpu.PrefetchScalarGridSpec(
        num_scalar_prefetch=1,
        grid=(db, n_chunks),
        in_specs=[_page_spec(layer, 0, d) for d in range(SAMPLE_PAGES_PER_STEP)]
                 + [pl.BlockSpec((1, NSA_KV_HEADS, rows, NSA_HEAD_DIM), lambda i, c, pt: (i, 0, 0, 0)),
                    full(pb), full(pbt), full(pe_term), full(phi), full(mt)],
        out_specs=[pl.BlockSpec((1, NSA_KV_HEADS, rows, NSA_HEAD_DIM), lambda i, c, pt: (i, 0, 0, 0)),
                   pl.BlockSpec((1, NSA_KV_HEADS, rows, n_pad), lambda i, c, pt: (i, 0, 0, 0))],
        scratch_shapes=[pltpu.VMEM((CMP_STRIDE, 2 * NSA_KV_DIM), BF16),
                        pltpu.VMEM((NSA_KV_HEADS, rows, n_r), F32),
                        pltpu.VMEM((n_r, NSA_KV_DIM), BF16)],
    )
    return pl.pallas_call(
        functools.partial(_cmp_sample_kernel, t_new=t_new, p_len=p_len),
        grid_spec=gs,
        out_shape=[jax.ShapeDtypeStruct((db, NSA_KV_HEADS, rows, NSA_HEAD_DIM), F32),
                   jax.ShapeDtypeStruct((db, NSA_KV_HEADS, rows, n_pad), F32)],
        compiler_params=_cparams("parallel", "arbitrary"),
        name="nsa_cmp_sample",
    )(page_table, *([cache] * SAMPLE_PAGES_PER_STEP), q16, pb, pbt, pe_term, phi, mt)


def _online_update(s, maskf, vv, m_ref, l_ref, acc_ref, kv):
    m_old = m_ref[kv]
    m_new = jnp.maximum(m_old, jnp.max(s, axis=-1, keepdims=True))
    a = jnp.exp(m_old - m_new)
    e = jnp.exp(s - m_new) * maskf
    l_ref[kv] = a * l_ref[kv] + jnp.sum(e, axis=-1, keepdims=True)
    acc_ref[kv] = a * acc_ref[kv] + _dot(e.astype(BF16), vv)
    m_ref[kv] = m_new


def _slc_sample_kernel(pt_ref, *refs, t_new, p_len):
    del pt_ref
    npg = SAMPLE_PAGES_PER_STEP
    page_refs = refs[:npg]
    q_ref, sel_ref, kn_ref, vn_ref, o_ref, m_ref, l_ref, acc_ref = refs[npg:]
    c = pl.program_id(1)
    rows = NSA_GROUP * t_new
    n_pad = sel_ref.shape[3]

    @pl.when(c == 0)
    def _():
        m_ref[...] = jnp.full_like(m_ref, NEG_INF)
        l_ref[...] = jnp.zeros_like(l_ref)
        acc_ref[...] = jnp.zeros_like(acc_ref)

    x = _gather_pages(page_refs)
    eb = lax.broadcasted_iota(jnp.int32, (n_pad, CMP_CHUNK), 0)
    ec = (c * CMP_CHUNK + lax.broadcasted_iota(jnp.int32, (n_pad, CMP_CHUNK), 1)) // SLC_BLOCK
    expand = jnp.where(eb == ec, 1.0, 0.0).astype(BF16)
    ti = lax.broadcasted_iota(jnp.int32, (rows, CMP_CHUNK), 0) % t_new
    kpos = c * CMP_CHUNK + lax.broadcasted_iota(jnp.int32, (rows, CMP_CHUNK), 1)
    dist = ((p_len + ti) - kpos).astype(F32)
    for kv in range(NSA_KV_HEADS):
        mask = _dot(sel_ref[0, kv], expand) > 0.5
        s = _dot_nt(q_ref[0, kv], x[:, kv * NSA_HEAD_DIM:(kv + 1) * NSA_HEAD_DIM])
        s = jnp.where(mask, s - _slope_col(kv, rows, t_new) * dist, NEG_INF)
        _online_update(s, mask.astype(F32), x[:, NSA_KV_DIM + kv * NSA_HEAD_DIM:NSA_KV_DIM + (kv + 1) * NSA_HEAD_DIM],
                       m_ref, l_ref, acc_ref, kv)

    @pl.when(c == pl.num_programs(1) - 1)
    def _():
        n_new = kn_ref.shape[1]
        ti2 = lax.broadcasted_iota(jnp.int32, (rows, n_new), 0) % t_new
        ci = lax.broadcasted_iota(jnp.int32, (rows, n_new), 1)
        mask2 = (ci <= ti2) & (ci < t_new)
        dist2 = (ti2 - ci).astype(F32)
        for kv in range(NSA_KV_HEADS):
            s = _dot_nt(q_ref[0, kv], kn_ref[0, :, kv * NSA_HEAD_DIM:(kv + 1) * NSA_HEAD_DIM])
            s = jnp.where(mask2, s - _slope_col(kv, rows, t_new) * dist2, NEG_INF)
            _online_update(s, mask2.astype(F32), vn_ref[0, :, kv * NSA_HEAD_DIM:(kv + 1) * NSA_HEAD_DIM],
                           m_ref, l_ref, acc_ref, kv)
            o_ref[0, kv] = acc_ref[kv] / jnp.maximum(l_ref[kv], 1e-30)


def _slc_sample(cache, layer, page_table, q16, sel16, k_new, v_new, t_new):
    db, n_pages = page_table.shape
    p_len = n_pages * PAGE_SIZE
    assert p_len % SLC_BLOCK == 0
    n_chunks = p_len // CMP_CHUNK
    rows = NSA_GROUP * t_new
    n_pad = sel16.shape[3]
    per_b = lambda a: pl.BlockSpec((1,) + a.shape[1:], lambda i, c, pt: (i,) + (0,) * (a.ndim - 1))
    gs = pltpu.PrefetchScalarGridSpec(
        num_scalar_prefetch=1,
        grid=(db, n_chunks),
        in_specs=[_page_spec(layer, 1, d) for d in range(SAMPLE_PAGES_PER_STEP)]
                 + [per_b(q16), per_b(sel16), per_b(k_new), per_b(v_new)],
        out_specs=pl.BlockSpec((1, NSA_KV_HEADS, rows, NSA_HEAD_DIM), lambda i, c, pt: (i, 0, 0, 0)),
        scratch_shapes=[pltpu.VMEM((NSA_KV_HEADS, rows, 1), F32),
                        pltpu.VMEM((NSA_KV_HEADS, rows, 1), F32),
                        pltpu.VMEM((NSA_KV_HEADS, rows, NSA_HEAD_DIM), F32)],
    )
    return pl.pallas_call(
        functools.partial(_slc_sample_kernel, t_new=t_new, p_len=p_len),
        grid_spec=gs,
        out_shape=jax.ShapeDtypeStruct((db, NSA_KV_HEADS, rows, NSA_HEAD_DIM), F32),
        compiler_params=_cparams("parallel", "arbitrary"),
        name="nsa_slc_sample",
    )(page_table, *([cache] * SAMPLE_PAGES_PER_STEP), q16, sel16, k_new, v_new)


def _win_sample_kernel(q_ref, wk_ref, wv_ref, kn_ref, vn_ref, o_ref, *, t_new):
    rows = NSA_GROUP * t_new
    wb = wk_ref.shape[1]
    n_new = kn_ref.shape[1]
    t1 = lax.broadcasted_iota(jnp.int32, (rows, wb), 0) % t_new
    c1 = lax.broadcasted_iota(jnp.int32, (rows, wb), 1)
    d1 = wb + t1 - c1
    mask1 = (d1 >= 0) & (d1 <= WINDOW)
    t2 = lax.broadcasted_iota(jnp.int32, (rows, n_new), 0) % t_new
    c2 = lax.broadcasted_iota(jnp.int32, (rows, n_new), 1)
    d2 = t2 - c2
    mask2 = (d2 >= 0) & (d2 <= WINDOW) & (c2 < t_new)
    for kv in range(NSA_KV_HEADS):
        lanes = slice(kv * NSA_HEAD_DIM, (kv + 1) * NSA_HEAD_DIM)
        slope = _slope_col(kv, rows, t_new)
        s1 = jnp.where(mask1, _dot_nt(q_ref[0, kv], wk_ref[0, :, lanes]) - slope * d1.astype(F32), NEG_INF)
        s2 = jnp.where(mask2, _dot_nt(q_ref[0, kv], kn_ref[0, :, lanes]) - slope * d2.astype(F32), NEG_INF)
        m = jnp.maximum(jnp.max(s1, axis=-1, keepdims=True), jnp.max(s2, axis=-1, keepdims=True))
        e1 = jnp.exp(s1 - m) * mask1.astype(F32)
        e2 = jnp.exp(s2 - m) * mask2.astype(F32)
        den = jnp.maximum(jnp.sum(e1, axis=-1, keepdims=True) + jnp.sum(e2, axis=-1, keepdims=True), 1e-30)
        o_ref[0, kv] = (_dot(e1.astype(BF16), wv_ref[0, :, lanes]) + _dot(e2.astype(BF16), vn_ref[0, :, lanes])) / den


def _win_sample(q16, wk, wv, k_new, v_new, t_new):
    db = q16.shape[0]
    rows = NSA_GROUP * t_new
    per_b = lambda a: pl.BlockSpec((1,) + a.shape[1:], lambda i: (i,) + (0,) * (a.ndim - 1))
    return pl.pallas_call(
        functools.partial(_win_sample_kernel, t_new=t_new),
        grid=(db,),
        in_specs=[per_b(q16), per_b(wk), per_b(wv), per_b(k_new), per_b(v_new)],
        out_specs=pl.BlockSpec((1, NSA_KV_HEADS, rows, NSA_HEAD_DIM), lambda i: (i, 0, 0, 0)),
        out_shape=jax.ShapeDtypeStruct((db, NSA_KV_HEADS, rows, NSA_HEAD_DIM), F32),
        compiler_params=_cparams("parallel"),
        name="nsa_win_sample",
    )(q16, wk, wv, k_new, v_new)


def _pad_rows(a, n):
    return jnp.pad(a, ((0, 0), (0, n - a.shape[1]), (0, 0)))


def _nsa_sample(x, cache, layer, win_l, page_table, weights):
    db, t, _ = x.shape
    w_main, w_gate, consts, w_out = weights
    xb = x.reshape(db * t, D_MODEL).astype(BF16)
    h = _mm(xb, w_main)
    hg = _mm(xb, w_gate)
    h3 = h.reshape(db, t, NSA_MAIN_DIM)
    q16 = (h3[..., :NSA_Q_DIM] * NSA_SCALE).astype(BF16).reshape(db, t, NSA_KV_HEADS, NSA_GROUP, NSA_HEAD_DIM)
    q16 = q16.transpose(0, 2, 3, 1, 4).reshape(db, NSA_KV_HEADS, NSA_GROUP * t, NSA_HEAD_DIM)
    kvn = h3[..., NSA_Q_DIM:].reshape(db, t, 6, NSA_KV_DIM)
    new = lambda slot: _pad_rows(kvn[:, :, slot].astype(BF16), LANES)
    cache = cache.reshape(cache.shape[:3] + (4 * NSA_KV_HEADS, NSA_HEAD_DIM))
    o_c, sel16 = _cmp_sample(cache, layer, page_table, q16, consts, t)
    o_s = _slc_sample(cache, layer, page_table, q16, sel16.astype(BF16), new(2), new(3), t)
    wb = win_l.shape[1]
    wkv = win_l.astype(BF16).reshape(db, wb, 2, NSA_KV_DIM)
    o_w = _win_sample(q16, wkv[:, :, 0], wkv[:, :, 1], new(4), new(5), t)
    unrow = lambda o: o.reshape(db, NSA_KV_HEADS, NSA_GROUP, t, NSA_HEAD_DIM).transpose(0, 3, 1, 2, 4).reshape(
        db * t, NSA_Q_DIM)
    o = _merge(unrow(o_c), unrow(o_s), unrow(o_w), hg)
    y = _mm(o, w_out).reshape(db, t, D_MODEL)
    kv = h3[..., NSA_Q_DIM:].reshape(db, t, 6, NSA_KV_HEADS, NSA_HEAD_DIM)
    wbuf = jnp.concatenate([win_l, kv[:, :, 4:]], axis=1)
    return y, kv[:, :, :4], wbuf[:, wbuf.shape[1] - wb:]


S5_SEQS = SUBLANES
S5_CB_GROUPS = 16
S5_CB_STATES = S5_CB_GROUPS * S5_STATE
S5_CB_CH = S5_CB_GROUPS * S5_GROUP_CH
S5_N_CB = S5_GROUPS // S5_CB_GROUPS
S5_MAX_STEPS = 128


def _s5_discretize(a_re, a_im, log_dt, b_re, b_im):
    dt = jnp.exp(log_dt.astype(F32))[:, None]
    mag = jnp.exp(a_re * dt)
    ab_re = mag * jnp.cos(a_im * dt)
    ab_im = mag * jnp.sin(a_im * dt)
    den = a_re * a_re + a_im * a_im
    nr = ab_re - 1.0
    f_re = (nr * a_re + ab_im * a_im) / den
    f_im = (ab_im * a_re - nr * a_im) / den
    bb_re = f_re[..., None] * b_re - f_im[..., None] * b_im
    bb_im = f_re[..., None] * b_im + f_im[..., None] * b_re
    return ab_re, ab_im, bb_re, bb_im


def _s5_weights(w_in, a_re, a_im, log_dt, b_re, b_im, c_re, c_im, d, w_glu):
    ab_re, ab_im, bb_re, bb_im = _s5_discretize(a_re, a_im, log_dt, b_re, b_im)
    eye = jnp.eye(S5_CB_GROUPS, dtype=F32)

    def bd_in(bb):
        x = bb.reshape(S5_N_CB, S5_CB_GROUPS, S5_STATE, S5_GROUP_CH)
        return jnp.einsum('ngpc,gh->ngchp', x, eye).reshape(S5_N_CB, S5_CB_CH, S5_CB_STATES)

    def bd_out(cc):
        x = cc.reshape(S5_N_CB, S5_CB_GROUPS, S5_GROUP_CH, S5_STATE)
        return jnp.einsum('ngcp,gh->ngphc', x, eye).reshape(S5_N_CB, S5_CB_STATES, S5_CB_CH)

    bmat = jnp.concatenate([bd_in(bb_re), bd_in(bb_im)], axis=2).astype(BF16)
    cmat = jnp.concatenate([bd_out(c_re.astype(F32)), -bd_out(c_im.astype(F32))], axis=1).astype(BF16)
    rep = lambda a: jnp.broadcast_to(a.reshape(S5_N_CB, 1, S5_CB_STATES), (S5_N_CB, S5_SEQS, S5_CB_STATES))
    return dict(w_in=w_in.astype(BF16), w_glu=w_glu.astype(BF16), bmat=bmat, cmat=cmat,
                ar=rep(ab_re), ai=rep(ab_im), ab_re=ab_re, ab_im=ab_im, d=d.reshape(1, D_MODEL).astype(F32))


def _s5_scan_kernel(u_ref, b_ref, c_ref, ar_ref, ai_ref, d_ref, h0r_ref, h0i_ref, *rest, steps, with_y):
    if with_y:
        y_ref, hr_ref, hi_ref, bu_ref = rest
    else:
        hr_ref, hi_ref, bu_ref = rest
    ns = S5_CB_STATES

    @pl.when(pl.program_id(1) == 0)
    def _():
        hr_ref[...] = h0r_ref[...]
        hi_ref[...] = h0i_ref[...]

    u = u_ref[...]
    bu_ref[...] = _dot(u.astype(BF16), b_ref[0])
    ar = ar_ref[0]
    ai = ai_ref[0]

    def step(s, carry):
        hr, hi = carry
        r0 = pl.multiple_of(s * S5_SEQS, S5_SEQS)
        br = bu_ref[pl.ds(r0, S5_SEQS), :ns]
        bi = bu_ref[pl.ds(r0, S5_SEQS), ns:]
        nr = ar * hr - ai * hi + br
        ni = ar * hi + ai * hr + bi
        if with_y:
            bu_ref[pl.ds(r0, S5_SEQS), :ns] = nr
            bu_ref[pl.ds(r0, S5_SEQS), ns:] = ni
        return nr, ni

    hr, hi = lax.fori_loop(0, steps, step, (hr_ref[0], hi_ref[0]))
    hr_ref[0] = hr
    hi_ref[0] = hi
    if with_y:
        y_ref[...] = _dot(bu_ref[...].astype(BF16), c_ref[0]) + d_ref[...] * u


def _s5_scan(u_rows, w, h0r, h0i, with_y=True):
    n_rows = u_rows.shape[0]
    n_steps = n_rows // S5_SEQS
    steps = min(S5_MAX_STEPS, n_steps)
    assert n_steps % steps == 0
    tr = steps * S5_SEQS
    cb_spec = lambda a: pl.BlockSpec((1,) + a.shape[1:], lambda cb, t: (cb,) + (0,) * (a.ndim - 1))
    y_spec = [pl.BlockSpec((tr, S5_CB_CH), lambda cb, t: (t, cb))] if with_y else []
    y_shape = [jax.ShapeDtypeStruct((n_rows, D_MODEL), F32)] if with_y else []
    return pl.pallas_call(
        functools.partial(_s5_scan_kernel, steps=steps, with_y=with_y),
        grid=(S5_N_CB, n_steps // steps),
        in_specs=[pl.BlockSpec((tr, S5_CB_CH), lambda cb, t: (t, cb)),
                  cb_spec(w['bmat']), cb_spec(w['cmat']), cb_spec(w['ar']), cb_spec(w['ai']),
                  pl.BlockSpec((1, S5_CB_CH), lambda cb, t: (0, cb)),
                  cb_spec(h0r), cb_spec(h0i)],
        out_specs=y_spec + [cb_spec(h0r), cb_spec(h0i)],
        out_shape=y_shape + [jax.ShapeDtypeStruct(h0r.shape, F32), jax.ShapeDtypeStruct(h0i.shape, F32)],
        scratch_shapes=[pltpu.VMEM((tr, 2 * S5_CB_STATES), F32)],
        compiler_params=_cparams("parallel", "arbitrary"),
        name="s5_scan",
    )(u_rows, w['bmat'], w['cmat'], w['ar'], w['ai'], w['d'], h0r, h0i)


def _state_to_blocks(h):
    return h.reshape(h.shape[0], S5_N_CB, S5_CB_STATES).transpose(1, 0, 2)


def _blocks_to_state(h):
    return h.transpose(1, 0, 2).reshape(h.shape[1], S5_GROUPS, S5_STATE)


def _cpow2(re, im, n):
    for _ in range(n):
        re, im = re * re - im * im, 2.0 * re * im
    return re, im


def _s5_prompt(x, w):
    b, t, _ = x.shape
    n_seg = S5_SEQS // b
    seg = t // n_seg
    assert n_seg * b == S5_SEQS and seg & (seg - 1) == 0
    xr = x.reshape(b, n_seg, seg, D_MODEL).transpose(2, 0, 1, 3).reshape(t * b, D_MODEL)
    u = _mm(xr.astype(BF16), w['w_in'])
    zero = jnp.zeros((S5_N_CB, S5_SEQS, S5_CB_STATES), F32)
    er, ei = _s5_scan(u, w, zero, zero, with_y=False)
    er = _blocks_to_state(er).reshape(b, n_seg, S5_GROUPS, S5_STATE)
    ei = _blocks_to_state(ei).reshape(b, n_seg, S5_GROUPS, S5_STATE)
    pr, pi = _cpow2(w['ab_re'], w['ab_im'], int(math.log2(seg)))
    sr = [jnp.zeros((b, S5_GROUPS, S5_STATE), F32)]
    si = [jnp.zeros((b, S5_GROUPS, S5_STATE), F32)]
    for k in range(n_seg - 1):
        sr.append(er[:, k] + pr * sr[k] - pi * si[k])
        si.append(ei[:, k] + pr * si[k] + pi * sr[k])
    h0r = _state_to_blocks(jnp.stack(sr, axis=1).reshape(S5_SEQS, S5_GROUPS, S5_STATE))
    h0i = _state_to_blocks(jnp.stack(si, axis=1).reshape(S5_SEQS, S5_GROUPS, S5_STATE))
    y, hr, hi = _s5_scan(u, w, h0r, h0i)
    hr = _blocks_to_state(hr).reshape(b, n_seg, S5_GROUPS, S5_STATE)[:, -1]
    hi = _blocks_to_state(hi).reshape(b, n_seg, S5_GROUPS, S5_STATE)[:, -1]
    z = _mm(_gelu_rows(y), w['w_glu'])
    unperm = lambda a: a.reshape(seg, b, n_seg, -1).transpose(1, 2, 0, 3).reshape(b * t, -1)
    return unperm(z), jnp.stack([hr, hi], axis=1)


def _s5_sample(x, h0, w):
    db, t, _ = x.shape
    assert db == S5_SEQS
    xr = x.transpose(1, 0, 2).reshape(t * db, D_MODEL)
    u = _mm(xr.astype(BF16), w['w_in'])
    y, hr, hi = _s5_scan(u, w, _state_to_blocks(h0[:, 0].astype(F32)), _state_to_blocks(h0[:, 1].astype(F32)))
    z = _mm(_gelu_rows(y), w['w_glu'])
    z = z.reshape(t, db, -1).transpose(1, 0, 2).reshape(db * t, -1)
    return z, jnp.stack([_blocks_to_state(hr), _blocks_to_state(hi)], axis=1).astype(h0.dtype)


def _gelu_kernel(y_ref, o_ref):
    o_ref[...] = _gelu_tanh(y_ref[...]).astype(o_ref.dtype)


def _gelu_rows(y):
    n, d = y.shape
    tm = min(256, n)
    row = pl.BlockSpec((tm, d), lambda i: (i, 0))
    return pl.pallas_call(_gelu_kernel, grid=(n // tm,), in_specs=[row], out_specs=row,
                          out_shape=jax.ShapeDtypeStruct((n, d), BF16),
                          compiler_params=_cparams("parallel"), name="gelu")(y)


ROUTER_TN = 256
N_RANKED = PEER_TOPK
_CAND_PAIRS = [(i, j) for i in range(N_RANKED) for j in range(N_RANKED) if (i + 1) * (j + 1) <= N_RANKED]


UNRANKED = 127.0


def _top_values(s, n, with_rank=False):
    vals = []
    cur = s
    rank = jnp.full(s.shape, UNRANKED, F32)
    for t in range(n):
        m = jnp.max(cur, axis=0, keepdims=True)
        vals.append(m)
        hit = cur == m
        if with_rank:
            rank = jnp.where(hit, float(t), rank)
        cur = jnp.where(hit, -jnp.inf, cur)
    return (vals, rank) if with_rank else vals


def _router_kernel(x_ref, wq_ref, keys_ref, w1z_ref, cnt_ref, rank_ref, w2_ref):
    q = _dot(x_ref[...], wq_ref[...])
    for h in range(PEER_HEADS):
        s = [_dot_nt(keys_ref[2 * h + c], q[:, (2 * h + c) * PEER_HALF:(2 * h + c + 1) * PEER_HALF].astype(BF16))
             for c in range(2)]
        top1 = _top_values(s[0], N_RANKED)
        top2, rank2 = _top_values(s[1], N_RANKED, with_rank=True)
        cand = jnp.concatenate([top1[i] + top2[j] for i, j in _CAND_PAIRS], axis=0)
        tau = _top_values(cand, PEER_TOPK)[PEER_TOPK - 1]
        m1, m2 = top1[0], top2[0]
        z = jnp.sum(jnp.where(cand >= tau, jnp.exp(cand - (m1 + m2)), 0.0), axis=0, keepdims=True)
        cnt = jnp.zeros(s[0].shape, F32)
        for j in range(N_RANKED):
            cnt = cnt + jnp.where(s[0] + top2[j] >= tau, 1.0, 0.0)
        w1z_ref[h] = jnp.exp(s[0] - m1) / z
        cnt_ref[h] = cnt
        rank_ref[h] = rank2.astype(rank_ref.dtype)
        w2_ref[h] = jnp.exp(s[1] - m2).astype(w2_ref.dtype)


def _router(x, wq, keys):
    n = x.shape[0]
    tn = min(ROUTER_TN, n)
    assert n % tn == 0
    shape = (PEER_HEADS, PEER_N_KEYS, n)
    ospec = pl.BlockSpec((PEER_HEADS, PEER_N_KEYS, tn), lambda i: (0, 0, i))
    return pl.pallas_call(
        _router_kernel, grid=(n // tn,),
        in_specs=[pl.BlockSpec((tn, D_MODEL), lambda i: (i, 0)),
                  pl.BlockSpec(wq.shape, lambda i: (0, 0)),
                  pl.BlockSpec(keys.shape, lambda i: (0, 0, 0))],
        out_specs=[ospec] * 4,
        out_shape=[jax.ShapeDtypeStruct(shape, F32), jax.ShapeDtypeStruct(shape, F32),
                   jax.ShapeDtypeStruct(shape, BF16), jax.ShapeDtypeStruct(shape, BF16)],
        compiler_params=_cparams("parallel"), name="peer_router",
    )(x, wq, keys)


EXPERT_TN = 512
EXPERT_TE = 1024
EXPERT_NC = EXPERT_TE // PEER_N_KEYS
EXPERT_NT = PEER_N_EXPERTS // EXPERT_TE


def _expert_gate(w1z_ref, cnt_ref, rank_ref, w2_ref, c):
    g = None
    for h in range(PEER_HEADS):
        w2 = w2_ref[h]
        w1 = w1z_ref[h, pl.ds(c, 1), :].astype(BF16)
        cnt = cnt_ref[h, pl.ds(c, 1), :].astype(BF16)
        term = w1 * jnp.where(rank_ref[h] < cnt, w2, jnp.zeros_like(w2))
        g = term if g is None else g + term
    return g


def _expert_kernel(xt_ref, u_ref, vt_ref, w1z_ref, cnt_ref, rank_ref, w2_ref, o_ref):
    j = pl.program_id(1)

    @pl.when(j == 0)
    def _():
        o_ref[...] = jnp.zeros_like(o_ref)

    ht = _dot(u_ref[...], xt_ref[...])
    parts = []
    for cc in range(EXPERT_NC):
        g = _expert_gate(w1z_ref, cnt_ref, rank_ref, w2_ref, j * EXPERT_NC + cc)
        parts.append(_gelu_tanh(ht[cc * PEER_N_KEYS:(cc + 1) * PEER_N_KEYS]).astype(BF16) * g)
    o_ref[...] += _dot(vt_ref[...], jnp.concatenate(parts, axis=0))


def _expert(xt, u, vt, layer, w1z, cnt, rank, w2):
    n = xt.shape[1]
    tn = min(EXPERT_TN, n)
    assert n % tn == 0
    rspec = pl.BlockSpec((PEER_HEADS, PEER_N_KEYS, tn), lambda i, j: (0, 0, i))
    return pl.pallas_call(
        _expert_kernel, grid=(n // tn, EXPERT_NT),
        in_specs=[pl.BlockSpec((D_MODEL, tn), lambda i, j: (0, i)),
                  pl.BlockSpec((None, EXPERT_TE, D_MODEL), lambda i, j: (layer, j, 0)),
                  pl.BlockSpec((None, D_MODEL, EXPERT_TE), lambda i, j: (layer, 0, j)),
                  rspec, rspec, rspec, rspec],
        out_specs=pl.BlockSpec((D_MODEL, tn), lambda i, j: (0, i)),
        out_shape=jax.ShapeDtypeStruct((D_MODEL, n), F32),
        compiler_params=_cparams("parallel", "arbitrary"), name="peer_expert",
    )(xt, u, vt, w1z, cnt, rank, w2)


def _peer_tables(peer_u, peer_v):
    return peer_u.astype(BF16), jnp.swapaxes(peer_v, 1, 2).astype(BF16)


def _peer(x, w_q, sub_keys, tables, layer):
    u, vt = tables
    n = x.shape[0]
    n_pad = -(-n // LANES) * LANES
    xb = jnp.pad(x, ((0, n_pad - n), (0, 0))).astype(BF16)
    keys = sub_keys.reshape(2 * PEER_HEADS, PEER_N_KEYS, PEER_HALF).astype(BF16)
    w1z, cnt, rank, w2 = _router(xb, w_q.astype(BF16), keys)
    out_t = _expert(xb.T, u, vt, layer, w1z, cnt, rank, w2)
    return out_t.T[:n]


def kernel(x_prompt, x_sample, cache_nsa, state_win, state_s5, page_table, nsa_w_in, nsa_cmp_pool, nsa_cmp_pe,
           nsa_cmp_phi, nsa_w_out, s5_w_in, s5_a_re, s5_a_im, s5_log_dt, s5_b_re, s5_b_im, s5_c_re, s5_c_im, s5_d,
           s5_w_glu, peer_w_q, peer_sub_keys, peer_u, peer_v, ln_g, ln_b):
    b, s, _ = x_prompt.shape
    db, t, _ = x_sample.shape
    xp = x_prompt.reshape(b * s, D_MODEL)
    xs = x_sample.reshape(db * t, D_MODEL)
    rows_p, rows_s, win_p, win_s, s5_p, s5_s = [], [], [], [], [], []
    tables = _peer_tables(peer_u, peer_v)
    for layer in range(DEPTH):
        j = layer // N_MIXERS
        if layer % N_MIXERS == 0:
            w = _nsa_weights(nsa_w_in[j], nsa_cmp_pool[j], nsa_cmp_pe[j], nsa_cmp_phi[j], nsa_w_out[j])
            mp, rp, wp = _nsa_prompt(xp.reshape(b, s, D_MODEL), w)
            ms, rs, ws = _nsa_sample(xs.reshape(db, t, D_MODEL), cache_nsa, j, state_win[j], page_table, w)
            rows_p.append(rp)
            rows_s.append(rs)
            win_p.append(wp)
            win_s.append(ws)
            xp = _ln_res(xp, mp.reshape(b * s, D_MODEL), ln_g[layer, 0], ln_b[layer, 0])
            xs = _ln_res(xs, ms.reshape(db * t, D_MODEL), ln_g[layer, 0], ln_b[layer, 0])
        else:
            w = _s5_weights(s5_w_in[j], s5_a_re[j], s5_a_im[j], s5_log_dt[j], s5_b_re[j], s5_b_im[j],
                            s5_c_re[j], s5_c_im[j], s5_d[j], s5_w_glu[j])
            zp, hp = _s5_prompt(xp.reshape(b, s, D_MODEL), w)
            zs, hs = _s5_sample(xs.reshape(db, t, D_MODEL), state_s5[j], w)
            s5_p.append(hp.astype(state_s5.dtype))
            s5_s.append(hs)
            xp = _ln_res(xp, zp, ln_g[layer, 0], ln_b[layer, 0], glu=True)
            xs = _ln_res(xs, zs, ln_g[layer, 0], ln_b[layer, 0], glu=True)
        xp = _ln_res(xp, _peer(xp, peer_w_q[layer], peer_sub_keys[layer], tables, layer),
                     ln_g[layer, 1], ln_b[layer, 1])
        xs = _ln_res(xs, _peer(xs, peer_w_q[layer], peer_sub_keys[layer], tables, layer),
                     ln_g[layer, 1], ln_b[layer, 1])
    return (xp.reshape(b, s, D_MODEL), xs.reshape(db, t, D_MODEL), jnp.stack(rows_p), jnp.stack(rows_s),
            jnp.stack(win_p), jnp.stack(win_s), jnp.stack(s5_p), jnp.stack(s5_s))
```

```python
import functools
import math

import numpy as np
import jax
import jax.numpy as jnp
from jax import lax
from jax.experimental import pallas as pl
from jax.experimental.pallas import tpu as pltpu

F32 = jnp.float32
BF16 = jnp.bfloat16

D_MODEL = 2048
DEPTH = 4
PAGE_SIZE = 128
N_MIXERS = 2
NSA_HEADS = 16
NSA_KV_HEADS = 4
NSA_GROUP = NSA_HEADS // NSA_KV_HEADS
NSA_HEAD_DIM = D_MODEL // NSA_HEADS
NSA_Q_DIM = NSA_HEADS * NSA_HEAD_DIM
NSA_KV_DIM = NSA_KV_HEADS * NSA_HEAD_DIM
NSA_MAIN_DIM = NSA_Q_DIM + 6 * NSA_KV_DIM
NSA_N_GATES = 3 * NSA_HEADS
NSA_SCALE = NSA_HEAD_DIM ** -0.5
CMP_BLOCK = 32
CMP_STRIDE = 16
SLC_BLOCK = 64
SLC_TOP_N = 16
WINDOW = 512
FORCE_BONUS = 1.0e4
S5_GROUP_CH = 16
S5_GROUPS = D_MODEL // S5_GROUP_CH
S5_STATE = 64
PEER_HEADS = 8
PEER_N_KEYS = 128
PEER_N_EXPERTS = PEER_N_KEYS ** 2
PEER_HALF = 128
PEER_TOPK = 16
LN_EPS = 1e-5
NEG_INF = -1.0e30
ALPHA = (2.0 * DEPTH) ** 0.25

LANES = 128
SUBLANES = 8
VMEM_LIMIT_BYTES = 56 * 1024 * 1024

CMP_CHUNK = 2048
CMP_ROWS = CMP_CHUNK // CMP_STRIDE
SAMPLE_PAGES_PER_STEP = CMP_CHUNK // PAGE_SIZE

_SLOPES = [2.0 ** (-8.0 * (h + 1) / NSA_HEADS) for h in range(NSA_HEADS)]


def _cparams(*sem):
    return pltpu.CompilerParams(dimension_semantics=sem, vmem_limit_bytes=VMEM_LIMIT_BYTES)


def _dot(a, b):
    return jnp.dot(a, b, preferred_element_type=F32)


def _dot_nt(a, b):
    return lax.dot_general(a, b, (((1,), (1,)), ((), ())), preferred_element_type=F32)


def _split3(x):
    hi = x.astype(BF16)
    r1 = x - hi.astype(F32)
    mid = r1.astype(BF16)
    lo = (r1 - mid.astype(F32)).astype(BF16)
    return hi, mid, lo


def _gelu_tanh(x):
    c = math.sqrt(2.0 / math.pi)
    return 0.5 * x * (1.0 + jnp.tanh(c * (x + 0.044715 * (x * x * x))))


def _mm_kernel(a_ref, b_ref, o_ref):
    o_ref[...] = _dot(a_ref[...], b_ref[...]).astype(o_ref.dtype)


def _mm(a, b, tm=512, tn=1024, out_dtype=F32):
    m, k = a.shape
    n = b.shape[1]
    tm = min(tm, m)
    tn = min(tn, n)
    assert m % tm == 0 and n % tn == 0
    return pl.pallas_call(
        _mm_kernel,
        grid=(n // tn, m // tm),
        in_specs=[pl.BlockSpec((tm, k), lambda j, i: (i, 0)),
                  pl.BlockSpec((k, tn), lambda j, i: (0, j))],
        out_specs=pl.BlockSpec((tm, tn), lambda j, i: (i, j)),
        out_shape=jax.ShapeDtypeStruct((m, n), out_dtype),
        compiler_params=_cparams("parallel", "parallel"),
        name="mm",
    )(a, b)


def _ln_kernel(x_ref, *refs, mixer, with_t):
    n_in = 2 if mixer == 'glu' else 1
    g_ref, b_ref = refs[n_in:n_in + 2]
    outs = refs[n_in + 2:]
    if mixer == 'glu':
        mix = refs[0][...] * (1.0 / (1.0 + jnp.exp(-refs[1][...])))
    elif mixer == 'cols':
        mix = refs[0][...].T
    else:
        mix = refs[0][...]
    y = ALPHA * x_ref[...] + mix
    mu = jnp.mean(y, axis=-1, keepdims=True)
    yc = y - mu
    var = jnp.mean(yc * yc, axis=-1, keepdims=True)
    out = yc * lax.rsqrt(var + LN_EPS) * g_ref[...] + b_ref[...]
    outs[0][...] = out
    if with_t:
        outs[1][...] = out.T.astype(outs[1].dtype)


def _ln_res(x, m, g, b, mixer='rows', with_t=False):
    n, d = x.shape
    tm = min(256, n)
    assert n % tm == 0
    row = pl.BlockSpec((tm, d), lambda i: (i, 0))
    col = pl.BlockSpec((d, tm), lambda i: (0, i))
    vec = pl.BlockSpec((1, d), lambda i: (0, 0))
    if mixer == 'glu':
        m_specs, m_args = [row, pl.BlockSpec((tm, d), lambda i: (i, 1))], (m, m)
    elif mixer == 'cols':
        m_specs, m_args = [col], (m,)
    else:
        m_specs, m_args = [row], (m,)
    out_specs = [row] + ([col] if with_t else [])
    out_shape = [jax.ShapeDtypeStruct((n, d), F32)] + ([jax.ShapeDtypeStruct((d, n), BF16)] if with_t else [])
    res = pl.pallas_call(
        functools.partial(_ln_kernel, mixer=mixer, with_t=with_t),
        grid=(n // tm,), in_specs=[row] + m_specs + [vec, vec], out_specs=out_specs, out_shape=out_shape,
        compiler_params=_cparams("parallel"), name="ln_res",
    )(x, *m_args, g.reshape(1, d), b.reshape(1, d))
    return res if with_t else res[0]


def _compress_consts(pool, pe, phi):
    eye_cur = jnp.eye(CMP_ROWS, dtype=F32)
    eye_prev = jnp.eye(CMP_ROWS, k=-1, dtype=F32)
    lo = pool[:, :, None, None, :CMP_STRIDE]
    hi = pool[:, :, None, None, CMP_STRIDE:]
    pb = (eye_prev[None, None, :, :, None] * lo + eye_cur[None, None, :, :, None] * hi).reshape(
        2, NSA_KV_HEADS, CMP_ROWS, CMP_CHUNK)
    first = jnp.asarray((np.arange(CMP_ROWS) == 0).astype(np.float32))[None, None, :, None]
    pbt = first * pool[:, :, None, :CMP_STRIDE]
    pe_term = jnp.einsum('skj,skjd->skd', pool, pe)
    return (pb.astype(BF16), pbt.astype(BF16), pe_term.reshape(2 * NSA_KV_HEADS, NSA_HEAD_DIM).astype(F32),
            phi.astype(BF16))


def _compress_chunk(x, tail, pb_ref, pbt_ref, pe_ref, phi_ref):
    outs = []
    for slot in range(2):
        for kv in range(NSA_KV_HEADS):
            lo = slot * NSA_KV_DIM + kv * NSA_HEAD_DIM
            xs = x[:, lo:lo + NSA_HEAD_DIM]
            pooled = _dot(pb_ref[slot, kv], xs) + _dot(pbt_ref[slot, kv], tail[:, lo:lo + NSA_HEAD_DIM])
            pooled = pooled + pe_ref[pl.ds(slot * NSA_KV_HEADS + kv, 1), :]
            outs.append(_dot(pooled.astype(BF16), phi_ref[slot, kv]))
    return jnp.concatenate(outs, axis=1)


def _slope_col(kv, rows, per):
    gi = lax.broadcasted_iota(jnp.int32, (rows, 1), 0) // per
    col = jnp.full((rows, 1), _SLOPES[kv * NSA_GROUP + NSA_GROUP - 1], F32)
    for g in range(NSA_GROUP - 1):
        col = jnp.where(gi == g, _SLOPES[kv * NSA_GROUP + g], col)
    return col


def _select_top(score, axis, n_entries):
    idx = lax.broadcasted_iota(jnp.int32, score.shape, axis).astype(F32)
    sel = jnp.zeros(score.shape, F32)
    cur = score
    for _ in range(SLC_TOP_N):
        m = jnp.max(cur, axis=axis, keepdims=True)
        first = jnp.min(jnp.where(cur == m, idx, float(n_entries)), axis=axis, keepdims=True)
        hit = idx == first
        sel = jnp.where(hit, 1.0, sel)
        cur = jnp.where(hit, -2.0, cur)
    return sel


def _compress_kernel(x_ref, pb_ref, pbt_ref, pe_ref, phi_ref, o_ref, tail_ref):
    @pl.when(pl.program_id(1) == 0)
    def _():
        tail_ref[...] = jnp.zeros_like(tail_ref)

    x = x_ref[0]
    o_ref[0] = _compress_chunk(x, tail_ref[...], pb_ref, pbt_ref, pe_ref, phi_ref).astype(o_ref.dtype)
    tail_ref[...] = x[CMP_CHUNK - CMP_STRIDE:, :]


def _compress(kv01, consts):
    b, s, w = kv01.shape
    n_chunks = s // CMP_CHUNK
    pb, pbt, pe_term, phi = consts
    full = lambda a: pl.BlockSpec(a.shape, lambda i, c: (0,) * a.ndim)
    return pl.pallas_call(
        _compress_kernel,
        grid=(b, n_chunks),
        in_specs=[pl.BlockSpec((1, CMP_CHUNK, w), lambda i, c: (i, c, 0)),
                  full(pb), full(pbt), full(pe_term), full(phi)],
        out_specs=pl.BlockSpec((1, CMP_ROWS, w), lambda i, c: (i, c, 0)),
        out_shape=jax.ShapeDtypeStruct((b, s // CMP_STRIDE, w), BF16),
        scratch_shapes=[pltpu.VMEM((CMP_STRIDE, w), BF16)],
        compiler_params=_cparams("parallel", "arbitrary"),
        name="nsa_compress",
    )(kv01, pb, pbt, pe_term, phi)


CMP_TQ = 128


def _cmp_prompt_kernel(q_ref, ckv_ref, mt_ref, oc_ref, sel_ref):
    tq = CMP_TQ
    n_r = ckv_ref.shape[1]
    n_slc = mt_ref.shape[0]
    q0 = pl.program_id(1) * tq
    qpos = q0 + lax.broadcasted_iota(jnp.int32, (tq, n_r), 0)
    r = lax.broadcasted_iota(jnp.int32, (tq, n_r), 1)
    dist_i = qpos - (CMP_STRIDE * r + CMP_STRIDE - 1)
    mask = (dist_i >= 0) & (r >= 1)
    maskf = mask.astype(F32)
    dist = dist_i.astype(F32)
    q = q_ref[0]
    ckv = ckv_ref[0]
    blk = lax.broadcasted_iota(jnp.int32, (n_slc, tq), 0)
    cur = (q0 + lax.broadcasted_iota(jnp.int32, (n_slc, tq), 1)) // SLC_BLOCK
    avail = blk <= cur
    forced = (blk == 0) | (blk == cur) | (blk == cur - 1)
    for kv in range(NSA_KV_HEADS):
        ck = ckv[:, kv * NSA_HEAD_DIM:(kv + 1) * NSA_HEAD_DIM]
        cv = ckv[:, NSA_KV_DIM + kv * NSA_HEAD_DIM:NSA_KV_DIM + (kv + 1) * NSA_HEAD_DIM]
        qs = jnp.concatenate([q[:, (kv * NSA_GROUP + g) * NSA_HEAD_DIM:(kv * NSA_GROUP + g + 1) * NSA_HEAD_DIM]
                              for g in range(NSA_GROUP)], axis=0)
        s_all = _dot_nt(qs, ck)
        p_sum = jnp.zeros((tq, n_r), F32)
        for g in range(NSA_GROUP):
            h = kv * NSA_GROUP + g
            s = jnp.where(mask, s_all[g * tq:(g + 1) * tq] - _SLOPES[h] * dist, NEG_INF)
            e = jnp.exp(s - jnp.max(s, axis=-1, keepdims=True)) * maskf
            p = e / jnp.maximum(jnp.sum(e, axis=-1, keepdims=True), 1e-30)
            oc_ref[0, :, h * NSA_HEAD_DIM:(h + 1) * NSA_HEAD_DIM] = _dot(p.astype(BF16), cv)
            p_sum = p_sum + p
        mt = mt_ref[...]
        p_slc = sum(_dot_nt(mt, part) for part in _split3(p_sum))
        score = jnp.where(avail, p_slc + jnp.where(forced, FORCE_BONUS, 0.0), -1.0)
        sel_ref[0, kv] = _select_top(score, 0, n_slc)


def _slc_map(n_slc, n_r, n_pad):
    m = np.arange(n_pad)[:, None]
    r = np.arange(n_r)[None, :]
    ratio = SLC_BLOCK // CMP_STRIDE
    return jnp.asarray(((r >= ratio * m) & (r <= ratio * m + ratio) & (m < n_slc)).astype(np.float32), BF16)


def _cmp_prompt(qs, ckv):
    b, s, _ = qs.shape
    n_r = ckv.shape[1]
    n_slc = s // SLC_BLOCK
    mt = _slc_map(n_slc, n_r, n_slc)
    return pl.pallas_call(
        _cmp_prompt_kernel,
        grid=(b, s // CMP_TQ),
        in_specs=[pl.BlockSpec((1, CMP_TQ, NSA_Q_DIM), lambda i, t: (i, t, 0)),
                  pl.BlockSpec((1, n_r, 2 * NSA_KV_DIM), lambda i, t: (i, 0, 0)),
                  pl.BlockSpec(mt.shape, lambda i, t: (0, 0))],
        out_specs=[pl.BlockSpec((1, CMP_TQ, NSA_Q_DIM), lambda i, t: (i, t, 0)),
                   pl.BlockSpec((1, NSA_KV_HEADS, n_slc, CMP_TQ), lambda i, t: (i, 0, 0, t))],
        out_shape=[jax.ShapeDtypeStruct((b, s, NSA_Q_DIM), F32),
                   jax.ShapeDtypeStruct((b, NSA_KV_HEADS, n_slc, s), F32)],
        compiler_params=_cparams("parallel", "parallel"),
        name="nsa_cmp_prompt",
    )(qs, ckv, mt)


SLC_TQ = 256
SLC_TK = 1024


def _slc_prompt_kernel(q_ref, k_ref, v_ref, sel_ref, o_ref, m_ref, l_ref, acc_ref):
    tq, tk = SLC_TQ, SLC_TK
    qi = pl.program_id(1)
    kj = pl.program_id(2)
    n_sel = sel_ref.shape[3]

    @pl.when(kj == 0)
    def _():
        m_ref[...] = jnp.full_like(m_ref, NEG_INF)
        l_ref[...] = jnp.zeros_like(l_ref)
        acc_ref[...] = jnp.zeros_like(acc_ref)

    @pl.when(kj * tk <= qi * tq + tq - 1)
    def _():
        q = q_ref[0]
        kt = k_ref[0]
        vt = v_ref[0]
        qpos = qi * tq + lax.broadcasted_iota(jnp.int32, (tq, tk), 0)
        kpos = kj * tk + lax.broadcasted_iota(jnp.int32, (tq, tk), 1)
        dist_i = qpos - kpos
        dist = dist_i.astype(F32)
        eb = lax.broadcasted_iota(jnp.int32, (n_sel, tk), 0)
        ec = (kj * tk + lax.broadcasted_iota(jnp.int32, (n_sel, tk), 1)) // SLC_BLOCK
        expand = jnp.where(eb == ec, 1.0, 0.0).astype(BF16)
        for kv in range(NSA_KV_HEADS):
            picked = _dot(sel_ref[0, kv], expand)
            mask = (picked > 0.5) & (dist_i >= 0)
            qs = jnp.concatenate([q[:, (kv * NSA_GROUP + g) * NSA_HEAD_DIM:(kv * NSA_GROUP + g + 1) * NSA_HEAD_DIM]
                                  for g in range(NSA_GROUP)], axis=0)
            s_all = _dot_nt(qs, kt[:, kv * NSA_HEAD_DIM:(kv + 1) * NSA_HEAD_DIM])
            vv = vt[:, kv * NSA_HEAD_DIM:(kv + 1) * NSA_HEAD_DIM]
            for g in range(NSA_GROUP):
                h = kv * NSA_GROUP + g
                rows = slice(g * tq, (g + 1) * tq)
                s = jnp.where(mask, s_all[rows] - _SLOPES[h] * dist, NEG_INF)
                m_old = m_ref[kv, rows]
                m_new = jnp.maximum(m_old, jnp.max(s, axis=-1, keepdims=True))
                a = jnp.exp(m_old - m_new)
                e = jnp.exp(s - m_new)
                l_ref[kv, rows] = a * l_ref[kv, rows] + jnp.sum(e, axis=-1, keepdims=True)
                acc_ref[kv, rows] = a * acc_ref[kv, rows] + _dot(e.astype(BF16), vv)
                m_ref[kv, rows] = m_new

    @pl.when(kj == pl.num_programs(2) - 1)
    def _():
        for kv in range(NSA_KV_HEADS):
            for g in range(NSA_GROUP):
                h = kv * NSA_GROUP + g
                rows = slice(g * tq, (g + 1) * tq)
                o_ref[0, :, h * NSA_HEAD_DIM:(h + 1) * NSA_HEAD_DIM] = (
                    acc_ref[kv, rows] / jnp.maximum(l_ref[kv, rows], 1e-30))


def _slc_prompt(qs, k_slc, v_slc, sel):
    b, s, _ = qs.shape
    tq, tk = min(SLC_TQ, s), min(SLC_TK, s)
    assert tq == SLC_TQ and tk == SLC_TK
    n_sel = sel.shape[3]
    last = lambda t: (t * tq + tq - 1) // tk
    return pl.pallas_call(
        _slc_prompt_kernel,
        grid=(b, s // tq, s // tk),
        in_specs=[pl.BlockSpec((1, tq, NSA_Q_DIM), lambda i, t, j: (i, t, 0)),
                  pl.BlockSpec((1, tk, NSA_KV_DIM), lambda i, t, j: (i, jnp.minimum(j, last(t)), 0)),
                  pl.BlockSpec((1, tk, NSA_KV_DIM), lambda i, t, j: (i, jnp.minimum(j, last(t)), 0)),
                  pl.BlockSpec((1, NSA_KV_HEADS, tq, n_sel), lambda i, t, j: (i, 0, t, 0))],
        out_specs=pl.BlockSpec((1, tq, NSA_Q_DIM), lambda i, t, j: (i, t, 0)),
        out_shape=jax.ShapeDtypeStruct((b, s, NSA_Q_DIM), F32),
        scratch_shapes=[pltpu.VMEM((NSA_KV_HEADS, NSA_GROUP * tq, 1), F32),
                        pltpu.VMEM((NSA_KV_HEADS, NSA_GROUP * tq, 1), F32),
                        pltpu.VMEM((NSA_KV_HEADS, NSA_GROUP * tq, NSA_HEAD_DIM), F32)],
        compiler_params=_cparams("parallel", "parallel", "arbitrary"),
        name="nsa_slc_prompt",
    )(qs, k_slc, v_slc, sel)


WIN_TQ = 256
WIN_NT = WINDOW // WIN_TQ + 1


def _win_prompt_kernel(q_ref, *refs):
    tq = WIN_TQ
    k_refs = refs[:WIN_NT]
    v_refs = refs[WIN_NT:2 * WIN_NT]
    o_ref = refs[2 * WIN_NT]
    qi = pl.program_id(1)
    nk = WIN_NT * tq
    row = lax.broadcasted_iota(jnp.int32, (tq, nk), 0)
    col = lax.broadcasted_iota(jnp.int32, (tq, nk), 1)
    dist_i = WINDOW + row - col
    kpos = (qi - (WIN_NT - 1)) * tq + col
    mask = (dist_i >= 0) & (dist_i <= WINDOW) & (kpos >= 0)
    dist = dist_i.astype(F32)
    q = q_ref[0]
    kcat = jnp.concatenate([r[0] for r in k_refs], axis=0)
    vcat = jnp.concatenate([r[0] for r in v_refs], axis=0)
    for kv in range(NSA_KV_HEADS):
        qs = jnp.concatenate([q[:, (kv * NSA_GROUP + g) * NSA_HEAD_DIM:(kv * NSA_GROUP + g + 1) * NSA_HEAD_DIM]
                              for g in range(NSA_GROUP)], axis=0)
        s_all = _dot_nt(qs, kcat[:, kv * NSA_HEAD_DIM:(kv + 1) * NSA_HEAD_DIM])
        vv = vcat[:, kv * NSA_HEAD_DIM:(kv + 1) * NSA_HEAD_DIM]
        for g in range(NSA_GROUP):
            h = kv * NSA_GROUP + g
            s = jnp.where(mask, s_all[g * tq:(g + 1) * tq] - _SLOPES[h] * dist, NEG_INF)
            e = jnp.exp(s - jnp.max(s, axis=-1, keepdims=True))
            den = jnp.maximum(jnp.sum(e, axis=-1, keepdims=True), 1e-30)
            o_ref[0, :, h * NSA_HEAD_DIM:(h + 1) * NSA_HEAD_DIM] = _dot(e.astype(BF16), vv) / den


def _win_prompt(qs, k_win, v_win):
    b, s, _ = qs.shape
    tq = WIN_TQ
    kspec = lambda d: pl.BlockSpec((1, tq, NSA_KV_DIM),
                                   lambda i, t: (i, jnp.maximum(t - (WIN_NT - 1) + d, 0), 0))
    return pl.pallas_call(
        _win_prompt_kernel,
        grid=(b, s // tq),
        in_specs=[pl.BlockSpec((1, tq, NSA_Q_DIM), lambda i, t: (i, t, 0))]
                 + [kspec(d) for d in range(WIN_NT)] + [kspec(d) for d in range(WIN_NT)],
        out_specs=pl.BlockSpec((1, tq, NSA_Q_DIM), lambda i, t: (i, t, 0)),
        out_shape=jax.ShapeDtypeStruct((b, s, NSA_Q_DIM), F32),
        compiler_params=_cparams("parallel", "parallel"),
        name="nsa_win_prompt",
    )(qs, *([k_win] * WIN_NT), *([v_win] * WIN_NT))


def _merge_kernel(oc_ref, os_ref, ow_ref, hg_ref, ex_ref, o_ref):
    gate = 1.0 / (1.0 + jnp.exp(-hg_ref[...]))
    parts = _split3(gate)[:2]
    out = None
    for br, ref in enumerate((oc_ref, os_ref, ow_ref)):
        gx = sum(_dot(p, ex_ref[br]) for p in parts)
        term = gx * ref[...]
        out = term if out is None else out + term
    o_ref[...] = out.astype(o_ref.dtype)


def _gate_expand():
    e = np.zeros((3, LANES, NSA_Q_DIM), np.float32)
    for br in range(3):
        for h in range(NSA_HEADS):
            e[br, h * 3 + br, h * NSA_HEAD_DIM:(h + 1) * NSA_HEAD_DIM] = 1.0
    return jnp.asarray(e, BF16)


def _merge(oc, os_, ow, hg):
    n = oc.shape[0]
    tm = min(256, n)
    ex = _gate_expand()
    row = pl.BlockSpec((tm, NSA_Q_DIM), lambda i: (i, 0))
    return pl.pallas_call(
        _merge_kernel, grid=(n // tm,),
        in_specs=[row, row, row, pl.BlockSpec((tm, LANES), lambda i: (i, 0)),
                  pl.BlockSpec(ex.shape, lambda i: (0, 0, 0))],
        out_specs=row,
        out_shape=jax.ShapeDtypeStruct((n, NSA_Q_DIM), BF16),
        compiler_params=_cparams("parallel"), name="nsa_merge",
    )(oc, os_, ow, hg, ex)


def _nsa_weights(w_in, pool, pe, phi, w_out):
    w_main = w_in[:, :NSA_MAIN_DIM].astype(BF16)
    w_gate = jnp.pad(w_in[:, NSA_MAIN_DIM:], ((0, 0), (0, LANES - NSA_N_GATES))).astype(BF16)
    return w_main, w_gate, _compress_consts(pool, pe, phi), w_out.astype(BF16)


def _nsa_prompt(x, weights):
    b, s, _ = x.shape
    w_main, w_gate, consts, w_out = weights
    xb = x.reshape(b * s, D_MODEL).astype(BF16)
    h = _mm(xb, w_main)
    hg = _mm(xb, w_gate)
    h3 = h.reshape(b, s, NSA_MAIN_DIM)
    qs = (h3[..., :NSA_Q_DIM] * NSA_SCALE).astype(BF16)
    kvb = h3[..., NSA_Q_DIM:].astype(BF16)
    ckv = _compress(kvb[..., :2 * NSA_KV_DIM], consts)
    o_c, sel_t = _cmp_prompt(qs, ckv)
    sel = jnp.swapaxes(sel_t, 2, 3).astype(BF16)
    o_s = _slc_prompt(qs, kvb[..., 2 * NSA_KV_DIM:3 * NSA_KV_DIM], kvb[..., 3 * NSA_KV_DIM:4 * NSA_KV_DIM], sel)
    o_w = _win_prompt(qs, kvb[..., 4 * NSA_KV_DIM:5 * NSA_KV_DIM], kvb[..., 5 * NSA_KV_DIM:])
    o = _merge(o_c.reshape(b * s, NSA_Q_DIM), o_s.reshape(b * s, NSA_Q_DIM), o_w.reshape(b * s, NSA_Q_DIM), hg)
    y = _mm(o, w_out).reshape(b, s, D_MODEL)
    kv = h3[..., NSA_Q_DIM:].reshape(b, s, 6, NSA_KV_HEADS, NSA_HEAD_DIM)
    rows = kv[:, :, :4].reshape(b, s // PAGE_SIZE, PAGE_SIZE, 4, NSA_KV_HEADS, NSA_HEAD_DIM)
    win = kv[:, s - min(WINDOW, s):, 4:]
    return y, rows, win


def _gather_pages(page_refs):
    n_sk = 2 * NSA_KV_HEADS
    pages = []
    for r in page_refs:
        rows = r.reshape(PAGE_SIZE * n_sk, NSA_HEAD_DIM)
        pages.append(jnp.concatenate(
            [rows[pl.ds(sk, PAGE_SIZE, stride=n_sk), :].astype(BF16) for sk in range(n_sk)], axis=1))
    return jnp.concatenate(pages, axis=0)


def _cmp_sample_kernel(pt_ref, *refs, t_new, p_len):
    del pt_ref
    npg = SAMPLE_PAGES_PER_STEP
    page_refs = refs[:npg]
    q_ref, pb_ref, pbt_ref, pe_ref, phi_ref, mt_ref, oc_ref, sel_ref, tail_ref, s_ref, cv_ref = refs[npg:]
    c = pl.program_id(1)
    rows = NSA_GROUP * t_new
    n_r = s_ref.shape[2]

    @pl.when(c == 0)
    def _():
        tail_ref[...] = jnp.zeros_like(tail_ref)

    x = _gather_pages(page_refs)
    ckv = _compress_chunk(x, tail_ref[...], pb_ref, pbt_ref, pe_ref, phi_ref)
    tail_ref[...] = x[CMP_CHUNK - CMP_STRIDE:, :]
    r0 = pl.multiple_of(c * CMP_ROWS, CMP_ROWS)
    cv_ref[pl.ds(r0, CMP_ROWS), :] = ckv[:, NSA_KV_DIM:].astype(BF16)
    for kv in range(NSA_KV_HEADS):
        ck = ckv[:, kv * NSA_HEAD_DIM:(kv + 1) * NSA_HEAD_DIM].astype(BF16)
        s_ref[kv, :, pl.ds(r0, CMP_ROWS)] = _dot_nt(q_ref[0, kv], ck)

    @pl.when(c == pl.num_programs(1) - 1)
    def _():
        ri = lax.broadcasted_iota(jnp.int32, (rows, n_r), 1)
        ti = lax.broadcasted_iota(jnp.int32, (rows, n_r), 0) % t_new
        dist_i = (p_len + ti) - (CMP_STRIDE * ri + CMP_STRIDE - 1)
        mask = (dist_i >= 0) & (ri >= 1)
        maskf = mask.astype(F32)
        dist = dist_i.astype(F32)
        n_pad = mt_ref.shape[1]
        blk = lax.broadcasted_iota(jnp.int32, (rows, n_pad), 1)
        cur = (p_len + lax.broadcasted_iota(jnp.int32, (rows, n_pad), 0) % t_new) // SLC_BLOCK
        n_slc = -(-(p_len + t_new) // SLC_BLOCK)
        avail = blk <= cur
        forced = (blk == 0) | (blk == cur) | (blk == cur - 1)
        gi = lax.broadcasted_iota(jnp.int32, (rows, rows), 0) % t_new
        gj = lax.broadcasted_iota(jnp.int32, (rows, rows), 1) % t_new
        gsum = jnp.where(gi == gj, 1.0, 0.0).astype(BF16)
        for kv in range(NSA_KV_HEADS):
            s = jnp.where(mask, s_ref[kv] - _slope_col(kv, rows, t_new) * dist, NEG_INF)
            e = jnp.exp(s - jnp.max(s, axis=-1, keepdims=True)) * maskf
            p = e / jnp.maximum(jnp.sum(e, axis=-1, keepdims=True), 1e-30)
            oc_ref[0, kv] = _dot(p.astype(BF16), cv_ref[:, kv * NSA_HEAD_DIM:(kv + 1) * NSA_HEAD_DIM])
            p_sum = sum(_dot(gsum, part) for part in _split3(p))
            p_slc = sum(_dot(part, mt_ref[...]) for part in _split3(p_sum))
            score = jnp.where(avail, p_slc + jnp.where(forced, FORCE_BONUS, 0.0), -1.0)
            score = jnp.where(blk < n_slc, score, -3.0)
            sel_ref[0, kv] = _select_top(score, 1, n_pad)


def _page_spec(layer, half, d):
    npg = SAMPLE_PAGES_PER_STEP
    return pl.BlockSpec((1, 1, PAGE_SIZE, 2 * NSA_KV_HEADS, NSA_HEAD_DIM),
                        lambda i, c, pt: (layer, pt[i, c * npg + d], 0, half, 0))


def _cmp_sample(cache, layer, page_table, q16, consts, t_new):
    db, n_pages = page_table.shape
    p_len = n_pages * PAGE_SIZE
    n_chunks = p_len // CMP_CHUNK
    n_r = p_len // CMP_STRIDE
    n_slc = -(-(p_len + t_new) // SLC_BLOCK)
    n_pad = -(-n_slc // LANES) * LANES
    mt = _slc_map(n_slc, n_r, n_pad).T
    pb, pbt, pe_term, phi = consts
    rows = NSA_GROUP * t_new
    full = lambda a: pl.BlockSpec(a.shape, lambda i, c, pt: (0,) * a.ndim)
    gs = pltpu.PrefetchScalarGridSpec(
        num_scalar_prefetch=1,
        grid=(db, n_chunks),
        in_specs=[_page_spec(layer, 0, d) for d in range(SAMPLE_PAGES_PER_STEP)]
                 + [pl.BlockSpec((1, NSA_KV_HEADS, rows, NSA_HEAD_DIM), lambda i, c, pt: (i, 0, 0, 0)),
                    full(pb), full(pbt), full(pe_term), full(phi), full(mt)],
        out_specs=[pl.BlockSpec((1, NSA_KV_HEADS, rows, NSA_HEAD_DIM), lambda i, c, pt: (i, 0, 0, 0)),
                   pl.BlockSpec((1, NSA_KV_HEADS, rows, n_pad), lambda i, c, pt: (i, 0, 0, 0))],
        scratch_shapes=[pltpu.VMEM((CMP_STRIDE, 2 * NSA_KV_DIM), BF16),
                        pltpu.VMEM((NSA_KV_HEADS, rows, n_r), F32),
                        pltpu.VMEM((n_r, NSA_KV_DIM), BF16)],
    )
    return pl.pallas_call(
        functools.partial(_cmp_sample_kernel, t_new=t_new, p_len=p_len),
        grid_spec=gs,
        out_shape=[jax.ShapeDtypeStruct((db, NSA_KV_HEADS, rows, NSA_HEAD_DIM), F32),
                   jax.ShapeDtypeStruct((db, NSA_KV_HEADS, rows, n_pad), F32)],
        compiler_params=_cparams("parallel", "arbitrary"),
        name="nsa_cmp_sample",
    )(page_table, *([cache] * SAMPLE_PAGES_PER_STEP), q16, pb, pbt, pe_term, phi, mt)


def _online_update(s, maskf, vv, m_ref, l_ref, acc_ref, kv):
    m_old = m_ref[kv]
    m_new = jnp.maximum(m_old, jnp.max(s, axis=-1, keepdims=True))
    a = jnp.exp(m_old - m_new)
    e = jnp.exp(s - m_new) * maskf
    l_ref[kv] = a * l_ref[kv] + jnp.sum(e, axis=-1, keepdims=True)
    acc_ref[kv] = a * acc_ref[kv] + _dot(e.astype(BF16), vv)
    m_ref[kv] = m_new


def _slc_sample_kernel(pt_ref, *refs, t_new, p_len):
    del pt_ref
    npg = SAMPLE_PAGES_PER_STEP
    page_refs = refs[:npg]
    q_ref, sel_ref, kn_ref, vn_ref, o_ref, m_ref, l_ref, acc_ref = refs[npg:]
    c = pl.program_id(1)
    rows = NSA_GROUP * t_new
    n_pad = sel_ref.shape[3]

    @pl.when(c == 0)
    def _():
        m_ref[...] = jnp.full_like(m_ref, NEG_INF)
        l_ref[...] = jnp.zeros_like(l_ref)
        acc_ref[...] = jnp.zeros_like(acc_ref)

    x = _gather_pages(page_refs)
    eb = lax.broadcasted_iota(jnp.int32, (n_pad, CMP_CHUNK), 0)
    ec = (c * CMP_CHUNK + lax.broadcasted_iota(jnp.int32, (n_pad, CMP_CHUNK), 1)) // SLC_BLOCK
    expand = jnp.where(eb == ec, 1.0, 0.0).astype(BF16)
    ti = lax.broadcasted_iota(jnp.int32, (rows, CMP_CHUNK), 0) % t_new
    kpos = c * CMP_CHUNK + lax.broadcasted_iota(jnp.int32, (rows, CMP_CHUNK), 1)
    dist = ((p_len + ti) - kpos).astype(F32)
    for kv in range(NSA_KV_HEADS):
        mask = _dot(sel_ref[0, kv], expand) > 0.5
        s = _dot_nt(q_ref[0, kv], x[:, kv * NSA_HEAD_DIM:(kv + 1) * NSA_HEAD_DIM])
        s = jnp.where(mask, s - _slope_col(kv, rows, t_new) * dist, NEG_INF)
        _online_update(s, mask.astype(F32), x[:, NSA_KV_DIM + kv * NSA_HEAD_DIM:NSA_KV_DIM + (kv + 1) * NSA_HEAD_DIM],
                       m_ref, l_ref, acc_ref, kv)

    @pl.when(c == pl.num_programs(1) - 1)
    def _():
        n_new = kn_ref.shape[1]
        ti2 = lax.broadcasted_iota(jnp.int32, (rows, n_new), 0) % t_new
        ci = lax.broadcasted_iota(jnp.int32, (rows, n_new), 1)
        mask2 = (ci <= ti2) & (ci < t_new)
        dist2 = (ti2 - ci).astype(F32)
        for kv in range(NSA_KV_HEADS):
            s = _dot_nt(q_ref[0, kv], kn_ref[0, :, kv * NSA_HEAD_DIM:(kv + 1) * NSA_HEAD_DIM])
            s = jnp.where(mask2, s - _slope_col(kv, rows, t_new) * dist2, NEG_INF)
            _online_update(s, mask2.astype(F32), vn_ref[0, :, kv * NSA_HEAD_DIM:(kv + 1) * NSA_HEAD_DIM],
                           m_ref, l_ref, acc_ref, kv)
            o_ref[0, kv] = acc_ref[kv] / jnp.maximum(l_ref[kv], 1e-30)


def _slc_sample(cache, layer, page_table, q16, sel16, k_new, v_new, t_new):
    db, n_pages = page_table.shape
    p_len = n_pages * PAGE_SIZE
    assert p_len % SLC_BLOCK == 0
    n_chunks = p_len // CMP_CHUNK
    rows = NSA_GROUP * t_new
    n_pad = sel16.shape[3]
    per_b = lambda a: pl.BlockSpec((1,) + a.shape[1:], lambda i, c, pt: (i,) + (0,) * (a.ndim - 1))
    gs = pltpu.PrefetchScalarGridSpec(
        num_scalar_prefetch=1,
        grid=(db, n_chunks),
        in_specs=[_page_spec(layer, 1, d) for d in range(SAMPLE_PAGES_PER_STEP)]
                 + [per_b(q16), per_b(sel16), per_b(k_new), per_b(v_new)],
        out_specs=pl.BlockSpec((1, NSA_KV_HEADS, rows, NSA_HEAD_DIM), lambda i, c, pt: (i, 0, 0, 0)),
        scratch_shapes=[pltpu.VMEM((NSA_KV_HEADS, rows, 1), F32),
                        pltpu.VMEM((NSA_KV_HEADS, rows, 1), F32),
                        pltpu.VMEM((NSA_KV_HEADS, rows, NSA_HEAD_DIM), F32)],
    )
    return pl.pallas_call(
        functools.partial(_slc_sample_kernel, t_new=t_new, p_len=p_len),
        grid_spec=gs,
        out_shape=jax.ShapeDtypeStruct((db, NSA_KV_HEADS, rows, NSA_HEAD_DIM), F32),
        compiler_params=_cparams("parallel", "arbitrary"),
        name="nsa_slc_sample",
    )(page_table, *([cache] * SAMPLE_PAGES_PER_STEP), q16, sel16, k_new, v_new)


def _win_sample_kernel(q_ref, wk_ref, wv_ref, kn_ref, vn_ref, o_ref, *, t_new):
    rows = NSA_GROUP * t_new
    wb = wk_ref.shape[1]
    n_new = kn_ref.shape[1]
    t1 = lax.broadcasted_iota(jnp.int32, (rows, wb), 0) % t_new
    c1 = lax.broadcasted_iota(jnp.int32, (rows, wb), 1)
    d1 = wb + t1 - c1
    mask1 = (d1 >= 0) & (d1 <= WINDOW)
    t2 = lax.broadcasted_iota(jnp.int32, (rows, n_new), 0) % t_new
    c2 = lax.broadcasted_iota(jnp.int32, (rows, n_new), 1)
    d2 = t2 - c2
    mask2 = (d2 >= 0) & (d2 <= WINDOW) & (c2 < t_new)
    for kv in range(NSA_KV_HEADS):
        lanes = slice(kv * NSA_HEAD_DIM, (kv + 1) * NSA_HEAD_DIM)
        slope = _slope_col(kv, rows, t_new)
        s1 = jnp.where(mask1, _dot_nt(q_ref[0, kv], wk_ref[0, :, lanes]) - slope * d1.astype(F32), NEG_INF)
        s2 = jnp.where(mask2, _dot_nt(q_ref[0, kv], kn_ref[0, :, lanes]) - slope * d2.astype(F32), NEG_INF)
        m = jnp.maximum(jnp.max(s1, axis=-1, keepdims=True), jnp.max(s2, axis=-1, keepdims=True))
        e1 = jnp.exp(s1 - m) * mask1.astype(F32)
        e2 = jnp.exp(s2 - m) * mask2.astype(F32)
        den = jnp.maximum(jnp.sum(e1, axis=-1, keepdims=True) + jnp.sum(e2, axis=-1, keepdims=True), 1e-30)
        o_ref[0, kv] = (_dot(e1.astype(BF16), wv_ref[0, :, lanes]) + _dot(e2.astype(BF16), vn_ref[0, :, lanes])) / den


def _win_sample(q16, wk, wv, k_new, v_new, t_new):
    db = q16.shape[0]
    rows = NSA_GROUP * t_new
    per_b = lambda a: pl.BlockSpec((1,) + a.shape[1:], lambda i: (i,) + (0,) * (a.ndim - 1))
    return pl.pallas_call(
        functools.partial(_win_sample_kernel, t_new=t_new),
        grid=(db,),
        in_specs=[per_b(q16), per_b(wk), per_b(wv), per_b(k_new), per_b(v_new)],
        out_specs=pl.BlockSpec((1, NSA_KV_HEADS, rows, NSA_HEAD_DIM), lambda i: (i, 0, 0, 0)),
        out_shape=jax.ShapeDtypeStruct((db, NSA_KV_HEADS, rows, NSA_HEAD_DIM), F32),
        compiler_params=_cparams("parallel"),
        name="nsa_win_sample",
    )(q16, wk, wv, k_new, v_new)


def _pad_rows(a, n):
    return jnp.pad(a, ((0, 0), (0, n - a.shape[1]), (0, 0)))


def _nsa_sample(x, cache, layer, win_l, page_table, weights):
    db, t, _ = x.shape
    w_main, w_gate, consts, w_out = weights
    xb = x.reshape(db * t, D_MODEL).astype(BF16)
    h = _mm(xb, w_main)
    hg = _mm(xb, w_gate)
    h3 = h.reshape(db, t, NSA_MAIN_DIM)
    q16 = (h3[..., :NSA_Q_DIM] * NSA_SCALE).astype(BF16).reshape(db, t, NSA_KV_HEADS, NSA_GROUP, NSA_HEAD_DIM)
    q16 = q16.transpose(0, 2, 3, 1, 4).reshape(db, NSA_KV_HEADS, NSA_GROUP * t, NSA_HEAD_DIM)
    kvn = h3[..., NSA_Q_DIM:].reshape(db, t, 6, NSA_KV_DIM)
    new = lambda slot: _pad_rows(kvn[:, :, slot].astype(BF16), LANES)
    cache = cache.reshape(cache.shape[:3] + (4 * NSA_KV_HEADS, NSA_HEAD_DIM))
    o_c, sel16 = _cmp_sample(cache, layer, page_table, q16, consts, t)
    o_s = _slc_sample(cache, layer, page_table, q16, sel16.astype(BF16), new(2), new(3), t)
    wb = win_l.shape[1]
    wkv = win_l.astype(BF16).reshape(db, wb, 2, NSA_KV_DIM)
    o_w = _win_sample(q16, wkv[:, :, 0], wkv[:, :, 1], new(4), new(5), t)
    unrow = lambda o: o.reshape(db, NSA_KV_HEADS, NSA_GROUP, t, NSA_HEAD_DIM).transpose(0, 3, 1, 2, 4).reshape(
        db * t, NSA_Q_DIM)
    o = _merge(unrow(o_c), unrow(o_s), unrow(o_w), hg)
    y = _mm(o, w_out).reshape(db, t, D_MODEL)
    kv = h3[..., NSA_Q_DIM:].reshape(db, t, 6, NSA_KV_HEADS, NSA_HEAD_DIM)
    wbuf = jnp.concatenate([win_l, kv[:, :, 4:]], axis=1)
    return y, kv[:, :, :4], wbuf[:, wbuf.shape[1] - wb:]


S5_SEQS = SUBLANES
S5_CB_GROUPS = 16
S5_CB_STATES = S5_CB_GROUPS * S5_STATE
S5_CB_CH = S5_CB_GROUPS * S5_GROUP_CH
S5_N_CB = S5_GROUPS // S5_CB_GROUPS
S5_MAX_STEPS = 128


def _s5_discretize(a_re, a_im, log_dt, b_re, b_im):
    dt = jnp.exp(log_dt.astype(F32))[:, None]
    mag = jnp.exp(a_re * dt)
    ab_re = mag * jnp.cos(a_im * dt)
    ab_im = mag * jnp.sin(a_im * dt)
    den = a_re * a_re + a_im * a_im
    nr = ab_re - 1.0
    f_re = (nr * a_re + ab_im * a_im) / den
    f_im = (ab_im * a_re - nr * a_im) / den
    bb_re = f_re[..., None] * b_re - f_im[..., None] * b_im
    bb_im = f_re[..., None] * b_im + f_im[..., None] * b_re
    return ab_re, ab_im, bb_re, bb_im


def _s5_weights(w_in, a_re, a_im, log_dt, b_re, b_im, c_re, c_im, d, w_glu):
    ab_re, ab_im, bb_re, bb_im = _s5_discretize(a_re, a_im, log_dt, b_re, b_im)
    eye = jnp.eye(S5_CB_GROUPS, dtype=F32)

    def bd_in(bb):
        x = bb.reshape(S5_N_CB, S5_CB_GROUPS, S5_STATE, S5_GROUP_CH)
        return jnp.einsum('ngpc,gh->ngchp', x, eye).reshape(S5_N_CB, S5_CB_CH, S5_CB_STATES)

    def bd_out(cc):
        x = cc.reshape(S5_N_CB, S5_CB_GROUPS, S5_GROUP_CH, S5_STATE)
        return jnp.einsum('ngcp,gh->ngphc', x, eye).reshape(S5_N_CB, S5_CB_STATES, S5_CB_CH)

    bmat = jnp.concatenate([bd_in(bb_re), bd_in(bb_im)], axis=2).astype(BF16)
    cmat = jnp.concatenate([bd_out(c_re.astype(F32)), -bd_out(c_im.astype(F32))], axis=1).astype(BF16)
    rep = lambda a: jnp.broadcast_to(a.reshape(S5_N_CB, 1, S5_CB_STATES), (S5_N_CB, S5_SEQS, S5_CB_STATES))
    return dict(w_in=w_in.astype(BF16), w_glu=w_glu.astype(BF16), bmat=bmat, cmat=cmat,
                ar=rep(ab_re), ai=rep(ab_im), ab_re=ab_re, ab_im=ab_im, d=d.reshape(1, D_MODEL).astype(F32))


def _s5_scan_kernel(u_ref, b_ref, c_ref, ar_ref, ai_ref, d_ref, h0r_ref, h0i_ref, *rest, steps, with_y):
    if with_y:
        y_ref, hr_ref, hi_ref, bu_ref = rest
    else:
        hr_ref, hi_ref, bu_ref = rest
    ns = S5_CB_STATES

    @pl.when(pl.program_id(1) == 0)
    def _():
        hr_ref[...] = h0r_ref[...]
        hi_ref[...] = h0i_ref[...]

    u = u_ref[...]
    bu_ref[...] = _dot(u.astype(BF16), b_ref[0])
    ar = ar_ref[0]
    ai = ai_ref[0]

    def step(s, carry):
        hr, hi = carry
        r0 = pl.multiple_of(s * S5_SEQS, S5_SEQS)
        br = bu_ref[pl.ds(r0, S5_SEQS), :ns]
        bi = bu_ref[pl.ds(r0, S5_SEQS), ns:]
        nr = ar * hr - ai * hi + br
        ni = ar * hi + ai * hr + bi
        if with_y:
            bu_ref[pl.ds(r0, S5_SEQS), :ns] = nr
            bu_ref[pl.ds(r0, S5_SEQS), ns:] = ni
        return nr, ni

    hr, hi = lax.fori_loop(0, steps, step, (hr_ref[0], hi_ref[0]))
    hr_ref[0] = hr
    hi_ref[0] = hi
    if with_y:
        y_ref[...] = _dot(bu_ref[...].astype(BF16), c_ref[0]) + d_ref[...] * u


def _s5_scan(u_rows, w, h0r, h0i, with_y=True):
    n_rows = u_rows.shape[0]
    n_steps = n_rows // S5_SEQS
    steps = min(S5_MAX_STEPS, n_steps)
    assert n_steps % steps == 0
    tr = steps * S5_SEQS
    cb_spec = lambda a: pl.BlockSpec((1,) + a.shape[1:], lambda cb, t: (cb,) + (0,) * (a.ndim - 1))
    y_spec = [pl.BlockSpec((tr, S5_CB_CH), lambda cb, t: (t, cb))] if with_y else []
    y_shape = [jax.ShapeDtypeStruct((n_rows, D_MODEL), F32)] if with_y else []
    return pl.pallas_call(
        functools.partial(_s5_scan_kernel, steps=steps, with_y=with_y),
        grid=(S5_N_CB, n_steps // steps),
        in_specs=[pl.BlockSpec((tr, S5_CB_CH), lambda cb, t: (t, cb)),
                  cb_spec(w['bmat']), cb_spec(w['cmat']), cb_spec(w['ar']), cb_spec(w['ai']),
                  pl.BlockSpec((1, S5_CB_CH), lambda cb, t: (0, cb)),
                  cb_spec(h0r), cb_spec(h0i)],
        out_specs=y_spec + [cb_spec(h0r), cb_spec(h0i)],
        out_shape=y_shape + [jax.ShapeDtypeStruct(h0r.shape, F32), jax.ShapeDtypeStruct(h0i.shape, F32)],
        scratch_shapes=[pltpu.VMEM((tr, 2 * S5_CB_STATES), F32)],
        compiler_params=_cparams("parallel", "arbitrary"),
        name="s5_scan",
    )(u_rows, w['bmat'], w['cmat'], w['ar'], w['ai'], w['d'], h0r, h0i)


def _state_to_blocks(h):
    return h.reshape(h.shape[0], S5_N_CB, S5_CB_STATES).transpose(1, 0, 2)


def _blocks_to_state(h):
    return h.transpose(1, 0, 2).reshape(h.shape[1], S5_GROUPS, S5_STATE)


def _cpow2(re, im, n):
    for _ in range(n):
        re, im = re * re - im * im, 2.0 * re * im
    return re, im


def _s5_prompt(x, w):
    b, t, _ = x.shape
    n_seg = S5_SEQS // b
    seg = t // n_seg
    assert n_seg * b == S5_SEQS and seg & (seg - 1) == 0
    xr = x.reshape(b, n_seg, seg, D_MODEL).transpose(2, 0, 1, 3).reshape(t * b, D_MODEL)
    u = _mm(xr.astype(BF16), w['w_in'])
    zero = jnp.zeros((S5_N_CB, S5_SEQS, S5_CB_STATES), F32)
    er, ei = _s5_scan(u, w, zero, zero, with_y=False)
    er = _blocks_to_state(er).reshape(b, n_seg, S5_GROUPS, S5_STATE)
    ei = _blocks_to_state(ei).reshape(b, n_seg, S5_GROUPS, S5_STATE)
    pr, pi = _cpow2(w['ab_re'], w['ab_im'], int(math.log2(seg)))
    sr = [jnp.zeros((b, S5_GROUPS, S5_STATE), F32)]
    si = [jnp.zeros((b, S5_GROUPS, S5_STATE), F32)]
    for k in range(n_seg - 1):
        sr.append(er[:, k] + pr * sr[k] - pi * si[k])
        si.append(ei[:, k] + pr * si[k] + pi * sr[k])
    h0r = _state_to_blocks(jnp.stack(sr, axis=1).reshape(S5_SEQS, S5_GROUPS, S5_STATE))
    h0i = _state_to_blocks(jnp.stack(si, axis=1).reshape(S5_SEQS, S5_GROUPS, S5_STATE))
    y, hr, hi = _s5_scan(u, w, h0r, h0i)
    hr = _blocks_to_state(hr).reshape(b, n_seg, S5_GROUPS, S5_STATE)[:, -1]
    hi = _blocks_to_state(hi).reshape(b, n_seg, S5_GROUPS, S5_STATE)[:, -1]
    z = _mm(_gelu_rows(y), w['w_glu'])
    unperm = lambda a: a.reshape(seg, b, n_seg, -1).transpose(1, 2, 0, 3).reshape(b * t, -1)
    return unperm(z), jnp.stack([hr, hi], axis=1)


def _s5_sample(x, h0, w):
    db, t, _ = x.shape
    assert db == S5_SEQS
    xr = x.transpose(1, 0, 2).reshape(t * db, D_MODEL)
    u = _mm(xr.astype(BF16), w['w_in'])
    y, hr, hi = _s5_scan(u, w, _state_to_blocks(h0[:, 0].astype(F32)), _state_to_blocks(h0[:, 1].astype(F32)))
    z = _mm(_gelu_rows(y), w['w_glu'])
    z = z.reshape(t, db, -1).transpose(1, 0, 2).reshape(db * t, -1)
    return z, jnp.stack([_blocks_to_state(hr), _blocks_to_state(hi)], axis=1).astype(h0.dtype)


def _gelu_kernel(y_ref, o_ref):
    o_ref[...] = _gelu_tanh(y_ref[...]).astype(o_ref.dtype)


def _gelu_rows(y):
    n, d = y.shape
    tm = min(256, n)
    row = pl.BlockSpec((tm, d), lambda i: (i, 0))
    return pl.pallas_call(_gelu_kernel, grid=(n // tm,), in_specs=[row], out_specs=row,
                          out_shape=jax.ShapeDtypeStruct((n, d), BF16),
                          compiler_params=_cparams("parallel"), name="gelu")(y)


ROUTER_TN = 256
N_RANKED = PEER_TOPK
_CAND_PAIRS = [(i, j) for i in range(N_RANKED) for j in range(N_RANKED) if (i + 1) * (j + 1) <= N_RANKED]


UNRANKED = 127.0


def _top_values(s, n, with_rank=False):
    vals = []
    cur = s
    rank = jnp.full(s.shape, UNRANKED, F32)
    for t in range(n):
        m = jnp.max(cur, axis=0, keepdims=True)
        vals.append(m)
        hit = cur == m
        if with_rank:
            rank = jnp.where(hit, float(t), rank)
        cur = jnp.where(hit, -jnp.inf, cur)
    return (vals, rank) if with_rank else vals


def _router_kernel(x_ref, wq_ref, keys_ref, w1z_ref, cnt_ref, rank_ref, w2_ref):
    q = _dot(x_ref[...].astype(BF16), wq_ref[...])
    for h in range(PEER_HEADS):
        s = [_dot_nt(keys_ref[2 * h + c], q[:, (2 * h + c) * PEER_HALF:(2 * h + c + 1) * PEER_HALF].astype(BF16))
             for c in range(2)]
        top1 = _top_values(s[0], N_RANKED)
        top2, rank2 = _top_values(s[1], N_RANKED, with_rank=True)
        cand = jnp.concatenate([top1[i] + top2[j] for i, j in _CAND_PAIRS], axis=0)
        tau = _top_values(cand, PEER_TOPK)[PEER_TOPK - 1]
        m1, m2 = top1[0], top2[0]
        z = jnp.sum(jnp.where(cand >= tau, jnp.exp(cand - (m1 + m2)), 0.0), axis=0, keepdims=True)
        cnt = jnp.zeros(s[0].shape, F32)
        for j in range(N_RANKED):
            cnt = cnt + jnp.where(s[0] + top2[j] >= tau, 1.0, 0.0)
        w1z_ref[h] = jnp.exp(s[0] - m1) / z
        cnt_ref[h] = cnt
        rank_ref[h] = rank2.astype(rank_ref.dtype)
        w2_ref[h] = jnp.exp(s[1] - m2).astype(w2_ref.dtype)


def _router(x, wq, keys):
    n = x.shape[0]
    tn = min(ROUTER_TN, n)
    assert n % tn == 0
    shape = (PEER_HEADS, PEER_N_KEYS, n)
    ospec = pl.BlockSpec((PEER_HEADS, PEER_N_KEYS, tn), lambda i: (0, 0, i))
    return pl.pallas_call(
        _router_kernel, grid=(n // tn,),
        in_specs=[pl.BlockSpec((tn, D_MODEL), lambda i: (i, 0)),
                  pl.BlockSpec(wq.shape, lambda i: (0, 0)),
                  pl.BlockSpec(keys.shape, lambda i: (0, 0, 0))],
        out_specs=[ospec] * 4,
        out_shape=[jax.ShapeDtypeStruct(shape, F32), jax.ShapeDtypeStruct(shape, F32),
                   jax.ShapeDtypeStruct(shape, BF16), jax.ShapeDtypeStruct(shape, BF16)],
        compiler_params=_cparams("parallel"), name="peer_router",
    )(x, wq, keys)


EXPERT_TN = 512
EXPERT_TE = 1024
EXPERT_NC = EXPERT_TE // PEER_N_KEYS
EXPERT_NT = PEER_N_EXPERTS // EXPERT_TE


def _expert_gate(w1z_ref, cnt_ref, rank_ref, w2_ref, c):
    g = None
    for h in range(PEER_HEADS):
        w2 = w2_ref[h]
        w1 = w1z_ref[h, pl.ds(c, 1), :].astype(BF16)
        cnt = cnt_ref[h, pl.ds(c, 1), :].astype(BF16)
        term = w1 * jnp.where(rank_ref[h] < cnt, w2, jnp.zeros_like(w2))
        g = term if g is None else g + term
    return g


def _expert_kernel(xt_ref, u_ref, v_ref, w1z_ref, cnt_ref, rank_ref, w2_ref, o_ref):
    j = pl.program_id(1)

    @pl.when(j == 0)
    def _():
        o_ref[...] = jnp.zeros_like(o_ref)

    ht = _dot(u_ref[...], xt_ref[...])
    parts = []
    for cc in range(EXPERT_NC):
        g = _expert_gate(w1z_ref, cnt_ref, rank_ref, w2_ref, j * EXPERT_NC + cc)
        parts.append(_gelu_tanh(ht[cc * PEER_N_KEYS:(cc + 1) * PEER_N_KEYS]).astype(BF16) * g)
    o_ref[...] += lax.dot_general(v_ref[...], jnp.concatenate(parts, axis=0), (((0,), (0,)), ((), ())),
                                  preferred_element_type=F32)


def _expert(xt, u, v, layer, w1z, cnt, rank, w2):
    n = xt.shape[1]
    tn = min(EXPERT_TN, n)
    assert n % tn == 0
    rspec = pl.BlockSpec((PEER_HEADS, PEER_N_KEYS, tn), lambda i, j: (0, 0, i))
    wspec = pl.BlockSpec((None, EXPERT_TE, D_MODEL), lambda i, j: (layer, j, 0))
    return pl.pallas_call(
        _expert_kernel, grid=(n // tn, EXPERT_NT),
        in_specs=[pl.BlockSpec((D_MODEL, tn), lambda i, j: (0, i)), wspec, wspec, rspec, rspec, rspec, rspec],
        out_specs=pl.BlockSpec((D_MODEL, tn), lambda i, j: (0, i)),
        out_shape=jax.ShapeDtypeStruct((D_MODEL, n), F32),
        compiler_params=_cparams("parallel", "arbitrary"), name="peer_expert",
    )(xt, u, v, w1z, cnt, rank, w2)


def _peer_tables(peer_u, peer_v):
    return peer_u.astype(BF16), peer_v.astype(BF16)


def _peer_t(x, xt, w_q, sub_keys, tables, layer):
    u, v = tables
    keys = sub_keys.reshape(2 * PEER_HEADS, PEER_N_KEYS, PEER_HALF).astype(BF16)
    w1z, cnt, rank, w2 = _router(x, w_q.astype(BF16), keys)
    return _expert(xt, u, v, layer, w1z, cnt, rank, w2)


def _peer(x, w_q, sub_keys, tables, layer):
    n = x.shape[0]
    xp = jnp.pad(x, ((0, -(-n // LANES) * LANES - n), (0, 0)))
    return _peer_t(xp, xp.T.astype(BF16), w_q, sub_keys, tables, layer).T[:n]


def kernel(x_prompt, x_sample, cache_nsa, state_win, state_s5, page_table, nsa_w_in, nsa_cmp_pool, nsa_cmp_pe,
           nsa_cmp_phi, nsa_w_out, s5_w_in, s5_a_re, s5_a_im, s5_log_dt, s5_b_re, s5_b_im, s5_c_re, s5_c_im, s5_d,
           s5_w_glu, peer_w_q, peer_sub_keys, peer_u, peer_v, ln_g, ln_b):
    b, s, _ = x_prompt.shape
    db, t, _ = x_sample.shape
    xp = x_prompt.reshape(b * s, D_MODEL)
    xs = x_sample.reshape(db * t, D_MODEL)
    rows_p, rows_s, win_p, win_s, s5_p, s5_s = [], [], [], [], [], []
    tables = _peer_tables(peer_u, peer_v)
    for layer in range(DEPTH):
        j = layer // N_MIXERS
        if layer % N_MIXERS == 0:
            w = _nsa_weights(nsa_w_in[j], nsa_cmp_pool[j], nsa_cmp_pe[j], nsa_cmp_phi[j], nsa_w_out[j])
            mp, rp, wp = _nsa_prompt(xp.reshape(b, s, D_MODEL), w)
            ms, rs, ws = _nsa_sample(xs.reshape(db, t, D_MODEL), cache_nsa, j, state_win[j], page_table, w)
            rows_p.append(rp)
            rows_s.append(rs)
            win_p.append(wp)
            win_s.append(ws)
            xp, xpt = _ln_res(xp, mp.reshape(b * s, D_MODEL), ln_g[layer, 0], ln_b[layer, 0], with_t=True)
            xs = _ln_res(xs, ms.reshape(db * t, D_MODEL), ln_g[layer, 0], ln_b[layer, 0])
        else:
            w = _s5_weights(s5_w_in[j], s5_a_re[j], s5_a_im[j], s5_log_dt[j], s5_b_re[j], s5_b_im[j],
                            s5_c_re[j], s5_c_im[j], s5_d[j], s5_w_glu[j])
            zp, hp = _s5_prompt(xp.reshape(b, s, D_MODEL), w)
            zs, hs = _s5_sample(xs.reshape(db, t, D_MODEL), state_s5[j], w)
            s5_p.append(hp.astype(state_s5.dtype))
            s5_s.append(hs)
            xp, xpt = _ln_res(xp, zp, ln_g[layer, 0], ln_b[layer, 0], mixer='glu', with_t=True)
            xs = _ln_res(xs, zs, ln_g[layer, 0], ln_b[layer, 0], mixer='glu')
        xp = _ln_res(xp, _peer_t(xp, xpt, peer_w_q[layer], peer_sub_keys[layer], tables, layer),
                     ln_g[layer, 1], ln_b[layer, 1], mixer='cols')
        xs = _ln_res(xs, _peer(xs, peer_w_q[layer], peer_sub_keys[layer], tables, layer),
                     ln_g[layer, 1], ln_b[layer, 1])
    return (xp.reshape(b, s, D_MODEL), xs.reshape(db, t, D_MODEL), jnp.stack(rows_p), jnp.stack(rows_s),
            jnp.stack(win_p), jnp.stack(win_s), jnp.stack(s5_p), jnp.stack(s5_s))
```

```python
import functools
import math

import numpy as np
import jax
import jax.numpy as jnp
from jax import lax
from jax.experimental import pallas as pl
from jax.experimental.pallas import tpu as pltpu

F32 = jnp.float32
BF16 = jnp.bfloat16

D_MODEL = 2048
DEPTH = 4
PAGE_SIZE = 128
N_MIXERS = 2
NSA_HEADS = 16
NSA_KV_HEADS = 4
NSA_GROUP = NSA_HEADS // NSA_KV_HEADS
NSA_HEAD_DIM = D_MODEL // NSA_HEADS
NSA_Q_DIM = NSA_HEADS * NSA_HEAD_DIM
NSA_KV_DIM = NSA_KV_HEADS * NSA_HEAD_DIM
NSA_MAIN_DIM = NSA_Q_DIM + 6 * NSA_KV_DIM
NSA_N_GATES = 3 * NSA_HEADS
NSA_SCALE = NSA_HEAD_DIM ** -0.5
CMP_BLOCK = 32
CMP_STRIDE = 16
SLC_BLOCK = 64
SLC_TOP_N = 16
WINDOW = 512
FORCE_BONUS = 1.0e4
S5_GROUP_CH = 16
S5_GROUPS = D_MODEL // S5_GROUP_CH
S5_STATE = 64
PEER_HEADS = 8
PEER_N_KEYS = 128
PEER_N_EXPERTS = PEER_N_KEYS ** 2
PEER_HALF = 128
PEER_TOPK = 16
LN_EPS = 1e-5
NEG_INF = -1.0e30
ALPHA = (2.0 * DEPTH) ** 0.25

LANES = 128
SUBLANES = 8
VMEM_LIMIT_BYTES = 56 * 1024 * 1024

CMP_CHUNK = 2048
CMP_ROWS = CMP_CHUNK // CMP_STRIDE
SAMPLE_PAGES_PER_STEP = CMP_CHUNK // PAGE_SIZE

_SLOPES = [2.0 ** (-8.0 * (h + 1) / NSA_HEADS) for h in range(NSA_HEADS)]


def _cparams(*sem):
    return pltpu.CompilerParams(dimension_semantics=sem, vmem_limit_bytes=VMEM_LIMIT_BYTES)


def _dot(a, b):
    return jnp.dot(a, b, preferred_element_type=F32)


def _dot_nt(a, b):
    return lax.dot_general(a, b, (((1,), (1,)), ((), ())), preferred_element_type=F32)


def _split3(x):
    hi = x.astype(BF16)
    r1 = x - hi.astype(F32)
    mid = r1.astype(BF16)
    lo = (r1 - mid.astype(F32)).astype(BF16)
    return hi, mid, lo


def _gelu_tanh(x):
    c = math.sqrt(2.0 / math.pi)
    return 0.5 * x * (1.0 + jnp.tanh(c * (x + 0.044715 * (x * x * x))))


def _mm_kernel(a_ref, b_ref, o_ref):
    o_ref[...] = _dot(a_ref[...], b_ref[...]).astype(o_ref.dtype)


def _mm(a, b, tm=512, tn=1024, out_dtype=F32):
    m, k = a.shape
    n = b.shape[1]
    tm = min(tm, m)
    tn = min(tn, n)
    assert m % tm == 0 and n % tn == 0
    return pl.pallas_call(
        _mm_kernel,
        grid=(n // tn, m // tm),
        in_specs=[pl.BlockSpec((tm, k), lambda j, i: (i, 0)),
                  pl.BlockSpec((k, tn), lambda j, i: (0, j))],
        out_specs=pl.BlockSpec((tm, tn), lambda j, i: (i, j)),
        out_shape=jax.ShapeDtypeStruct((m, n), out_dtype),
        compiler_params=_cparams("parallel", "parallel"),
        name="mm",
    )(a, b)


def _ln_kernel(x_ref, *refs, mixer, with_t):
    n_in = 2 if mixer == 'glu' else 1
    g_ref, b_ref = refs[n_in:n_in + 2]
    outs = refs[n_in + 2:]
    if mixer == 'glu':
        mix = refs[0][...] * (1.0 / (1.0 + jnp.exp(-refs[1][...])))
    elif mixer == 'cols':
        mix = refs[0][...].T
    else:
        mix = refs[0][...]
    y = ALPHA * x_ref[...] + mix
    mu = jnp.mean(y, axis=-1, keepdims=True)
    yc = y - mu
    var = jnp.mean(yc * yc, axis=-1, keepdims=True)
    out = yc * lax.rsqrt(var + LN_EPS) * g_ref[...] + b_ref[...]
    outs[0][...] = out
    if with_t:
        outs[1][...] = out.T.astype(outs[1].dtype)


def _ln_res(x, m, g, b, mixer='rows', with_t=False):
    n, d = x.shape
    tm = min(256, n)
    assert n % tm == 0
    row = pl.BlockSpec((tm, d), lambda i: (i, 0))
    col = pl.BlockSpec((d, tm), lambda i: (0, i))
    vec = pl.BlockSpec((1, d), lambda i: (0, 0))
    if mixer == 'glu':
        m_specs, m_args = [row, pl.BlockSpec((tm, d), lambda i: (i, 1))], (m, m)
    elif mixer == 'cols':
        m_specs, m_args = [col], (m,)
    else:
        m_specs, m_args = [row], (m,)
    out_specs = [row] + ([col] if with_t else [])
    out_shape = [jax.ShapeDtypeStruct((n, d), F32)] + ([jax.ShapeDtypeStruct((d, n), BF16)] if with_t else [])
    res = pl.pallas_call(
        functools.partial(_ln_kernel, mixer=mixer, with_t=with_t),
        grid=(n // tm,), in_specs=[row] + m_specs + [vec, vec], out_specs=out_specs, out_shape=out_shape,
        compiler_params=_cparams("parallel"), name="ln_res",
    )(x, *m_args, g.reshape(1, d), b.reshape(1, d))
    return res if with_t else res[0]


def _compress_consts(pool, pe, phi):
    eye_cur = jnp.eye(CMP_ROWS, dtype=F32)
    eye_prev = jnp.eye(CMP_ROWS, k=-1, dtype=F32)
    lo = pool[:, :, None, None, :CMP_STRIDE]
    hi = pool[:, :, None, None, CMP_STRIDE:]
    pb = (eye_prev[None, None, :, :, None] * lo + eye_cur[None, None, :, :, None] * hi).reshape(
        2, NSA_KV_HEADS, CMP_ROWS, CMP_CHUNK)
    first = jnp.asarray((np.arange(CMP_ROWS) == 0).astype(np.float32))[None, None, :, None]
    pbt = first * pool[:, :, None, :CMP_STRIDE]
    pe_term = jnp.einsum('skj,skjd->skd', pool, pe)
    return (pb.astype(BF16), pbt.astype(BF16), pe_term.reshape(2 * NSA_KV_HEADS, NSA_HEAD_DIM).astype(F32),
            phi.astype(BF16))


def _compress_chunk(x, tail, pb_ref, pbt_ref, pe_ref, phi_ref):
    outs = []
    for slot in range(2):
        for kv in range(NSA_KV_HEADS):
            lo = slot * NSA_KV_DIM + kv * NSA_HEAD_DIM
            xs = x[:, lo:lo + NSA_HEAD_DIM]
            pooled = _dot(pb_ref[slot, kv], xs) + _dot(pbt_ref[slot, kv], tail[:, lo:lo + NSA_HEAD_DIM])
            pooled = pooled + pe_ref[pl.ds(slot * NSA_KV_HEADS + kv, 1), :]
            outs.append(_dot(pooled.astype(BF16), phi_ref[slot, kv]))
    return jnp.concatenate(outs, axis=1)


def _slope_col(kv, rows, per):
    gi = lax.broadcasted_iota(jnp.int32, (rows, 1), 0) // per
    col = jnp.full((rows, 1), _SLOPES[kv * NSA_GROUP + NSA_GROUP - 1], F32)
    for g in range(NSA_GROUP - 1):
        col = jnp.where(gi == g, _SLOPES[kv * NSA_GROUP + g], col)
    return col


def _select_top(score, axis, n_entries):
    idx = lax.broadcasted_iota(jnp.int32, score.shape, axis).astype(F32)
    sel = jnp.zeros(score.shape, F32)
    cur = score
    for _ in range(SLC_TOP_N):
        m = jnp.max(cur, axis=axis, keepdims=True)
        first = jnp.min(jnp.where(cur == m, idx, float(n_entries)), axis=axis, keepdims=True)
        hit = idx == first
        sel = jnp.where(hit, 1.0, sel)
        cur = jnp.where(hit, -2.0, cur)
    return sel


def _compress_kernel(x_ref, pb_ref, pbt_ref, pe_ref, phi_ref, o_ref, tail_ref):
    @pl.when(pl.program_id(1) == 0)
    def _():
        tail_ref[...] = jnp.zeros_like(tail_ref)

    x = x_ref[0]
    o_ref[0] = _compress_chunk(x, tail_ref[...], pb_ref, pbt_ref, pe_ref, phi_ref).astype(o_ref.dtype)
    tail_ref[...] = x[CMP_CHUNK - CMP_STRIDE:, :]


def _compress(kvb, consts):
    b, s, _ = kvb.shape
    w = 2 * NSA_KV_DIM
    n_chunks = s // CMP_CHUNK
    pb, pbt, pe_term, phi = consts
    full = lambda a: pl.BlockSpec(a.shape, lambda i, c: (0,) * a.ndim)
    return pl.pallas_call(
        _compress_kernel,
        grid=(b, n_chunks),
        in_specs=[pl.BlockSpec((1, CMP_CHUNK, w), lambda i, c: (i, c, 0)),
                  full(pb), full(pbt), full(pe_term), full(phi)],
        out_specs=pl.BlockSpec((1, CMP_ROWS, w), lambda i, c: (i, c, 0)),
        out_shape=jax.ShapeDtypeStruct((b, s // CMP_STRIDE, w), BF16),
        scratch_shapes=[pltpu.VMEM((CMP_STRIDE, w), BF16)],
        compiler_params=_cparams("parallel", "arbitrary"),
        name="nsa_compress",
    )(kvb, pb, pbt, pe_term, phi)


CMP_TQ = 128


def _cmp_prompt_kernel(q_ref, ckv_ref, mt_ref, oc_ref, sel_ref):
    tq = CMP_TQ
    n_r = ckv_ref.shape[1]
    n_slc = mt_ref.shape[0]
    q0 = pl.program_id(1) * tq
    qpos = q0 + lax.broadcasted_iota(jnp.int32, (tq, n_r), 0)
    r = lax.broadcasted_iota(jnp.int32, (tq, n_r), 1)
    dist_i = qpos - (CMP_STRIDE * r + CMP_STRIDE - 1)
    mask = (dist_i >= 0) & (r >= 1)
    maskf = mask.astype(F32)
    dist = dist_i.astype(F32)
    q = q_ref[0]
    ckv = ckv_ref[0]
    blk = lax.broadcasted_iota(jnp.int32, (n_slc, tq), 0)
    cur = (q0 + lax.broadcasted_iota(jnp.int32, (n_slc, tq), 1)) // SLC_BLOCK
    avail = blk <= cur
    forced = (blk == 0) | (blk == cur) | (blk == cur - 1)
    for kv in range(NSA_KV_HEADS):
        ck = ckv[:, kv * NSA_HEAD_DIM:(kv + 1) * NSA_HEAD_DIM]
        cv = ckv[:, NSA_KV_DIM + kv * NSA_HEAD_DIM:NSA_KV_DIM + (kv + 1) * NSA_HEAD_DIM]
        qs = jnp.concatenate([q[:, (kv * NSA_GROUP + g) * NSA_HEAD_DIM:(kv * NSA_GROUP + g + 1) * NSA_HEAD_DIM]
                              for g in range(NSA_GROUP)], axis=0)
        s_all = _dot_nt(qs, ck)
        p_sum = jnp.zeros((tq, n_r), F32)
        for g in range(NSA_GROUP):
            h = kv * NSA_GROUP + g
            s = jnp.where(mask, s_all[g * tq:(g + 1) * tq] - _SLOPES[h] * dist, NEG_INF)
            e = jnp.exp(s - jnp.max(s, axis=-1, keepdims=True)) * maskf
            p = e / jnp.maximum(jnp.sum(e, axis=-1, keepdims=True), 1e-30)
            oc_ref[0, :, h * NSA_HEAD_DIM:(h + 1) * NSA_HEAD_DIM] = _dot(p.astype(BF16), cv)
            p_sum = p_sum + p
        mt = mt_ref[...]
        p_slc = sum(_dot_nt(mt, part) for part in _split3(p_sum))
        score = jnp.where(avail, p_slc + jnp.where(forced, FORCE_BONUS, 0.0), -1.0)
        sel_ref[0, kv] = _select_top(score, 0, n_slc)


def _slc_map(n_slc, n_r, n_pad):
    m = np.arange(n_pad)[:, None]
    r = np.arange(n_r)[None, :]
    ratio = SLC_BLOCK // CMP_STRIDE
    return jnp.asarray(((r >= ratio * m) & (r <= ratio * m + ratio) & (m < n_slc)).astype(np.float32), BF16)


def _cmp_prompt(qs, ckv):
    b, s, _ = qs.shape
    n_r = ckv.shape[1]
    n_slc = s // SLC_BLOCK
    mt = _slc_map(n_slc, n_r, n_slc)
    return pl.pallas_call(
        _cmp_prompt_kernel,
        grid=(b, s // CMP_TQ),
        in_specs=[pl.BlockSpec((1, CMP_TQ, NSA_Q_DIM), lambda i, t: (i, t, 0)),
                  pl.BlockSpec((1, n_r, 2 * NSA_KV_DIM), lambda i, t: (i, 0, 0)),
                  pl.BlockSpec(mt.shape, lambda i, t: (0, 0))],
        out_specs=[pl.BlockSpec((1, CMP_TQ, NSA_Q_DIM), lambda i, t: (i, t, 0)),
                   pl.BlockSpec((1, NSA_KV_HEADS, n_slc, CMP_TQ), lambda i, t: (i, 0, 0, t))],
        out_shape=[jax.ShapeDtypeStruct((b, s, NSA_Q_DIM), F32),
                   jax.ShapeDtypeStruct((b, NSA_KV_HEADS, n_slc, s), F32)],
        compiler_params=_cparams("parallel", "parallel"),
        name="nsa_cmp_prompt",
    )(qs, ckv, mt)


SLC_TQ = 256
SLC_TK = 1024


def _slc_prompt_kernel(q_ref, k_ref, v_ref, sel_ref, o_ref, m_ref, l_ref, acc_ref):
    tq, tk = SLC_TQ, SLC_TK
    qi = pl.program_id(1)
    kj = pl.program_id(2)
    n_sel = sel_ref.shape[3]

    @pl.when(kj == 0)
    def _():
        m_ref[...] = jnp.full_like(m_ref, NEG_INF)
        l_ref[...] = jnp.zeros_like(l_ref)
        acc_ref[...] = jnp.zeros_like(acc_ref)

    @pl.when(kj * tk <= qi * tq + tq - 1)
    def _():
        q = q_ref[0]
        kt = k_ref[0]
        vt = v_ref[0]
        qpos = qi * tq + lax.broadcasted_iota(jnp.int32, (tq, tk), 0)
        kpos = kj * tk + lax.broadcasted_iota(jnp.int32, (tq, tk), 1)
        dist_i = qpos - kpos
        dist = dist_i.astype(F32)
        eb = lax.broadcasted_iota(jnp.int32, (n_sel, tk), 0)
        ec = (kj * tk + lax.broadcasted_iota(jnp.int32, (n_sel, tk), 1)) // SLC_BLOCK
        expand = jnp.where(eb == ec, 1.0, 0.0).astype(BF16)
        for kv in range(NSA_KV_HEADS):
            picked = _dot(sel_ref[0, kv], expand)
            mask = (picked > 0.5) & (dist_i >= 0)
            qs = jnp.concatenate([q[:, (kv * NSA_GROUP + g) * NSA_HEAD_DIM:(kv * NSA_GROUP + g + 1) * NSA_HEAD_DIM]
                                  for g in range(NSA_GROUP)], axis=0)
            s_all = _dot_nt(qs, kt[:, kv * NSA_HEAD_DIM:(kv + 1) * NSA_HEAD_DIM])
            vv = vt[:, kv * NSA_HEAD_DIM:(kv + 1) * NSA_HEAD_DIM]
            for g in range(NSA_GROUP):
                h = kv * NSA_GROUP + g
                rows = slice(g * tq, (g + 1) * tq)
                s = jnp.where(mask, s_all[rows] - _SLOPES[h] * dist, NEG_INF)
                m_old = m_ref[kv, rows]
                m_new = jnp.maximum(m_old, jnp.max(s, axis=-1, keepdims=True))
                a = jnp.exp(m_old - m_new)
                e = jnp.exp(s - m_new)
                l_ref[kv, rows] = a * l_ref[kv, rows] + jnp.sum(e, axis=-1, keepdims=True)
                acc_ref[kv, rows] = a * acc_ref[kv, rows] + _dot(e.astype(BF16), vv)
                m_ref[kv, rows] = m_new

    @pl.when(kj == pl.num_programs(2) - 1)
    def _():
        for kv in range(NSA_KV_HEADS):
            for g in range(NSA_GROUP):
                h = kv * NSA_GROUP + g
                rows = slice(g * tq, (g + 1) * tq)
                o_ref[0, :, h * NSA_HEAD_DIM:(h + 1) * NSA_HEAD_DIM] = (
                    acc_ref[kv, rows] / jnp.maximum(l_ref[kv, rows], 1e-30))


def _slc_prompt(qs, kvb, sel):
    b, s, _ = qs.shape
    tq, tk = min(SLC_TQ, s), min(SLC_TK, s)
    assert tq == SLC_TQ and tk == SLC_TK
    n_sel = sel.shape[3]
    last = lambda t: (t * tq + tq - 1) // tk
    return pl.pallas_call(
        _slc_prompt_kernel,
        grid=(b, s // tq, s // tk),
        in_specs=[pl.BlockSpec((1, tq, NSA_Q_DIM), lambda i, t, j: (i, t, 0)),
                  pl.BlockSpec((1, tk, NSA_KV_DIM), lambda i, t, j: (i, jnp.minimum(j, last(t)), 2)),
                  pl.BlockSpec((1, tk, NSA_KV_DIM), lambda i, t, j: (i, jnp.minimum(j, last(t)), 3)),
                  pl.BlockSpec((1, NSA_KV_HEADS, tq, n_sel), lambda i, t, j: (i, 0, t, 0))],
        out_specs=pl.BlockSpec((1, tq, NSA_Q_DIM), lambda i, t, j: (i, t, 0)),
        out_shape=jax.ShapeDtypeStruct((b, s, NSA_Q_DIM), F32),
        scratch_shapes=[pltpu.VMEM((NSA_KV_HEADS, NSA_GROUP * tq, 1), F32),
                        pltpu.VMEM((NSA_KV_HEADS, NSA_GROUP * tq, 1), F32),
                        pltpu.VMEM((NSA_KV_HEADS, NSA_GROUP * tq, NSA_HEAD_DIM), F32)],
        compiler_params=_cparams("parallel", "parallel", "arbitrary"),
        name="nsa_slc_prompt",
    )(qs, kvb, kvb, sel)


WIN_TQ = 256
WIN_NT = WINDOW // WIN_TQ + 1


def _win_prompt_kernel(q_ref, *refs):
    tq = WIN_TQ
    k_refs = refs[:WIN_NT]
    v_refs = refs[WIN_NT:2 * WIN_NT]
    o_ref = refs[2 * WIN_NT]
    qi = pl.program_id(1)
    nk = WIN_NT * tq
    row = lax.broadcasted_iota(jnp.int32, (tq, nk), 0)
    col = lax.broadcasted_iota(jnp.int32, (tq, nk), 1)
    dist_i = WINDOW + row - col
    kpos = (qi - (WIN_NT - 1)) * tq + col
    mask = (dist_i >= 0) & (dist_i <= WINDOW) & (kpos >= 0)
    dist = dist_i.astype(F32)
    q = q_ref[0]
    kcat = jnp.concatenate([r[0] for r in k_refs], axis=0)
    vcat = jnp.concatenate([r[0] for r in v_refs], axis=0)
    for kv in range(NSA_KV_HEADS):
        qs = jnp.concatenate([q[:, (kv * NSA_GROUP + g) * NSA_HEAD_DIM:(kv * NSA_GROUP + g + 1) * NSA_HEAD_DIM]
                              for g in range(NSA_GROUP)], axis=0)
        s_all = _dot_nt(qs, kcat[:, kv * NSA_HEAD_DIM:(kv + 1) * NSA_HEAD_DIM])
        vv = vcat[:, kv * NSA_HEAD_DIM:(kv + 1) * NSA_HEAD_DIM]
        for g in range(NSA_GROUP):
            h = kv * NSA_GROUP + g
            s = jnp.where(mask, s_all[g * tq:(g + 1) * tq] - _SLOPES[h] * dist, NEG_INF)
            e = jnp.exp(s - jnp.max(s, axis=-1, keepdims=True))
            den = jnp.maximum(jnp.sum(e, axis=-1, keepdims=True), 1e-30)
            o_ref[0, :, h * NSA_HEAD_DIM:(h + 1) * NSA_HEAD_DIM] = _dot(e.astype(BF16), vv) / den


def _win_prompt(qs, kvb):
    b, s, _ = qs.shape
    tq = WIN_TQ
    kspec = lambda d, col: pl.BlockSpec((1, tq, NSA_KV_DIM),
                                        lambda i, t: (i, jnp.maximum(t - (WIN_NT - 1) + d, 0), col))
    return pl.pallas_call(
        _win_prompt_kernel,
        grid=(b, s // tq),
        in_specs=[pl.BlockSpec((1, tq, NSA_Q_DIM), lambda i, t: (i, t, 0))]
                 + [kspec(d, 4) for d in range(WIN_NT)] + [kspec(d, 5) for d in range(WIN_NT)],
        out_specs=pl.BlockSpec((1, tq, NSA_Q_DIM), lambda i, t: (i, t, 0)),
        out_shape=jax.ShapeDtypeStruct((b, s, NSA_Q_DIM), F32),
        compiler_params=_cparams("parallel", "parallel"),
        name="nsa_win_prompt",
    )(qs, *([kvb] * (2 * WIN_NT)))


def _merge_kernel(oc_ref, os_ref, ow_ref, hg_ref, ex_ref, o_ref):
    gate = 1.0 / (1.0 + jnp.exp(-hg_ref[...]))
    parts = _split3(gate)[:2]
    out = None
    for br, ref in enumerate((oc_ref, os_ref, ow_ref)):
        gx = sum(_dot(p, ex_ref[br]) for p in parts)
        term = gx * ref[...]
        out = term if out is None else out + term
    o_ref[...] = out.astype(o_ref.dtype)


def _gate_expand():
    e = np.zeros((3, LANES, NSA_Q_DIM), np.float32)
    for br in range(3):
        for h in range(NSA_HEADS):
            e[br, h * 3 + br, h * NSA_HEAD_DIM:(h + 1) * NSA_HEAD_DIM] = 1.0
    return jnp.asarray(e, BF16)


def _merge(oc, os_, ow, hg):
    n = oc.shape[0]
    tm = min(256, n)
    ex = _gate_expand()
    row = pl.BlockSpec((tm, NSA_Q_DIM), lambda i: (i, 0))
    return pl.pallas_call(
        _merge_kernel, grid=(n // tm,),
        in_specs=[row, row, row, pl.BlockSpec((tm, LANES), lambda i: (i, 0)),
                  pl.BlockSpec(ex.shape, lambda i: (0, 0, 0))],
        out_specs=row,
        out_shape=jax.ShapeDtypeStruct((n, NSA_Q_DIM), BF16),
        compiler_params=_cparams("parallel"), name="nsa_merge",
    )(oc, os_, ow, hg, ex)


def _nsa_weights(w_in, pool, pe, phi, w_out):
    w_main = w_in[:, :NSA_MAIN_DIM].astype(BF16)
    w_gate = jnp.pad(w_in[:, NSA_MAIN_DIM:], ((0, 0), (0, LANES - NSA_N_GATES))).astype(BF16)
    return w_main, w_gate, _compress_consts(pool, pe, phi), w_out.astype(BF16)


def _nsa_prompt(x, weights):
    b, s, _ = x.shape
    w_main, w_gate, consts, w_out = weights
    xb = x.reshape(b * s, D_MODEL).astype(BF16)
    h = _mm(xb, w_main)
    hg = _mm(xb, w_gate)
    h3 = h.reshape(b, s, NSA_MAIN_DIM)
    qs = (h3[..., :NSA_Q_DIM] * NSA_SCALE).astype(BF16)
    kvb = h3[..., NSA_Q_DIM:].astype(BF16)
    ckv = _compress(kvb, consts)
    o_c, sel_t = _cmp_prompt(qs, ckv)
    sel = jnp.swapaxes(sel_t, 2, 3).astype(BF16)
    o_s = _slc_prompt(qs, kvb, sel)
    o_w = _win_prompt(qs, kvb)
    o = _merge(o_c.reshape(b * s, NSA_Q_DIM), o_s.reshape(b * s, NSA_Q_DIM), o_w.reshape(b * s, NSA_Q_DIM), hg)
    y = _mm(o, w_out).reshape(b, s, D_MODEL)
    kv = h3[..., NSA_Q_DIM:].reshape(b, s, 6, NSA_KV_HEADS, NSA_HEAD_DIM)
    rows = kv[:, :, :4].reshape(b, s // PAGE_SIZE, PAGE_SIZE, 4, NSA_KV_HEADS, NSA_HEAD_DIM)
    win = kv[:, s - min(WINDOW, s):, 4:]
    return y, rows, win


def _gather_pages(page_refs):
    n_sk = 2 * NSA_KV_HEADS
    pages = []
    for r in page_refs:
        rows = r.reshape(PAGE_SIZE * n_sk, NSA_HEAD_DIM)
        pages.append(jnp.concatenate(
            [rows[pl.ds(sk, PAGE_SIZE, stride=n_sk), :].astype(BF16) for sk in range(n_sk)], axis=1))
    return jnp.concatenate(pages, axis=0)


def _cmp_sample_kernel(pt_ref, *refs, t_new, p_len):
    del pt_ref
    npg = SAMPLE_PAGES_PER_STEP
    page_refs = refs[:npg]
    q_ref, pb_ref, pbt_ref, pe_ref, phi_ref, mt_ref, oc_ref, sel_ref, tail_ref, s_ref, cv_ref = refs[npg:]
    c = pl.program_id(1)
    rows = NSA_GROUP * t_new
    n_r = s_ref.shape[2]

    @pl.when(c == 0)
    def _():
        tail_ref[...] = jnp.zeros_like(tail_ref)

    x = _gather_pages(page_refs)
    ckv = _compress_chunk(x, tail_ref[...], pb_ref, pbt_ref, pe_ref, phi_ref)
    tail_ref[...] = x[CMP_CHUNK - CMP_STRIDE:, :]
    r0 = pl.multiple_of(c * CMP_ROWS, CMP_ROWS)
    cv_ref[pl.ds(r0, CMP_ROWS), :] = ckv[:, NSA_KV_DIM:].astype(BF16)
    for kv in range(NSA_KV_HEADS):
        ck = ckv[:, kv * NSA_HEAD_DIM:(kv + 1) * NSA_HEAD_DIM].astype(BF16)
        s_ref[kv, :, pl.ds(r0, CMP_ROWS)] = _dot_nt(q_ref[0, kv], ck)

    @pl.when(c == pl.num_programs(1) - 1)
    def _():
        ri = lax.broadcasted_iota(jnp.int32, (rows, n_r), 1)
        ti = lax.broadcasted_iota(jnp.int32, (rows, n_r), 0) % t_new
        dist_i = (p_len + ti) - (CMP_STRIDE * ri + CMP_STRIDE - 1)
        mask = (dist_i >= 0) & (ri >= 1)
        maskf = mask.astype(F32)
        dist = dist_i.astype(F32)
        n_pad = mt_ref.shape[1]
        blk = lax.broadcasted_iota(jnp.int32, (rows, n_pad), 1)
        cur = (p_len + lax.broadcasted_iota(jnp.int32, (rows, n_pad), 0) % t_new) // SLC_BLOCK
        n_slc = -(-(p_len + t_new) // SLC_BLOCK)
        avail = blk <= cur
        forced = (blk == 0) | (blk == cur) | (blk == cur - 1)
        gi = lax.broadcasted_iota(jnp.int32, (rows, rows), 0) % t_new
        gj = lax.broadcasted_iota(jnp.int32, (rows, rows), 1) % t_new
        gsum = jnp.where(gi == gj, 1.0, 0.0).astype(BF16)
        for kv in range(NSA_KV_HEADS):
            s = jnp.where(mask, s_ref[kv] - _slope_col(kv, rows, t_new) * dist, NEG_INF)
            e = jnp.exp(s - jnp.max(s, axis=-1, keepdims=True)) * maskf
            p = e / jnp.maximum(jnp.sum(e, axis=-1, keepdims=True), 1e-30)
            oc_ref[0, kv] = _dot(p.astype(BF16), cv_ref[:, kv * NSA_HEAD_DIM:(kv + 1) * NSA_HEAD_DIM])
            p_sum = sum(_dot(gsum, part) for part in _split3(p))
            p_slc = sum(_dot(part, mt_ref[...]) for part in _split3(p_sum))
            score = jnp.where(avail, p_slc + jnp.where(forced, FORCE_BONUS, 0.0), -1.0)
            score = jnp.where(blk < n_slc, score, -3.0)
            sel_ref[0, kv] = _select_top(score, 1, n_pad)


def _page_spec(layer, half, d):
    npg = SAMPLE_PAGES_PER_STEP
    return pl.BlockSpec((1, 1, PAGE_SIZE, 2 * NSA_KV_HEADS, NSA_HEAD_DIM),
                        lambda i, c, pt: (layer, pt[i, c * npg + d], 0, half, 0))


def _cmp_sample(cache, layer, page_table, q16, consts, t_new):
    db, n_pages = page_table.shape
    p_len = n_pages * PAGE_SIZE
    n_chunks = p_len // CMP_CHUNK
    n_r = p_len // CMP_STRIDE
    n_slc = -(-(p_len + t_new) // SLC_BLOCK)
    n_pad = -(-n_slc // LANES) * LANES
    mt = _slc_map(n_slc, n_r, n_pad).T
    pb, pbt, pe_term, phi = consts
    rows = NSA_GROUP * t_new
    full = lambda a: pl.BlockSpec(a.shape, lambda i, c, pt: (0,) * a.ndim)
    gs = pltpu.PrefetchScalarGridSpec(
        num_scalar_prefetch=1,
        grid=(db, n_chunks),
        in_specs=[_page_spec(layer, 0, d) for d in range(SAMPLE_PAGES_PER_STEP)]
                 + [pl.BlockSpec((1, NSA_KV_HEADS, rows, NSA_HEAD_DIM), lambda i, c, pt: (i, 0, 0, 0)),
                    full(pb), full(pbt), full(pe_term), full(phi), full(mt)],
        out_specs=[pl.BlockSpec((1, NSA_KV_HEADS, rows, NSA_HEAD_DIM), lambda i, c, pt: (i, 0, 0, 0)),
                   pl.BlockSpec((1, NSA_KV_HEADS, rows, n_pad), lambda i, c, pt: (i, 0, 0, 0))],
        scratch_shapes=[pltpu.VMEM((CMP_STRIDE, 2 * NSA_KV_DIM), BF16),
                        pltpu.VMEM((NSA_KV_HEADS, rows, n_r), F32),
                        pltpu.VMEM((n_r, NSA_KV_DIM), BF16)],
    )
    return pl.pallas_call(
        functools.partial(_cmp_sample_kernel, t_new=t_new, p_len=p_len),
        grid_spec=gs,
        out_shape=[jax.ShapeDtypeStruct((db, NSA_KV_HEADS, rows, NSA_HEAD_DIM), F32),
                   jax.ShapeDtypeStruct((db, NSA_KV_HEADS, rows, n_pad), F32)],
        compiler_params=_cparams("parallel", "arbitrary"),
        name="nsa_cmp_sample",
    )(page_table, *([cache] * SAMPLE_PAGES_PER_STEP), q16, pb, pbt, pe_term, phi, mt)


def _online_update(s, maskf, vv, m_ref, l_ref, acc_ref, kv):
    m_old = m_ref[kv]
    m_new = jnp.maximum(m_old, jnp.max(s, axis=-1, keepdims=True))
    a = jnp.exp(m_old - m_new)
    e = jnp.exp(s - m_new) * maskf
    l_ref[kv] = a * l_ref[kv] + jnp.sum(e, axis=-1, keepdims=True)
    acc_ref[kv] = a * acc_ref[kv] + _dot(e.astype(BF16), vv)
    m_ref[kv] = m_new


def _slc_sample_kernel(pt_ref, *refs, t_new, p_len):
    del pt_ref
    npg = SAMPLE_PAGES_PER_STEP
    page_refs = refs[:npg]
    q_ref, sel_ref, kn_ref, vn_ref, o_ref, m_ref, l_ref, acc_ref = refs[npg:]
    c = pl.program_id(1)
    rows = NSA_GROUP * t_new
    n_pad = sel_ref.shape[3]

    @pl.when(c == 0)
    def _():
        m_ref[...] = jnp.full_like(m_ref, NEG_INF)
        l_ref[...] = jnp.zeros_like(l_ref)
        acc_ref[...] = jnp.zeros_like(acc_ref)

    x = _gather_pages(page_refs)
    eb = lax.broadcasted_iota(jnp.int32, (n_pad, CMP_CHUNK), 0)
    ec = (c * CMP_CHUNK + lax.broadcasted_iota(jnp.int32, (n_pad, CMP_CHUNK), 1)) // SLC_BLOCK
    expand = jnp.where(eb == ec, 1.0, 0.0).astype(BF16)
    ti = lax.broadcasted_iota(jnp.int32, (rows, CMP_CHUNK), 0) % t_new
    kpos = c * CMP_CHUNK + lax.broadcasted_iota(jnp.int32, (rows, CMP_CHUNK), 1)
    dist = ((p_len + ti) - kpos).astype(F32)
    for kv in range(NSA_KV_HEADS):
        mask = _dot(sel_ref[0, kv], expand) > 0.5
        s = _dot_nt(q_ref[0, kv], x[:, kv * NSA_HEAD_DIM:(kv + 1) * NSA_HEAD_DIM])
        s = jnp.where(mask, s - _slope_col(kv, rows, t_new) * dist, NEG_INF)
        _online_update(s, mask.astype(F32), x[:, NSA_KV_DIM + kv * NSA_HEAD_DIM:NSA_KV_DIM + (kv + 1) * NSA_HEAD_DIM],
                       m_ref, l_ref, acc_ref, kv)

    @pl.when(c == pl.num_programs(1) - 1)
    def _():
        n_new = kn_ref.shape[1]
        ti2 = lax.broadcasted_iota(jnp.int32, (rows, n_new), 0) % t_new
        ci = lax.broadcasted_iota(jnp.int32, (rows, n_new), 1)
        mask2 = (ci <= ti2) & (ci < t_new)
        dist2 = (ti2 - ci).astype(F32)
        for kv in range(NSA_KV_HEADS):
            s = _dot_nt(q_ref[0, kv], kn_ref[0, :, kv * NSA_HEAD_DIM:(kv + 1) * NSA_HEAD_DIM])
            s = jnp.where(mask2, s - _slope_col(kv, rows, t_new) * dist2, NEG_INF)
            _online_update(s, mask2.astype(F32), vn_ref[0, :, kv * NSA_HEAD_DIM:(kv + 1) * NSA_HEAD_DIM],
                           m_ref, l_ref, acc_ref, kv)
            o_ref[0, kv] = acc_ref[kv] / jnp.maximum(l_ref[kv], 1e-30)


def _slc_sample(cache, layer, page_table, q16, sel16, k_new, v_new, t_new):
    db, n_pages = page_table.shape
    p_len = n_pages * PAGE_SIZE
    assert p_len % SLC_BLOCK == 0
    n_chunks = p_len // CMP_CHUNK
    rows = NSA_GROUP * t_new
    n_pad = sel16.shape[3]
    per_b = lambda a: pl.BlockSpec((1,) + a.shape[1:], lambda i, c, pt: (i,) + (0,) * (a.ndim - 1))
    gs = pltpu.PrefetchScalarGridSpec(
        num_scalar_prefetch=1,
        grid=(db, n_chunks),
        in_specs=[_page_spec(layer, 1, d) for d in range(SAMPLE_PAGES_PER_STEP)]
                 + [per_b(q16), per_b(sel16), per_b(k_new), per_b(v_new)],
        out_specs=pl.BlockSpec((1, NSA_KV_HEADS, rows, NSA_HEAD_DIM), lambda i, c, pt: (i, 0, 0, 0)),
        scratch_shapes=[pltpu.VMEM((NSA_KV_HEADS, rows, 1), F32),
                        pltpu.VMEM((NSA_KV_HEADS, rows, 1), F32),
                        pltpu.VMEM((NSA_KV_HEADS, rows, NSA_HEAD_DIM), F32)],
    )
    return pl.pallas_call(
        functools.partial(_slc_sample_kernel, t_new=t_new, p_len=p_len),
        grid_spec=gs,
        out_shape=jax.ShapeDtypeStruct((db, NSA_KV_HEADS, rows, NSA_HEAD_DIM), F32),
        compiler_params=_cparams("parallel", "arbitrary"),
        name="nsa_slc_sample",
    )(page_table, *([cache] * SAMPLE_PAGES_PER_STEP), q16, sel16, k_new, v_new)


def _win_sample_kernel(q_ref, wk_ref, wv_ref, kn_ref, vn_ref, o_ref, *, t_new):
    rows = NSA_GROUP * t_new
    wb = wk_ref.shape[1]
    n_new = kn_ref.shape[1]
    t1 = lax.broadcasted_iota(jnp.int32, (rows, wb), 0) % t_new
    c1 = lax.broadcasted_iota(jnp.int32, (rows, wb), 1)
    d1 = wb + t1 - c1
    mask1 = (d1 >= 0) & (d1 <= WINDOW)
    t2 = lax.broadcasted_iota(jnp.int32, (rows, n_new), 0) % t_new
    c2 = lax.broadcasted_iota(jnp.int32, (rows, n_new), 1)
    d2 = t2 - c2
    mask2 = (d2 >= 0) & (d2 <= WINDOW) & (c2 < t_new)
    for kv in range(NSA_KV_HEADS):
        lanes = slice(kv * NSA_HEAD_DIM, (kv + 1) * NSA_HEAD_DIM)
        slope = _slope_col(kv, rows, t_new)
        s1 = jnp.where(mask1, _dot_nt(q_ref[0, kv], wk_ref[0, :, lanes]) - slope * d1.astype(F32), NEG_INF)
        s2 = jnp.where(mask2, _dot_nt(q_ref[0, kv], kn_ref[0, :, lanes]) - slope * d2.astype(F32), NEG_INF)
        m = jnp.maximum(jnp.max(s1, axis=-1, keepdims=True), jnp.max(s2, axis=-1, keepdims=True))
        e1 = jnp.exp(s1 - m) * mask1.astype(F32)
        e2 = jnp.exp(s2 - m) * mask2.astype(F32)
        den = jnp.maximum(jnp.sum(e1, axis=-1, keepdims=True) + jnp.sum(e2, axis=-1, keepdims=True), 1e-30)
        o_ref[0, kv] = (_dot(e1.astype(BF16), wv_ref[0, :, lanes]) + _dot(e2.astype(BF16), vn_ref[0, :, lanes])) / den


def _win_sample(q16, wk, wv, k_new, v_new, t_new):
    db = q16.shape[0]
    rows = NSA_GROUP * t_new
    per_b = lambda a: pl.BlockSpec((1,) + a.shape[1:], lambda i: (i,) + (0,) * (a.ndim - 1))
    return pl.pallas_call(
        functools.partial(_win_sample_kernel, t_new=t_new),
        grid=(db,),
        in_specs=[per_b(q16), per_b(wk), per_b(wv), per_b(k_new), per_b(v_new)],
        out_specs=pl.BlockSpec((1, NSA_KV_HEADS, rows, NSA_HEAD_DIM), lambda i: (i, 0, 0, 0)),
        out_shape=jax.ShapeDtypeStruct((db, NSA_KV_HEADS, rows, NSA_HEAD_DIM), F32),
        compiler_params=_cparams("parallel"),
        name="nsa_win_sample",
    )(q16, wk, wv, k_new, v_new)


def _pad_rows(a, n):
    return jnp.pad(a, ((0, 0), (0, n - a.shape[1]), (0, 0)))


def _nsa_sample(x, cache, layer, win_l, page_table, weights):
    db, t, _ = x.shape
    w_main, w_gate, consts, w_out = weights
    xb = x.reshape(db * t, D_MODEL).astype(BF16)
    h = _mm(xb, w_main)
    hg = _mm(xb, w_gate)
    h3 = h.reshape(db, t, NSA_MAIN_DIM)
    q16 = (h3[..., :NSA_Q_DIM] * NSA_SCALE).astype(BF16).reshape(db, t, NSA_KV_HEADS, NSA_GROUP, NSA_HEAD_DIM)
    q16 = q16.transpose(0, 2, 3, 1, 4).reshape(db, NSA_KV_HEADS, NSA_GROUP * t, NSA_HEAD_DIM)
    kvn = h3[..., NSA_Q_DIM:].reshape(db, t, 6, NSA_KV_DIM)
    new = lambda slot: _pad_rows(kvn[:, :, slot].astype(BF16), LANES)
    cache = cache.reshape(cache.shape[:3] + (4 * NSA_KV_HEADS, NSA_HEAD_DIM))
    o_c, sel16 = _cmp_sample(cache, layer, page_table, q16, consts, t)
    o_s = _slc_sample(cache, layer, page_table, q16, sel16.astype(BF16), new(2), new(3), t)
    wb = win_l.shape[1]
    wkv = win_l.astype(BF16).reshape(db, wb, 2, NSA_KV_DIM)
    o_w = _win_sample(q16, wkv[:, :, 0], wkv[:, :, 1], new(4), new(5), t)
    unrow = lambda o: o.reshape(db, NSA_KV_HEADS, NSA_GROUP, t, NSA_HEAD_DIM).transpose(0, 3, 1, 2, 4).reshape(
        db * t, NSA_Q_DIM)
    o = _merge(unrow(o_c), unrow(o_s), unrow(o_w), hg)
    y = _mm(o, w_out).reshape(db, t, D_MODEL)
    kv = h3[..., NSA_Q_DIM:].reshape(db, t, 6, NSA_KV_HEADS, NSA_HEAD_DIM)
    wbuf = jnp.concatenate([win_l, kv[:, :, 4:]], axis=1)
    return y, kv[:, :, :4], wbuf[:, wbuf.shape[1] - wb:]


S5_SEQS = SUBLANES
S5_CB_GROUPS = 16
S5_CB_STATES = S5_CB_GROUPS * S5_STATE
S5_CB_CH = S5_CB_GROUPS * S5_GROUP_CH
S5_N_CB = S5_GROUPS // S5_CB_GROUPS
S5_MAX_STEPS = 128


def _s5_discretize(a_re, a_im, log_dt, b_re, b_im):
    dt = jnp.exp(log_dt.astype(F32))[:, None]
    mag = jnp.exp(a_re * dt)
    ab_re = mag * jnp.cos(a_im * dt)
    ab_im = mag * jnp.sin(a_im * dt)
    den = a_re * a_re + a_im * a_im
    nr = ab_re - 1.0
    f_re = (nr * a_re + ab_im * a_im) / den
    f_im = (ab_im * a_re - nr * a_im) / den
    bb_re = f_re[..., None] * b_re - f_im[..., None] * b_im
    bb_im = f_re[..., None] * b_im + f_im[..., None] * b_re
    return ab_re, ab_im, bb_re, bb_im


def _s5_weights(w_in, a_re, a_im, log_dt, b_re, b_im, c_re, c_im, d, w_glu):
    ab_re, ab_im, bb_re, bb_im = _s5_discretize(a_re, a_im, log_dt, b_re, b_im)
    eye = jnp.eye(S5_CB_GROUPS, dtype=F32)

    def bd_in(bb):
        x = bb.reshape(S5_N_CB, S5_CB_GROUPS, S5_STATE, S5_GROUP_CH)
        return jnp.einsum('ngpc,gh->ngchp', x, eye).reshape(S5_N_CB, S5_CB_CH, S5_CB_STATES)

    def bd_out(cc):
        x = cc.reshape(S5_N_CB, S5_CB_GROUPS, S5_GROUP_CH, S5_STATE)
        return jnp.einsum('ngcp,gh->ngphc', x, eye).reshape(S5_N_CB, S5_CB_STATES, S5_CB_CH)

    bmat = jnp.concatenate([bd_in(bb_re), bd_in(bb_im)], axis=2).astype(BF16)
    cmat = jnp.concatenate([bd_out(c_re.astype(F32)), -bd_out(c_im.astype(F32))], axis=1).astype(BF16)
    rep = lambda a: jnp.broadcast_to(a.reshape(S5_N_CB, 1, S5_CB_STATES), (S5_N_CB, S5_SEQS, S5_CB_STATES))
    return dict(w_in=w_in.astype(BF16), w_glu=w_glu.astype(BF16), bmat=bmat, cmat=cmat,
                ar=rep(ab_re), ai=rep(ab_im), ab_re=ab_re, ab_im=ab_im, d=d.reshape(1, D_MODEL).astype(F32))


def _s5_scan_kernel(u_ref, b_ref, c_ref, ar_ref, ai_ref, d_ref, h0r_ref, h0i_ref, *rest, steps, with_y):
    if with_y:
        y_ref, hr_ref, hi_ref, bu_ref = rest
    else:
        hr_ref, hi_ref, bu_ref = rest
    ns = S5_CB_STATES

    @pl.when(pl.program_id(1) == 0)
    def _():
        hr_ref[...] = h0r_ref[...]
        hi_ref[...] = h0i_ref[...]

    u = u_ref[...]
    bu_ref[...] = _dot(u.astype(BF16), b_ref[0])
    ar = ar_ref[0]
    ai = ai_ref[0]

    def step(s, carry):
        hr, hi = carry
        r0 = pl.multiple_of(s * S5_SEQS, S5_SEQS)
        br = bu_ref[pl.ds(r0, S5_SEQS), :ns]
        bi = bu_ref[pl.ds(r0, S5_SEQS), ns:]
        nr = ar * hr - ai * hi + br
        ni = ar * hi + ai * hr + bi
        if with_y:
            bu_ref[pl.ds(r0, S5_SEQS), :ns] = nr
            bu_ref[pl.ds(r0, S5_SEQS), ns:] = ni
        return nr, ni

    hr, hi = lax.fori_loop(0, steps, step, (hr_ref[0], hi_ref[0]))
    hr_ref[0] = hr
    hi_ref[0] = hi
    if with_y:
        y_ref[...] = _dot(bu_ref[...].astype(BF16), c_ref[0]) + d_ref[...] * u


def _s5_scan(u_rows, w, h0r, h0i, with_y=True):
    n_rows = u_rows.shape[0]
    n_steps = n_rows // S5_SEQS
    steps = min(S5_MAX_STEPS, n_steps)
    assert n_steps % steps == 0
    tr = steps * S5_SEQS
    cb_spec = lambda a: pl.BlockSpec((1,) + a.shape[1:], lambda cb, t: (cb,) + (0,) * (a.ndim - 1))
    y_spec = [pl.BlockSpec((tr, S5_CB_CH), lambda cb, t: (t, cb))] if with_y else []
    y_shape = [jax.ShapeDtypeStruct((n_rows, D_MODEL), F32)] if with_y else []
    return pl.pallas_call(
        functools.partial(_s5_scan_kernel, steps=steps, with_y=with_y),
        grid=(S5_N_CB, n_steps // steps),
        in_specs=[pl.BlockSpec((tr, S5_CB_CH), lambda cb, t: (t, cb)),
                  cb_spec(w['bmat']), cb_spec(w['cmat']), cb_spec(w['ar']), cb_spec(w['ai']),
                  pl.BlockSpec((1, S5_CB_CH), lambda cb, t: (0, cb)),
                  cb_spec(h0r), cb_spec(h0i)],
        out_specs=y_spec + [cb_spec(h0r), cb_spec(h0i)],
        out_shape=y_shape + [jax.ShapeDtypeStruct(h0r.shape, F32), jax.ShapeDtypeStruct(h0i.shape, F32)],
        scratch_shapes=[pltpu.VMEM((tr, 2 * S5_CB_STATES), F32)],
        compiler_params=_cparams("parallel", "arbitrary"),
        name="s5_scan",
    )(u_rows, w['bmat'], w['cmat'], w['ar'], w['ai'], w['d'], h0r, h0i)


def _state_to_blocks(h):
    return h.reshape(h.shape[0], S5_N_CB, S5_CB_STATES).transpose(1, 0, 2)


def _blocks_to_state(h):
    return h.transpose(1, 0, 2).reshape(h.shape[1], S5_GROUPS, S5_STATE)


def _cpow2(re, im, n):
    for _ in range(n):
        re, im = re * re - im * im, 2.0 * re * im
    return re, im


def _s5_prompt(x, w):
    b, t, _ = x.shape
    n_seg = S5_SEQS // b
    seg = t // n_seg
    assert n_seg * b == S5_SEQS and seg & (seg - 1) == 0
    xr = x.reshape(b, n_seg, seg, D_MODEL).transpose(2, 0, 1, 3).reshape(t * b, D_MODEL)
    u = _mm(xr.astype(BF16), w['w_in'])
    zero = jnp.zeros((S5_N_CB, S5_SEQS, S5_CB_STATES), F32)
    er, ei = _s5_scan(u, w, zero, zero, with_y=False)
    er = _blocks_to_state(er).reshape(b, n_seg, S5_GROUPS, S5_STATE)
    ei = _blocks_to_state(ei).reshape(b, n_seg, S5_GROUPS, S5_STATE)
    pr, pi = _cpow2(w['ab_re'], w['ab_im'], int(math.log2(seg)))
    sr = [jnp.zeros((b, S5_GROUPS, S5_STATE), F32)]
    si = [jnp.zeros((b, S5_GROUPS, S5_STATE), F32)]
    for k in range(n_seg - 1):
        sr.append(er[:, k] + pr * sr[k] - pi * si[k])
        si.append(ei[:, k] + pr * si[k] + pi * sr[k])
    h0r = _state_to_blocks(jnp.stack(sr, axis=1).reshape(S5_SEQS, S5_GROUPS, S5_STATE))
    h0i = _state_to_blocks(jnp.stack(si, axis=1).reshape(S5_SEQS, S5_GROUPS, S5_STATE))
    y, hr, hi = _s5_scan(u, w, h0r, h0i)
    hr = _blocks_to_state(hr).reshape(b, n_seg, S5_GROUPS, S5_STATE)[:, -1]
    hi = _blocks_to_state(hi).reshape(b, n_seg, S5_GROUPS, S5_STATE)[:, -1]
    unperm = lambda a: a.reshape(seg, b, n_seg, -1).transpose(1, 2, 0, 3).reshape(b * t, -1)
    z = _mm(unperm(_gelu_rows(y)), w['w_glu'])
    return z, jnp.stack([hr, hi], axis=1)


def _s5_sample(x, h0, w):
    db, t, _ = x.shape
    assert db == S5_SEQS
    xr = x.transpose(1, 0, 2).reshape(t * db, D_MODEL)
    u = _mm(xr.astype(BF16), w['w_in'])
    y, hr, hi = _s5_scan(u, w, _state_to_blocks(h0[:, 0].astype(F32)), _state_to_blocks(h0[:, 1].astype(F32)))
    z = _mm(_gelu_rows(y), w['w_glu'])
    z = z.reshape(t, db, -1).transpose(1, 0, 2).reshape(db * t, -1)
    return z, jnp.stack([_blocks_to_state(hr), _blocks_to_state(hi)], axis=1).astype(h0.dtype)


def _gelu_kernel(y_ref, o_ref):
    o_ref[...] = _gelu_tanh(y_ref[...]).astype(o_ref.dtype)


def _gelu_rows(y):
    n, d = y.shape
    tm = min(256, n)
    row = pl.BlockSpec((tm, d), lambda i: (i, 0))
    return pl.pallas_call(_gelu_kernel, grid=(n // tm,), in_specs=[row], out_specs=row,
                          out_shape=jax.ShapeDtypeStruct((n, d), BF16),
                          compiler_params=_cparams("parallel"), name="gelu")(y)


ROUTER_TN = 256
N_RANKED = PEER_TOPK
_CAND_PAIRS = [(i, j) for i in range(N_RANKED) for j in range(N_RANKED) if (i + 1) * (j + 1) <= N_RANKED]


UNRANKED = 127.0


def _top_values(s, n, with_rank=False):
    vals = []
    cur = s
    rank = jnp.full(s.shape, UNRANKED, F32)
    for t in range(n):
        m = jnp.max(cur, axis=0, keepdims=True)
        vals.append(m)
        hit = cur == m
        if with_rank:
            rank = jnp.where(hit, float(t), rank)
        cur = jnp.where(hit, -jnp.inf, cur)
    return (vals, rank) if with_rank else vals


def _router_kernel(x_ref, wq_ref, keys_ref, w1z_ref, cnt_ref, rank_ref, w2_ref):
    q = _dot(x_ref[...].astype(BF16), wq_ref[...])
    for h in range(PEER_HEADS):
        s = [_dot_nt(keys_ref[2 * h + c], q[:, (2 * h + c) * PEER_HALF:(2 * h + c + 1) * PEER_HALF].astype(BF16))
             for c in range(2)]
        top1 = _top_values(s[0], N_RANKED)
        top2, rank2 = _top_values(s[1], N_RANKED, with_rank=True)
        cand = jnp.concatenate([top1[i] + top2[j] for i, j in _CAND_PAIRS], axis=0)
        tau = _top_values(cand, PEER_TOPK)[PEER_TOPK - 1]
        m1, m2 = top1[0], top2[0]
        z = jnp.sum(jnp.where(cand >= tau, jnp.exp(cand - (m1 + m2)), 0.0), axis=0, keepdims=True)
        cnt = jnp.zeros(s[0].shape, F32)
        for j in range(N_RANKED):
            cnt = cnt + jnp.where(s[0] + top2[j] >= tau, 1.0, 0.0)
        w1z_ref[h] = jnp.exp(s[0] - m1) / z
        cnt_ref[h] = cnt
        rank_ref[h] = rank2.astype(rank_ref.dtype)
        w2_ref[h] = jnp.exp(s[1] - m2).astype(w2_ref.dtype)


def _router(x, wq, keys):
    n = x.shape[0]
    tn = min(ROUTER_TN, n)
    assert n % tn == 0
    shape = (PEER_HEADS, PEER_N_KEYS, n)
    ospec = pl.BlockSpec((PEER_HEADS, PEER_N_KEYS, tn), lambda i: (0, 0, i))
    return pl.pallas_call(
        _router_kernel, grid=(n // tn,),
        in_specs=[pl.BlockSpec((tn, D_MODEL), lambda i: (i, 0)),
                  pl.BlockSpec(wq.shape, lambda i: (0, 0)),
                  pl.BlockSpec(keys.shape, lambda i: (0, 0, 0))],
        out_specs=[ospec] * 4,
        out_shape=[jax.ShapeDtypeStruct(shape, F32), jax.ShapeDtypeStruct(shape, F32),
                   jax.ShapeDtypeStruct(shape, BF16), jax.ShapeDtypeStruct(shape, BF16)],
        compiler_params=_cparams("parallel"), name="peer_router",
    )(x, wq, keys)


EXPERT_TN = 512
EXPERT_TE = 1024
EXPERT_NC = EXPERT_TE // PEER_N_KEYS
EXPERT_NT = PEER_N_EXPERTS // EXPERT_TE


def _expert_gate(w1z_ref, cnt_ref, rank_ref, w2_ref, c):
    g = None
    for h in range(PEER_HEADS):
        w2 = w2_ref[h]
        w1 = w1z_ref[h, pl.ds(c, 1), :].astype(BF16)
        cnt = cnt_ref[h, pl.ds(c, 1), :].astype(BF16)
        term = w1 * jnp.where(rank_ref[h] < cnt, w2, jnp.zeros_like(w2))
        g = term if g is None else g + term
    return g


def _expert_kernel(xt_ref, u_ref, v_ref, w1z_ref, cnt_ref, rank_ref, w2_ref, o_ref):
    j = pl.program_id(1)

    @pl.when(j == 0)
    def _():
        o_ref[...] = jnp.zeros_like(o_ref)

    ht = _dot(u_ref[...], xt_ref[...])
    parts = []
    for cc in range(EXPERT_NC):
        g = _expert_gate(w1z_ref, cnt_ref, rank_ref, w2_ref, j * EXPERT_NC + cc)
        parts.append(_gelu_tanh(ht[cc * PEER_N_KEYS:(cc + 1) * PEER_N_KEYS].astype(BF16)) * g)
    o_ref[...] += lax.dot_general(v_ref[...], jnp.concatenate(parts, axis=0), (((0,), (0,)), ((), ())),
                                  preferred_element_type=F32)


def _expert(xt, u, v, layer, w1z, cnt, rank, w2):
    n = xt.shape[1]
    tn = min(EXPERT_TN, n)
    assert n % tn == 0
    rspec = pl.BlockSpec((PEER_HEADS, PEER_N_KEYS, tn), lambda i, j: (0, 0, i))
    wspec = pl.BlockSpec((None, EXPERT_TE, D_MODEL), lambda i, j: (layer, j, 0))
    return pl.pallas_call(
        _expert_kernel, grid=(n // tn, EXPERT_NT),
        in_specs=[pl.BlockSpec((D_MODEL, tn), lambda i, j: (0, i)), wspec, wspec, rspec, rspec, rspec, rspec],
        out_specs=pl.BlockSpec((D_MODEL, tn), lambda i, j: (0, i)),
        out_shape=jax.ShapeDtypeStruct((D_MODEL, n), F32),
        compiler_params=_cparams("parallel", "arbitrary"), name="peer_expert",
    )(xt, u, v, w1z, cnt, rank, w2)


def _peer_tables(peer_u, peer_v):
    return peer_u.astype(BF16), peer_v.astype(BF16)


def _peer_t(x, xt, w_q, sub_keys, tables, layer):
    u, v = tables
    keys = sub_keys.reshape(2 * PEER_HEADS, PEER_N_KEYS, PEER_HALF).astype(BF16)
    w1z, cnt, rank, w2 = _router(x, w_q.astype(BF16), keys)
    return _expert(xt, u, v, layer, w1z, cnt, rank, w2)


def _peer(x, w_q, sub_keys, tables, layer):
    n = x.shape[0]
    xp = jnp.pad(x, ((0, -(-n // LANES) * LANES - n), (0, 0)))
    return _peer_t(xp, xp.T.astype(BF16), w_q, sub_keys, tables, layer).T[:n]


def kernel(x_prompt, x_sample, cache_nsa, state_win, state_s5, page_table, nsa_w_in, nsa_cmp_pool, nsa_cmp_pe,
           nsa_cmp_phi, nsa_w_out, s5_w_in, s5_a_re, s5_a_im, s5_log_dt, s5_b_re, s5_b_im, s5_c_re, s5_c_im, s5_d,
           s5_w_glu, peer_w_q, peer_sub_keys, peer_u, peer_v, ln_g, ln_b):
    b, s, _ = x_prompt.shape
    db, t, _ = x_sample.shape
    xp = x_prompt.reshape(b * s, D_MODEL)
    xs = x_sample.reshape(db * t, D_MODEL)
    rows_p, rows_s, win_p, win_s, s5_p, s5_s = [], [], [], [], [], []
    tables = _peer_tables(peer_u, peer_v)
    for layer in range(DEPTH):
        j = layer // N_MIXERS
        if layer % N_MIXERS == 0:
            w = _nsa_weights(nsa_w_in[j], nsa_cmp_pool[j], nsa_cmp_pe[j], nsa_cmp_phi[j], nsa_w_out[j])
            mp, rp, wp = _nsa_prompt(xp.reshape(b, s, D_MODEL), w)
            ms, rs, ws = _nsa_sample(xs.reshape(db, t, D_MODEL), cache_nsa, j, state_win[j], page_table, w)
            rows_p.append(rp)
            rows_s.append(rs)
            win_p.append(wp)
            win_s.append(ws)
            xp, xpt = _ln_res(xp, mp.reshape(b * s, D_MODEL), ln_g[layer, 0], ln_b[layer, 0], with_t=True)
            xs = _ln_res(xs, ms.reshape(db * t, D_MODEL), ln_g[layer, 0], ln_b[layer, 0])
        else:
            w = _s5_weights(s5_w_in[j], s5_a_re[j], s5_a_im[j], s5_log_dt[j], s5_b_re[j], s5_b_im[j],
                            s5_c_re[j], s5_c_im[j], s5_d[j], s5_w_glu[j])
            zp, hp = _s5_prompt(xp.reshape(b, s, D_MODEL), w)
            zs, hs = _s5_sample(xs.reshape(db, t, D_MODEL), state_s5[j], w)
            s5_p.append(hp.astype(state_s5.dtype))
            s5_s.append(hs)
            xp, xpt = _ln_res(xp, zp, ln_g[layer, 0], ln_b[layer, 0], mixer='glu', with_t=True)
            xs = _ln_res(xs, zs, ln_g[layer, 0], ln_b[layer, 0], mixer='glu')
        xp = _ln_res(xp, _peer_t(xp, xpt, peer_w_q[layer], peer_sub_keys[layer], tables, layer),
                     ln_g[layer, 1], ln_b[layer, 1], mixer='cols')
        xs = _ln_res(xs, _peer(xs, peer_w_q[layer], peer_sub_keys[layer], tables, layer),
                     ln_g[layer, 1], ln_b[layer, 1])
    return (xp.reshape(b, s, D_MODEL), xs.reshape(db, t, D_MODEL), jnp.stack(rows_p), jnp.stack(rows_s),
            jnp.stack(win_p), jnp.stack(win_s), jnp.stack(s5_p), jnp.stack(s5_s))
```

```python
import functools
import math

import numpy as np
import jax
import jax.numpy as jnp
from jax import lax
from jax.experimental import pallas as pl
from jax.experimental.pallas import tpu as pltpu

F32 = jnp.float32
BF16 = jnp.bfloat16

D_MODEL = 2048
DEPTH = 4
PAGE_SIZE = 128
N_MIXERS = 2
NSA_HEADS = 16
NSA_KV_HEADS = 4
NSA_GROUP = NSA_HEADS // NSA_KV_HEADS
NSA_HEAD_DIM = D_MODEL // NSA_HEADS
NSA_Q_DIM = NSA_HEADS * NSA_HEAD_DIM
NSA_KV_DIM = NSA_KV_HEADS * NSA_HEAD_DIM
NSA_MAIN_DIM = NSA_Q_DIM + 6 * NSA_KV_DIM
NSA_N_GATES = 3 * NSA_HEADS
NSA_SCALE = NSA_HEAD_DIM ** -0.5
CMP_BLOCK = 32
CMP_STRIDE = 16
SLC_BLOCK = 64
SLC_TOP_N = 16
WINDOW = 512
FORCE_BONUS = 1.0e4
S5_GROUP_CH = 16
S5_GROUPS = D_MODEL // S5_GROUP_CH
S5_STATE = 64
PEER_HEADS = 8
PEER_N_KEYS = 128
PEER_N_EXPERTS = PEER_N_KEYS ** 2
PEER_HALF = 128
PEER_TOPK = 16
LN_EPS = 1e-5
NEG_INF = -1.0e30
ALPHA = (2.0 * DEPTH) ** 0.25

LANES = 128
SUBLANES = 8
VMEM_LIMIT_BYTES = 56 * 1024 * 1024

CMP_CHUNK = 2048
CMP_ROWS = CMP_CHUNK // CMP_STRIDE
SAMPLE_PAGES_PER_STEP = CMP_CHUNK // PAGE_SIZE

_SLOPES = [2.0 ** (-8.0 * (h + 1) / NSA_HEADS) for h in range(NSA_HEADS)]


def _cparams(*sem):
    return pltpu.CompilerParams(dimension_semantics=sem, vmem_limit_bytes=VMEM_LIMIT_BYTES)


def _dot(a, b):
    return jnp.dot(a, b, preferred_element_type=F32)


def _dot_nt(a, b):
    return lax.dot_general(a, b, (((1,), (1,)), ((), ())), preferred_element_type=F32)


def _split3(x):
    hi = x.astype(BF16)
    r1 = x - hi.astype(F32)
    mid = r1.astype(BF16)
    lo = (r1 - mid.astype(F32)).astype(BF16)
    return hi, mid, lo


def _gelu_tanh(x):
    c = math.sqrt(2.0 / math.pi)
    return 0.5 * x * (1.0 + jnp.tanh(c * (x + 0.044715 * (x * x * x))))


def _mm_kernel(a_ref, b_ref, o_ref):
    o_ref[...] = _dot(a_ref[...], b_ref[...]).astype(o_ref.dtype)


def _mm(a, b, tm=512, tn=1024, out_dtype=F32):
    m, k = a.shape
    n = b.shape[1]
    tm = min(tm, m)
    tn = min(tn, n)
    assert m % tm == 0 and n % tn == 0
    return pl.pallas_call(
        _mm_kernel,
        grid=(n // tn, m // tm),
        in_specs=[pl.BlockSpec((tm, k), lambda j, i: (i, 0)),
                  pl.BlockSpec((k, tn), lambda j, i: (0, j))],
        out_specs=pl.BlockSpec((tm, tn), lambda j, i: (i, j)),
        out_shape=jax.ShapeDtypeStruct((m, n), out_dtype),
        compiler_params=_cparams("parallel", "parallel"),
        name="mm",
    )(a, b)


def _ln_kernel(x_ref, *refs, mixer, with_t):
    n_in = 2 if mixer == 'glu' else 1
    g_ref, b_ref = refs[n_in:n_in + 2]
    outs = refs[n_in + 2:]
    if mixer == 'glu':
        mix = refs[0][...] * (1.0 / (1.0 + jnp.exp(-refs[1][...])))
    elif mixer == 'cols':
        mix = refs[0][...].T
    else:
        mix = refs[0][...]
    y = ALPHA * x_ref[...] + mix
    mu = jnp.mean(y, axis=-1, keepdims=True)
    yc = y - mu
    var = jnp.mean(yc * yc, axis=-1, keepdims=True)
    out = yc * lax.rsqrt(var + LN_EPS) * g_ref[...] + b_ref[...]
    outs[0][...] = out
    if with_t:
        outs[1][...] = out.T.astype(outs[1].dtype)


def _ln_res(x, m, g, b, mixer='rows', with_t=False):
    n, d = x.shape
    tm = min(256, n)
    assert n % tm == 0
    row = pl.BlockSpec((tm, d), lambda i: (i, 0))
    col = pl.BlockSpec((d, tm), lambda i: (0, i))
    vec = pl.BlockSpec((1, d), lambda i: (0, 0))
    if mixer == 'glu':
        m_specs, m_args = [row, pl.BlockSpec((tm, d), lambda i: (i, 1))], (m, m)
    elif mixer == 'cols':
        m_specs, m_args = [col], (m,)
    else:
        m_specs, m_args = [row], (m,)
    out_specs = [row] + ([col] if with_t else [])
    out_shape = [jax.ShapeDtypeStruct((n, d), F32)] + ([jax.ShapeDtypeStruct((d, n), BF16)] if with_t else [])
    res = pl.pallas_call(
        functools.partial(_ln_kernel, mixer=mixer, with_t=with_t),
        grid=(n // tm,), in_specs=[row] + m_specs + [vec, vec], out_specs=out_specs, out_shape=out_shape,
        compiler_params=_cparams("parallel"), name="ln_res",
    )(x, *m_args, g.reshape(1, d), b.reshape(1, d))
    return res if with_t else res[0]


def _compress_consts(pool, pe, phi):
    eye_cur = jnp.eye(CMP_ROWS, dtype=F32)
    eye_prev = jnp.eye(CMP_ROWS, k=-1, dtype=F32)
    lo = pool[:, :, None, None, :CMP_STRIDE]
    hi = pool[:, :, None, None, CMP_STRIDE:]
    pb = (eye_prev[None, None, :, :, None] * lo + eye_cur[None, None, :, :, None] * hi).reshape(
        2, NSA_KV_HEADS, CMP_ROWS, CMP_CHUNK)
    first = jnp.asarray((np.arange(CMP_ROWS) == 0).astype(np.float32))[None, None, :, None]
    pbt = first * pool[:, :, None, :CMP_STRIDE]
    pe_term = jnp.einsum('skj,skjd->skd', pool, pe)
    return (pb.astype(BF16), pbt.astype(BF16), pe_term.reshape(2 * NSA_KV_HEADS, NSA_HEAD_DIM).astype(F32),
            phi.astype(BF16))


def _compress_chunk(x, tail, pb_ref, pbt_ref, pe_ref, phi_ref):
    outs = []
    for slot in range(2):
        for kv in range(NSA_KV_HEADS):
            lo = slot * NSA_KV_DIM + kv * NSA_HEAD_DIM
            xs = x[:, lo:lo + NSA_HEAD_DIM]
            pooled = _dot(pb_ref[slot, kv], xs) + _dot(pbt_ref[slot, kv], tail[:, lo:lo + NSA_HEAD_DIM])
            pooled = pooled + pe_ref[pl.ds(slot * NSA_KV_HEADS + kv, 1), :]
            outs.append(_dot(pooled.astype(BF16), phi_ref[slot, kv]))
    return jnp.concatenate(outs, axis=1)


def _slope_col(kv, rows, per):
    gi = lax.broadcasted_iota(jnp.int32, (rows, 1), 0) // per
    col = jnp.full((rows, 1), _SLOPES[kv * NSA_GROUP + NSA_GROUP - 1], F32)
    for g in range(NSA_GROUP - 1):
        col = jnp.where(gi == g, _SLOPES[kv * NSA_GROUP + g], col)
    return col


def _select_top(score, axis, n_entries):
    idx = lax.broadcasted_iota(jnp.int32, score.shape, axis).astype(F32)
    sel = jnp.zeros(score.shape, F32)
    cur = score
    for _ in range(SLC_TOP_N):
        m = jnp.max(cur, axis=axis, keepdims=True)
        first = jnp.min(jnp.where(cur == m, idx, float(n_entries)), axis=axis, keepdims=True)
        hit = idx == first
        sel = jnp.where(hit, 1.0, sel)
        cur = jnp.where(hit, -2.0, cur)
    return sel


def _compress_kernel(x_ref, pb_ref, pbt_ref, pe_ref, phi_ref, o_ref, tail_ref):
    @pl.when(pl.program_id(1) == 0)
    def _():
        tail_ref[...] = jnp.zeros_like(tail_ref)

    x = x_ref[0]
    o_ref[0] = _compress_chunk(x, tail_ref[...], pb_ref, pbt_ref, pe_ref, phi_ref).astype(o_ref.dtype)
    tail_ref[...] = x[CMP_CHUNK - CMP_STRIDE:, :]


def _compress(kvb, consts):
    b, s, _ = kvb.shape
    w = 2 * NSA_KV_DIM
    n_chunks = s // CMP_CHUNK
    pb, pbt, pe_term, phi = consts
    full = lambda a: pl.BlockSpec(a.shape, lambda i, c: (0,) * a.ndim)
    return pl.pallas_call(
        _compress_kernel,
        grid=(b, n_chunks),
        in_specs=[pl.BlockSpec((1, CMP_CHUNK, w), lambda i, c: (i, c, 0)),
                  full(pb), full(pbt), full(pe_term), full(phi)],
        out_specs=pl.BlockSpec((1, CMP_ROWS, w), lambda i, c: (i, c, 0)),
        out_shape=jax.ShapeDtypeStruct((b, s // CMP_STRIDE, w), BF16),
        scratch_shapes=[pltpu.VMEM((CMP_STRIDE, w), BF16)],
        compiler_params=_cparams("parallel", "arbitrary"),
        name="nsa_compress",
    )(kvb, pb, pbt, pe_term, phi)


CMP_TQ = 128


def _cmp_prompt_kernel(q_ref, ckv_ref, mt_ref, oc_ref, sel_ref):
    tq = CMP_TQ
    n_r = ckv_ref.shape[1]
    n_slc = mt_ref.shape[0]
    q0 = pl.program_id(1) * tq
    qpos = q0 + lax.broadcasted_iota(jnp.int32, (tq, n_r), 0)
    r = lax.broadcasted_iota(jnp.int32, (tq, n_r), 1)
    dist_i = qpos - (CMP_STRIDE * r + CMP_STRIDE - 1)
    mask = (dist_i >= 0) & (r >= 1)
    maskf = mask.astype(F32)
    dist = dist_i.astype(F32)
    q = q_ref[0]
    ckv = ckv_ref[0]
    blk = lax.broadcasted_iota(jnp.int32, (n_slc, tq), 0)
    cur = (q0 + lax.broadcasted_iota(jnp.int32, (n_slc, tq), 1)) // SLC_BLOCK
    avail = blk <= cur
    forced = (blk == 0) | (blk == cur) | (blk == cur - 1)
    for kv in range(NSA_KV_HEADS):
        ck = ckv[:, kv * NSA_HEAD_DIM:(kv + 1) * NSA_HEAD_DIM]
        cv = ckv[:, NSA_KV_DIM + kv * NSA_HEAD_DIM:NSA_KV_DIM + (kv + 1) * NSA_HEAD_DIM]
        qs = jnp.concatenate([q[:, (kv * NSA_GROUP + g) * NSA_HEAD_DIM:(kv * NSA_GROUP + g + 1) * NSA_HEAD_DIM]
                              for g in range(NSA_GROUP)], axis=0)
        s_all = _dot_nt(qs, ck)
        p_sum = jnp.zeros((tq, n_r), F32)
        for g in range(NSA_GROUP):
            h = kv * NSA_GROUP + g
            s = jnp.where(mask, s_all[g * tq:(g + 1) * tq] - _SLOPES[h] * dist, NEG_INF)
            e = jnp.exp(s - jnp.max(s, axis=-1, keepdims=True)) * maskf
            p = e / jnp.maximum(jnp.sum(e, axis=-1, keepdims=True), 1e-30)
            oc_ref[0, :, h * NSA_HEAD_DIM:(h + 1) * NSA_HEAD_DIM] = _dot(p.astype(BF16), cv)
            p_sum = p_sum + p
        mt = mt_ref[...]
        p_slc = sum(_dot_nt(mt, part) for part in _split3(p_sum))
        score = jnp.where(avail, p_slc + jnp.where(forced, FORCE_BONUS, 0.0), -1.0)
        sel_ref[0, kv] = _select_top(score, 0, n_slc)


def _slc_map(n_slc, n_r, n_pad):
    m = np.arange(n_pad)[:, None]
    r = np.arange(n_r)[None, :]
    ratio = SLC_BLOCK // CMP_STRIDE
    return jnp.asarray(((r >= ratio * m) & (r <= ratio * m + ratio) & (m < n_slc)).astype(np.float32), BF16)


def _cmp_prompt(qs, ckv):
    b, s, _ = qs.shape
    n_r = ckv.shape[1]
    n_slc = s // SLC_BLOCK
    mt = _slc_map(n_slc, n_r, n_slc)
    return pl.pallas_call(
        _cmp_prompt_kernel,
        grid=(b, s // CMP_TQ),
        in_specs=[pl.BlockSpec((1, CMP_TQ, NSA_Q_DIM), lambda i, t: (i, t, 0)),
                  pl.BlockSpec((1, n_r, 2 * NSA_KV_DIM), lambda i, t: (i, 0, 0)),
                  pl.BlockSpec(mt.shape, lambda i, t: (0, 0))],
        out_specs=[pl.BlockSpec((1, CMP_TQ, NSA_Q_DIM), lambda i, t: (i, t, 0)),
                   pl.BlockSpec((1, NSA_KV_HEADS, n_slc, CMP_TQ), lambda i, t: (i, 0, 0, t))],
        out_shape=[jax.ShapeDtypeStruct((b, s, NSA_Q_DIM), F32),
                   jax.ShapeDtypeStruct((b, NSA_KV_HEADS, n_slc, s), F32)],
        compiler_params=_cparams("parallel", "parallel"),
        name="nsa_cmp_prompt",
    )(qs, ckv, mt)


SLC_TQ = 256
SLC_TK = 1024


def _slc_prompt_kernel(q_ref, k_ref, v_ref, sel_ref, o_ref, m_ref, l_ref, acc_ref):
    tq, tk = SLC_TQ, SLC_TK
    qi = pl.program_id(1)
    kj = pl.program_id(2)
    n_sel = sel_ref.shape[3]

    @pl.when(kj == 0)
    def _():
        m_ref[...] = jnp.full_like(m_ref, NEG_INF)
        l_ref[...] = jnp.zeros_like(l_ref)
        acc_ref[...] = jnp.zeros_like(acc_ref)

    @pl.when(kj * tk <= qi * tq + tq - 1)
    def _():
        q = q_ref[0]
        kt = k_ref[0]
        vt = v_ref[0]
        qpos = qi * tq + lax.broadcasted_iota(jnp.int32, (tq, tk), 0)
        kpos = kj * tk + lax.broadcasted_iota(jnp.int32, (tq, tk), 1)
        dist_i = qpos - kpos
        krel = (kj * tk - qi * tq + lax.broadcasted_iota(jnp.int32, (1, tk), 1)).astype(F32)
        eb = lax.broadcasted_iota(jnp.int32, (n_sel, tk), 0)
        ec = (kj * tk + lax.broadcasted_iota(jnp.int32, (n_sel, tk), 1)) // SLC_BLOCK
        expand = jnp.where(eb == ec, 1.0, 0.0).astype(BF16)
        for kv in range(NSA_KV_HEADS):
            picked = _dot(sel_ref[0, kv], expand)
            mbias = jnp.where((picked > 0.5) & (dist_i >= 0), 0.0, NEG_INF)
            qs = jnp.concatenate([q[:, (kv * NSA_GROUP + g) * NSA_HEAD_DIM:(kv * NSA_GROUP + g + 1) * NSA_HEAD_DIM]
                                  for g in range(NSA_GROUP)], axis=0)
            s_all = _dot_nt(qs, kt[:, kv * NSA_HEAD_DIM:(kv + 1) * NSA_HEAD_DIM])
            vv = vt[:, kv * NSA_HEAD_DIM:(kv + 1) * NSA_HEAD_DIM]
            for g in range(NSA_GROUP):
                h = kv * NSA_GROUP + g
                rows = slice(g * tq, (g + 1) * tq)
                s = s_all[rows] + (mbias + _SLOPES[h] * krel)
                m_old = m_ref[kv, rows]
                m_new = jnp.maximum(m_old, jnp.max(s, axis=-1, keepdims=True))
                a = jnp.exp(m_old - m_new)
                e = jnp.exp(s - m_new)
                l_ref[kv, rows] = a * l_ref[kv, rows] + jnp.sum(e, axis=-1, keepdims=True)
                acc_ref[kv, rows] = a * acc_ref[kv, rows] + _dot(e.astype(BF16), vv)
                m_ref[kv, rows] = m_new

    @pl.when(kj == pl.num_programs(2) - 1)
    def _():
        for kv in range(NSA_KV_HEADS):
            for g in range(NSA_GROUP):
                h = kv * NSA_GROUP + g
                rows = slice(g * tq, (g + 1) * tq)
                o_ref[0, :, h * NSA_HEAD_DIM:(h + 1) * NSA_HEAD_DIM] = (
                    acc_ref[kv, rows] / jnp.maximum(l_ref[kv, rows], 1e-30))


def _slc_prompt(qs, kvb, sel):
    b, s, _ = qs.shape
    tq, tk = min(SLC_TQ, s), min(SLC_TK, s)
    assert tq == SLC_TQ and tk == SLC_TK
    n_sel = sel.shape[3]
    last = lambda t: (t * tq + tq - 1) // tk
    return pl.pallas_call(
        _slc_prompt_kernel,
        grid=(b, s // tq, s // tk),
        in_specs=[pl.BlockSpec((1, tq, NSA_Q_DIM), lambda i, t, j: (i, t, 0)),
                  pl.BlockSpec((1, tk, NSA_KV_DIM), lambda i, t, j: (i, jnp.minimum(j, last(t)), 2)),
                  pl.BlockSpec((1, tk, NSA_KV_DIM), lambda i, t, j: (i, jnp.minimum(j, last(t)), 3)),
                  pl.BlockSpec((1, NSA_KV_HEADS, tq, n_sel), lambda i, t, j: (i, 0, t, 0))],
        out_specs=pl.BlockSpec((1, tq, NSA_Q_DIM), lambda i, t, j: (i, t, 0)),
        out_shape=jax.ShapeDtypeStruct((b, s, NSA_Q_DIM), F32),
        scratch_shapes=[pltpu.VMEM((NSA_KV_HEADS, NSA_GROUP * tq, 1), F32),
                        pltpu.VMEM((NSA_KV_HEADS, NSA_GROUP * tq, 1), F32),
                        pltpu.VMEM((NSA_KV_HEADS, NSA_GROUP * tq, NSA_HEAD_DIM), F32)],
        compiler_params=_cparams("parallel", "parallel", "arbitrary"),
        name="nsa_slc_prompt",
    )(qs, kvb, kvb, sel)


WIN_TQ = 256
WIN_NT = WINDOW // WIN_TQ + 1


def _win_prompt_kernel(q_ref, *refs):
    tq = WIN_TQ
    k_refs = refs[:WIN_NT]
    v_refs = refs[WIN_NT:2 * WIN_NT]
    o_ref = refs[2 * WIN_NT]
    qi = pl.program_id(1)
    nk = WIN_NT * tq
    row = lax.broadcasted_iota(jnp.int32, (tq, nk), 0)
    col = lax.broadcasted_iota(jnp.int32, (tq, nk), 1)
    dist_i = WINDOW + row - col
    kpos = (qi - (WIN_NT - 1)) * tq + col
    mbias = jnp.where((dist_i >= 0) & (dist_i <= WINDOW) & (kpos >= 0), 0.0, NEG_INF)
    crel = (lax.broadcasted_iota(jnp.int32, (1, nk), 1) - WINDOW).astype(F32)
    q = q_ref[0]
    kcat = jnp.concatenate([r[0] for r in k_refs], axis=0)
    vcat = jnp.concatenate([r[0] for r in v_refs], axis=0)
    for kv in range(NSA_KV_HEADS):
        qs = jnp.concatenate([q[:, (kv * NSA_GROUP + g) * NSA_HEAD_DIM:(kv * NSA_GROUP + g + 1) * NSA_HEAD_DIM]
                              for g in range(NSA_GROUP)], axis=0)
        s_all = _dot_nt(qs, kcat[:, kv * NSA_HEAD_DIM:(kv + 1) * NSA_HEAD_DIM])
        vv = vcat[:, kv * NSA_HEAD_DIM:(kv + 1) * NSA_HEAD_DIM]
        for g in range(NSA_GROUP):
            h = kv * NSA_GROUP + g
            s = s_all[g * tq:(g + 1) * tq] + (mbias + _SLOPES[h] * crel)
            e = jnp.exp(s - jnp.max(s, axis=-1, keepdims=True))
            den = jnp.maximum(jnp.sum(e, axis=-1, keepdims=True), 1e-30)
            o_ref[0, :, h * NSA_HEAD_DIM:(h + 1) * NSA_HEAD_DIM] = _dot(e.astype(BF16), vv) / den


def _win_prompt(qs, kvb):
    b, s, _ = qs.shape
    tq = WIN_TQ
    kspec = lambda d, col: pl.BlockSpec((1, tq, NSA_KV_DIM),
                                        lambda i, t: (i, jnp.maximum(t - (WIN_NT - 1) + d, 0), col))
    return pl.pallas_call(
        _win_prompt_kernel,
        grid=(b, s // tq),
        in_specs=[pl.BlockSpec((1, tq, NSA_Q_DIM), lambda i, t: (i, t, 0))]
                 + [kspec(d, 4) for d in range(WIN_NT)] + [kspec(d, 5) for d in range(WIN_NT)],
        out_specs=pl.BlockSpec((1, tq, NSA_Q_DIM), lambda i, t: (i, t, 0)),
        out_shape=jax.ShapeDtypeStruct((b, s, NSA_Q_DIM), F32),
        compiler_params=_cparams("parallel", "parallel"),
        name="nsa_win_prompt",
    )(qs, *([kvb] * (2 * WIN_NT)))


def _merge_kernel(oc_ref, os_ref, ow_ref, hg_ref, ex_ref, o_ref):
    gate = 1.0 / (1.0 + jnp.exp(-hg_ref[...]))
    parts = _split3(gate)[:2]
    out = None
    for br, ref in enumerate((oc_ref, os_ref, ow_ref)):
        gx = sum(_dot(p, ex_ref[br]) for p in parts)
        term = gx * ref[...]
        out = term if out is None else out + term
    o_ref[...] = out.astype(o_ref.dtype)


def _gate_expand():
    e = np.zeros((3, LANES, NSA_Q_DIM), np.float32)
    for br in range(3):
        for h in range(NSA_HEADS):
            e[br, h * 3 + br, h * NSA_HEAD_DIM:(h + 1) * NSA_HEAD_DIM] = 1.0
    return jnp.asarray(e, BF16)


def _merge(oc, os_, ow, hg):
    n = oc.shape[0]
    tm = min(256, n)
    ex = _gate_expand()
    row = pl.BlockSpec((tm, NSA_Q_DIM), lambda i: (i, 0))
    return pl.pallas_call(
        _merge_kernel, grid=(n // tm,),
        in_specs=[row, row, row, pl.BlockSpec((tm, LANES), lambda i: (i, 0)),
                  pl.BlockSpec(ex.shape, lambda i: (0, 0, 0))],
        out_specs=row,
        out_shape=jax.ShapeDtypeStruct((n, NSA_Q_DIM), BF16),
        compiler_params=_cparams("parallel"), name="nsa_merge",
    )(oc, os_, ow, hg, ex)


def _nsa_weights(w_in, pool, pe, phi, w_out):
    w_main = w_in[:, :NSA_MAIN_DIM].astype(BF16)
    w_gate = jnp.pad(w_in[:, NSA_MAIN_DIM:], ((0, 0), (0, LANES - NSA_N_GATES))).astype(BF16)
    return w_main, w_gate, _compress_consts(pool, pe, phi), w_out.astype(BF16)


def _nsa_prompt(x, weights):
    b, s, _ = x.shape
    w_main, w_gate, consts, w_out = weights
    xb = x.reshape(b * s, D_MODEL).astype(BF16)
    h = _mm(xb, w_main)
    hg = _mm(xb, w_gate)
    h3 = h.reshape(b, s, NSA_MAIN_DIM)
    qs = (h3[..., :NSA_Q_DIM] * NSA_SCALE).astype(BF16)
    kvb = h3[..., NSA_Q_DIM:].astype(BF16)
    ckv = _compress(kvb, consts)
    o_c, sel_t = _cmp_prompt(qs, ckv)
    sel = jnp.swapaxes(sel_t, 2, 3).astype(BF16)
    o_s = _slc_prompt(qs, kvb, sel)
    o_w = _win_prompt(qs, kvb)
    o = _merge(o_c.reshape(b * s, NSA_Q_DIM), o_s.reshape(b * s, NSA_Q_DIM), o_w.reshape(b * s, NSA_Q_DIM), hg)
    y = _mm(o, w_out).reshape(b, s, D_MODEL)
    kv = h3[..., NSA_Q_DIM:].reshape(b, s, 6, NSA_KV_HEADS, NSA_HEAD_DIM)
    rows = kv[:, :, :4].reshape(b, s // PAGE_SIZE, PAGE_SIZE, 4, NSA_KV_HEADS, NSA_HEAD_DIM)
    win = kv[:, s - min(WINDOW, s):, 4:]
    return y, rows, win


def _gather_pages(page_refs):
    n_sk = 2 * NSA_KV_HEADS
    pages = []
    for r in page_refs:
        rows = r.reshape(PAGE_SIZE * n_sk, NSA_HEAD_DIM)
        pages.append(jnp.concatenate(
            [rows[pl.ds(sk, PAGE_SIZE, stride=n_sk), :].astype(BF16) for sk in range(n_sk)], axis=1))
    return jnp.concatenate(pages, axis=0)


def _cmp_sample_kernel(pt_ref, *refs, t_new, p_len):
    del pt_ref
    npg = SAMPLE_PAGES_PER_STEP
    page_refs = refs[:npg]
    q_ref, pb_ref, pbt_ref, pe_ref, phi_ref, mt_ref, oc_ref, sel_ref, tail_ref, s_ref, cv_ref = refs[npg:]
    c = pl.program_id(1)
    rows = NSA_GROUP * t_new
    n_r = s_ref.shape[2]

    @pl.when(c == 0)
    def _():
        tail_ref[...] = jnp.zeros_like(tail_ref)

    x = _gather_pages(page_refs)
    ckv = _compress_chunk(x, tail_ref[...], pb_ref, pbt_ref, pe_ref, phi_ref)
    tail_ref[...] = x[CMP_CHUNK - CMP_STRIDE:, :]
    r0 = pl.multiple_of(c * CMP_ROWS, CMP_ROWS)
    cv_ref[pl.ds(r0, CMP_ROWS), :] = ckv[:, NSA_KV_DIM:].astype(BF16)
    for kv in range(NSA_KV_HEADS):
        ck = ckv[:, kv * NSA_HEAD_DIM:(kv + 1) * NSA_HEAD_DIM].astype(BF16)
        s_ref[kv, :, pl.ds(r0, CMP_ROWS)] = _dot_nt(q_ref[0, kv], ck)

    @pl.when(c == pl.num_programs(1) - 1)
    def _():
        ri = lax.broadcasted_iota(jnp.int32, (rows, n_r), 1)
        ti = lax.broadcasted_iota(jnp.int32, (rows, n_r), 0) % t_new
        dist_i = (p_len + ti) - (CMP_STRIDE * ri + CMP_STRIDE - 1)
        mask = (dist_i >= 0) & (ri >= 1)
        maskf = mask.astype(F32)
        dist = dist_i.astype(F32)
        n_pad = mt_ref.shape[1]
        blk = lax.broadcasted_iota(jnp.int32, (rows, n_pad), 1)
        cur = (p_len + lax.broadcasted_iota(jnp.int32, (rows, n_pad), 0) % t_new) // SLC_BLOCK
        n_slc = -(-(p_len + t_new) // SLC_BLOCK)
        avail = blk <= cur
        forced = (blk == 0) | (blk == cur) | (blk == cur - 1)
        gi = lax.broadcasted_iota(jnp.int32, (rows, rows), 0) % t_new
        gj = lax.broadcasted_iota(jnp.int32, (rows, rows), 1) % t_new
        gsum = jnp.where(gi == gj, 1.0, 0.0).astype(BF16)
        for kv in range(NSA_KV_HEADS):
            s = jnp.where(mask, s_ref[kv] - _slope_col(kv, rows, t_new) * dist, NEG_INF)
            e = jnp.exp(s - jnp.max(s, axis=-1, keepdims=True)) * maskf
            p = e / jnp.maximum(jnp.sum(e, axis=-1, keepdims=True), 1e-30)
            oc_ref[0, kv] = _dot(p.astype(BF16), cv_ref[:, kv * NSA_HEAD_DIM:(kv + 1) * NSA_HEAD_DIM])
            p_sum = sum(_dot(gsum, part) for part in _split3(p))
            p_slc = sum(_dot(part, mt_ref[...]) for part in _split3(p_sum))
            score = jnp.where(avail, p_slc + jnp.where(forced, FORCE_BONUS, 0.0), -1.0)
            score = jnp.where(blk < n_slc, score, -3.0)
            sel_ref[0, kv] = _select_top(score, 1, n_pad)


def _page_spec(layer, half, d):
    npg = SAMPLE_PAGES_PER_STEP
    return pl.BlockSpec((1, 1, PAGE_SIZE, 2 * NSA_KV_HEADS, NSA_HEAD_DIM),
                        lambda i, c, pt: (layer, pt[i, c * npg + d], 0, half, 0))


def _cmp_sample(cache, layer, page_table, q16, consts, t_new):
    db, n_pages = page_table.shape
    p_len = n_pages * PAGE_SIZE
    n_chunks = p_len // CMP_CHUNK
    n_r = p_len // CMP_STRIDE
    n_slc = -(-(p_len + t_new) // SLC_BLOCK)
    n_pad = -(-n_slc // LANES) * LANES
    mt = _slc_map(n_slc, n_r, n_pad).T
    pb, pbt, pe_term, phi = consts
    rows = NSA_GROUP * t_new
    full = lambda a: pl.BlockSpec(a.shape, lambda i, c, pt: (0,) * a.ndim)
    gs = pltpu.PrefetchScalarGridSpec(
        num_scalar_prefetch=1,
        grid=(db, n_chunks),
        in_specs=[_page_spec(layer, 0, d) for d in range(SAMPLE_PAGES_PER_STEP)]
                 + [pl.BlockSpec((1, NSA_KV_HEADS, rows, NSA_HEAD_DIM), lambda i, c, pt: (i, 0, 0, 0)),
                    full(pb), full(pbt), full(pe_term), full(phi), full(mt)],
        out_specs=[pl.BlockSpec((1, NSA_KV_HEADS, rows, NSA_HEAD_DIM), lambda i, c, pt: (i, 0, 0, 0)),
                   pl.BlockSpec((1, NSA_KV_HEADS, rows, n_pad), lambda i, c, pt: (i, 0, 0, 0))],
        scratch_shapes=[pltpu.VMEM((CMP_STRIDE, 2 * NSA_KV_DIM), BF16),
                        pltpu.VMEM((NSA_KV_HEADS, rows, n_r), F32),
                        pltpu.VMEM((n_r, NSA_KV_DIM), BF16)],
    )
    return pl.pallas_call(
        functools.partial(_cmp_sample_kernel, t_new=t_new, p_len=p_len),
        grid_spec=gs,
        out_shape=[jax.ShapeDtypeStruct((db, NSA_KV_HEADS, rows, NSA_HEAD_DIM), F32),
                   jax.ShapeDtypeStruct((db, NSA_KV_HEADS, rows, n_pad), F32)],
        compiler_params=_cparams("parallel", "arbitrary"),
        name="nsa_cmp_sample",
    )(page_table, *([cache] * SAMPLE_PAGES_PER_STEP), q16, pb, pbt, pe_term, phi, mt)


def _online_update(s, maskf, vv, m_ref, l_ref, acc_ref, kv):
    m_old = m_ref[kv]
    m_new = jnp.maximum(m_old, jnp.max(s, axis=-1, keepdims=True))
    a = jnp.exp(m_old - m_new)
    e = jnp.exp(s - m_new) * maskf
    l_ref[kv] = a * l_ref[kv] + jnp.sum(e, axis=-1, keepdims=True)
    acc_ref[kv] = a * acc_ref[kv] + _dot(e.astype(BF16), vv)
    m_ref[kv] = m_new


def _slc_sample_kernel(pt_ref, *refs, t_new, p_len):
    del pt_ref
    npg = SAMPLE_PAGES_PER_STEP
    page_refs = refs[:npg]
    q_ref, sel_ref, kn_ref, vn_ref, o_ref, m_ref, l_ref, acc_ref = refs[npg:]
    c = pl.program_id(1)
    rows = NSA_GROUP * t_new
    n_pad = sel_ref.shape[3]

    @pl.when(c == 0)
    def _():
        m_ref[...] = jnp.full_like(m_ref, NEG_INF)
        l_ref[...] = jnp.zeros_like(l_ref)
        acc_ref[...] = jnp.zeros_like(acc_ref)

    x = _gather_pages(page_refs)
    eb = lax.broadcasted_iota(jnp.int32, (n_pad, CMP_CHUNK), 0)
    ec = (c * CMP_CHUNK + lax.broadcasted_iota(jnp.int32, (n_pad, CMP_CHUNK), 1)) // SLC_BLOCK
    expand = jnp.where(eb == ec, 1.0, 0.0).astype(BF16)
    ti = lax.broadcasted_iota(jnp.int32, (rows, CMP_CHUNK), 0) % t_new
    kpos = c * CMP_CHUNK + lax.broadcasted_iota(jnp.int32, (rows, CMP_CHUNK), 1)
    dist = ((p_len + ti) - kpos).astype(F32)
    for kv in range(NSA_KV_HEADS):
        mask = _dot(sel_ref[0, kv], expand) > 0.5
        s = _dot_nt(q_ref[0, kv], x[:, kv * NSA_HEAD_DIM:(kv + 1) * NSA_HEAD_DIM])
        s = jnp.where(mask, s - _slope_col(kv, rows, t_new) * dist, NEG_INF)
        _online_update(s, mask.astype(F32), x[:, NSA_KV_DIM + kv * NSA_HEAD_DIM:NSA_KV_DIM + (kv + 1) * NSA_HEAD_DIM],
                       m_ref, l_ref, acc_ref, kv)

    @pl.when(c == pl.num_programs(1) - 1)
    def _():
        n_new = kn_ref.shape[1]
        ti2 = lax.broadcasted_iota(jnp.int32, (rows, n_new), 0) % t_new
        ci = lax.broadcasted_iota(jnp.int32, (rows, n_new), 1)
        mask2 = (ci <= ti2) & (ci < t_new)
        dist2 = (ti2 - ci).astype(F32)
        for kv in range(NSA_KV_HEADS):
            s = _dot_nt(q_ref[0, kv], kn_ref[0, :, kv * NSA_HEAD_DIM:(kv + 1) * NSA_HEAD_DIM])
            s = jnp.where(mask2, s - _slope_col(kv, rows, t_new) * dist2, NEG_INF)
            _online_update(s, mask2.astype(F32), vn_ref[0, :, kv * NSA_HEAD_DIM:(kv + 1) * NSA_HEAD_DIM],
                           m_ref, l_ref, acc_ref, kv)
            o_ref[0, kv] = acc_ref[kv] / jnp.maximum(l_ref[kv], 1e-30)


def _slc_sample(cache, layer, page_table, q16, sel16, k_new, v_new, t_new):
    db, n_pages = page_table.shape
    p_len = n_pages * PAGE_SIZE
    assert p_len % SLC_BLOCK == 0
    n_chunks = p_len // CMP_CHUNK
    rows = NSA_GROUP * t_new
    n_pad = sel16.shape[3]
    per_b = lambda a: pl.BlockSpec((1,) + a.shape[1:], lambda i, c, pt: (i,) + (0,) * (a.ndim - 1))
    gs = pltpu.PrefetchScalarGridSpec(
        num_scalar_prefetch=1,
        grid=(db, n_chunks),
        in_specs=[_page_spec(layer, 1, d) for d in range(SAMPLE_PAGES_PER_STEP)]
                 + [per_b(q16), per_b(sel16), per_b(k_new), per_b(v_new)],
        out_specs=pl.BlockSpec((1, NSA_KV_HEADS, rows, NSA_HEAD_DIM), lambda i, c, pt: (i, 0, 0, 0)),
        scratch_shapes=[pltpu.VMEM((NSA_KV_HEADS, rows, 1), F32),
                        pltpu.VMEM((NSA_KV_HEADS, rows, 1), F32),
                        pltpu.VMEM((NSA_KV_HEADS, rows, NSA_HEAD_DIM), F32)],
    )
    return pl.pallas_call(
        functools.partial(_slc_sample_kernel, t_new=t_new, p_len=p_len),
        grid_spec=gs,
        out_shape=jax.ShapeDtypeStruct((db, NSA_KV_HEADS, rows, NSA_HEAD_DIM), F32),
        compiler_params=_cparams("parallel", "arbitrary"),
        name="nsa_slc_sample",
    )(page_table, *([cache] * SAMPLE_PAGES_PER_STEP), q16, sel16, k_new, v_new)


def _win_sample_kernel(q_ref, wk_ref, wv_ref, kn_ref, vn_ref, o_ref, *, t_new):
    rows = NSA_GROUP * t_new
    wb = wk_ref.shape[1]
    n_new = kn_ref.shape[1]
    t1 = lax.broadcasted_iota(jnp.int32, (rows, wb), 0) % t_new
    c1 = lax.broadcasted_iota(jnp.int32, (rows, wb), 1)
    d1 = wb + t1 - c1
    mask1 = (d1 >= 0) & (d1 <= WINDOW)
    t2 = lax.broadcasted_iota(jnp.int32, (rows, n_new), 0) % t_new
    c2 = lax.broadcasted_iota(jnp.int32, (rows, n_new), 1)
    d2 = t2 - c2
    mask2 = (d2 >= 0) & (d2 <= WINDOW) & (c2 < t_new)
    for kv in range(NSA_KV_HEADS):
        lanes = slice(kv * NSA_HEAD_DIM, (kv + 1) * NSA_HEAD_DIM)
        slope = _slope_col(kv, rows, t_new)
        s1 = jnp.where(mask1, _dot_nt(q_ref[0, kv], wk_ref[0, :, lanes]) - slope * d1.astype(F32), NEG_INF)
        s2 = jnp.where(mask2, _dot_nt(q_ref[0, kv], kn_ref[0, :, lanes]) - slope * d2.astype(F32), NEG_INF)
        m = jnp.maximum(jnp.max(s1, axis=-1, keepdims=True), jnp.max(s2, axis=-1, keepdims=True))
        e1 = jnp.exp(s1 - m) * mask1.astype(F32)
        e2 = jnp.exp(s2 - m) * mask2.astype(F32)
        den = jnp.maximum(jnp.sum(e1, axis=-1, keepdims=True) + jnp.sum(e2, axis=-1, keepdims=True), 1e-30)
        o_ref[0, kv] = (_dot(e1.astype(BF16), wv_ref[0, :, lanes]) + _dot(e2.astype(BF16), vn_ref[0, :, lanes])) / den


def _win_sample(q16, wk, wv, k_new, v_new, t_new):
    db = q16.shape[0]
    rows = NSA_GROUP * t_new
    per_b = lambda a: pl.BlockSpec((1,) + a.shape[1:], lambda i: (i,) + (0,) * (a.ndim - 1))
    return pl.pallas_call(
        functools.partial(_win_sample_kernel, t_new=t_new),
        grid=(db,),
        in_specs=[per_b(q16), per_b(wk), per_b(wv), per_b(k_new), per_b(v_new)],
        out_specs=pl.BlockSpec((1, NSA_KV_HEADS, rows, NSA_HEAD_DIM), lambda i: (i, 0, 0, 0)),
        out_shape=jax.ShapeDtypeStruct((db, NSA_KV_HEADS, rows, NSA_HEAD_DIM), F32),
        compiler_params=_cparams("parallel"),
        name="nsa_win_sample",
    )(q16, wk, wv, k_new, v_new)


def _pad_rows(a, n):
    return jnp.pad(a, ((0, 0), (0, n - a.shape[1]), (0, 0)))


def _nsa_sample(x, cache, layer, win_l, page_table, weights):
    db, t, _ = x.shape
    w_main, w_gate, consts, w_out = weights
    xb = x.reshape(db * t, D_MODEL).astype(BF16)
    h = _mm(xb, w_main)
    hg = _mm(xb, w_gate)
    h3 = h.reshape(db, t, NSA_MAIN_DIM)
    q16 = (h3[..., :NSA_Q_DIM] * NSA_SCALE).astype(BF16).reshape(db, t, NSA_KV_HEADS, NSA_GROUP, NSA_HEAD_DIM)
    q16 = q16.transpose(0, 2, 3, 1, 4).reshape(db, NSA_KV_HEADS, NSA_GROUP * t, NSA_HEAD_DIM)
    kvn = h3[..., NSA_Q_DIM:].reshape(db, t, 6, NSA_KV_DIM)
    new = lambda slot: _pad_rows(kvn[:, :, slot].astype(BF16), LANES)
    cache = cache.reshape(cache.shape[:3] + (4 * NSA_KV_HEADS, NSA_HEAD_DIM))
    o_c, sel16 = _cmp_sample(cache, layer, page_table, q16, consts, t)
    o_s = _slc_sample(cache, layer, page_table, q16, sel16.astype(BF16), new(2), new(3), t)
    wb = win_l.shape[1]
    wkv = win_l.astype(BF16).reshape(db, wb, 2, NSA_KV_DIM)
    o_w = _win_sample(q16, wkv[:, :, 0], wkv[:, :, 1], new(4), new(5), t)
    unrow = lambda o: o.reshape(db, NSA_KV_HEADS, NSA_GROUP, t, NSA_HEAD_DIM).transpose(0, 3, 1, 2, 4).reshape(
        db * t, NSA_Q_DIM)
    o = _merge(unrow(o_c), unrow(o_s), unrow(o_w), hg)
    y = _mm(o, w_out).reshape(db, t, D_MODEL)
    kv = h3[..., NSA_Q_DIM:].reshape(db, t, 6, NSA_KV_HEADS, NSA_HEAD_DIM)
    wbuf = jnp.concatenate([win_l, kv[:, :, 4:]], axis=1)
    return y, kv[:, :, :4], wbuf[:, wbuf.shape[1] - wb:]


S5_SEQS = SUBLANES
S5_CB_GROUPS = 16
S5_CB_STATES = S5_CB_GROUPS * S5_STATE
S5_CB_CH = S5_CB_GROUPS * S5_GROUP_CH
S5_N_CB = S5_GROUPS // S5_CB_GROUPS
S5_MAX_STEPS = 128


def _s5_discretize(a_re, a_im, log_dt, b_re, b_im):
    dt = jnp.exp(log_dt.astype(F32))[:, None]
    mag = jnp.exp(a_re * dt)
    ab_re = mag * jnp.cos(a_im * dt)
    ab_im = mag * jnp.sin(a_im * dt)
    den = a_re * a_re + a_im * a_im
    nr = ab_re - 1.0
    f_re = (nr * a_re + ab_im * a_im) / den
    f_im = (ab_im * a_re - nr * a_im) / den
    bb_re = f_re[..., None] * b_re - f_im[..., None] * b_im
    bb_im = f_re[..., None] * b_im + f_im[..., None] * b_re
    return ab_re, ab_im, bb_re, bb_im


def _s5_weights(w_in, a_re, a_im, log_dt, b_re, b_im, c_re, c_im, d, w_glu):
    ab_re, ab_im, bb_re, bb_im = _s5_discretize(a_re, a_im, log_dt, b_re, b_im)
    eye = jnp.eye(S5_CB_GROUPS, dtype=F32)

    def bd_in(bb):
        x = bb.reshape(S5_N_CB, S5_CB_GROUPS, S5_STATE, S5_GROUP_CH)
        return jnp.einsum('ngpc,gh->ngchp', x, eye).reshape(S5_N_CB, S5_CB_CH, S5_CB_STATES)

    def bd_out(cc):
        x = cc.reshape(S5_N_CB, S5_CB_GROUPS, S5_GROUP_CH, S5_STATE)
        return jnp.einsum('ngcp,gh->ngphc', x, eye).reshape(S5_N_CB, S5_CB_STATES, S5_CB_CH)

    bmat = jnp.concatenate([bd_in(bb_re), bd_in(bb_im)], axis=2).astype(BF16)
    cmat = jnp.concatenate([bd_out(c_re.astype(F32)), -bd_out(c_im.astype(F32))], axis=1).astype(BF16)
    rep = lambda a: jnp.broadcast_to(a.reshape(S5_N_CB, 1, S5_CB_STATES), (S5_N_CB, S5_SEQS, S5_CB_STATES))
    return dict(w_in=w_in.astype(BF16), w_glu=w_glu.astype(BF16), bmat=bmat, cmat=cmat,
                ar=rep(ab_re), ai=rep(ab_im), ab_re=ab_re, ab_im=ab_im, d=d.reshape(1, D_MODEL).astype(F32))


def _s5_scan_kernel(u_ref, b_ref, c_ref, ar_ref, ai_ref, d_ref, h0r_ref, h0i_ref, *rest, steps, with_y):
    if with_y:
        y_ref, hr_ref, hi_ref, bu_ref = rest
    else:
        hr_ref, hi_ref, bu_ref = rest
    ns = S5_CB_STATES

    @pl.when(pl.program_id(1) == 0)
    def _():
        hr_ref[...] = h0r_ref[...]
        hi_ref[...] = h0i_ref[...]

    u = u_ref[...]
    bu_ref[...] = _dot(u.astype(BF16), b_ref[0])
    ar = ar_ref[0]
    ai = ai_ref[0]

    def step(s, carry):
        hr, hi = carry
        r0 = pl.multiple_of(s * S5_SEQS, S5_SEQS)
        br = bu_ref[pl.ds(r0, S5_SEQS), :ns]
        bi = bu_ref[pl.ds(r0, S5_SEQS), ns:]
        nr = ar * hr - ai * hi + br
        ni = ar * hi + ai * hr + bi
        if with_y:
            bu_ref[pl.ds(r0, S5_SEQS), :ns] = nr
            bu_ref[pl.ds(r0, S5_SEQS), ns:] = ni
        return nr, ni

    hr, hi = lax.fori_loop(0, steps, step, (hr_ref[0], hi_ref[0]))
    hr_ref[0] = hr
    hi_ref[0] = hi
    if with_y:
        y_ref[...] = _dot(bu_ref[...].astype(BF16), c_ref[0]) + d_ref[...] * u


def _s5_scan(u_rows, w, h0r, h0i, with_y=True):
    n_rows = u_rows.shape[0]
    n_steps = n_rows // S5_SEQS
    steps = min(S5_MAX_STEPS, n_steps)
    assert n_steps % steps == 0
    tr = steps * S5_SEQS
    cb_spec = lambda a: pl.BlockSpec((1,) + a.shape[1:], lambda cb, t: (cb,) + (0,) * (a.ndim - 1))
    y_spec = [pl.BlockSpec((tr, S5_CB_CH), lambda cb, t: (t, cb))] if with_y else []
    y_shape = [jax.ShapeDtypeStruct((n_rows, D_MODEL), F32)] if with_y else []
    return pl.pallas_call(
        functools.partial(_s5_scan_kernel, steps=steps, with_y=with_y),
        grid=(S5_N_CB, n_steps // steps),
        in_specs=[pl.BlockSpec((tr, S5_CB_CH), lambda cb, t: (t, cb)),
                  cb_spec(w['bmat']), cb_spec(w['cmat']), cb_spec(w['ar']), cb_spec(w['ai']),
                  pl.BlockSpec((1, S5_CB_CH), lambda cb, t: (0, cb)),
                  cb_spec(h0r), cb_spec(h0i)],
        out_specs=y_spec + [cb_spec(h0r), cb_spec(h0i)],
        out_shape=y_shape + [jax.ShapeDtypeStruct(h0r.shape, F32), jax.ShapeDtypeStruct(h0i.shape, F32)],
        scratch_shapes=[pltpu.VMEM((tr, 2 * S5_CB_STATES), F32)],
        compiler_params=_cparams("parallel", "arbitrary"),
        name="s5_scan",
    )(u_rows, w['bmat'], w['cmat'], w['ar'], w['ai'], w['d'], h0r, h0i)


def _state_to_blocks(h):
    return h.reshape(h.shape[0], S5_N_CB, S5_CB_STATES).transpose(1, 0, 2)


def _blocks_to_state(h):
    return h.transpose(1, 0, 2).reshape(h.shape[1], S5_GROUPS, S5_STATE)


def _cpow2(re, im, n):
    for _ in range(n):
        re, im = re * re - im * im, 2.0 * re * im
    return re, im


def _s5_prompt(x, w):
    b, t, _ = x.shape
    n_seg = S5_SEQS // b
    seg = t // n_seg
    assert n_seg * b == S5_SEQS and seg & (seg - 1) == 0
    xr = x.reshape(b, n_seg, seg, D_MODEL).transpose(2, 0, 1, 3).reshape(t * b, D_MODEL)
    u = _mm(xr.astype(BF16), w['w_in'])
    zero = jnp.zeros((S5_N_CB, S5_SEQS, S5_CB_STATES), F32)
    er, ei = _s5_scan(u, w, zero, zero, with_y=False)
    er = _blocks_to_state(er).reshape(b, n_seg, S5_GROUPS, S5_STATE)
    ei = _blocks_to_state(ei).reshape(b, n_seg, S5_GROUPS, S5_STATE)
    pr, pi = _cpow2(w['ab_re'], w['ab_im'], int(math.log2(seg)))
    sr = [jnp.zeros((b, S5_GROUPS, S5_STATE), F32)]
    si = [jnp.zeros((b, S5_GROUPS, S5_STATE), F32)]
    for k in range(n_seg - 1):
        sr.append(er[:, k] + pr * sr[k] - pi * si[k])
        si.append(ei[:, k] + pr * si[k] + pi * sr[k])
    h0r = _state_to_blocks(jnp.stack(sr, axis=1).reshape(S5_SEQS, S5_GROUPS, S5_STATE))
    h0i = _state_to_blocks(jnp.stack(si, axis=1).reshape(S5_SEQS, S5_GROUPS, S5_STATE))
    y, hr, hi = _s5_scan(u, w, h0r, h0i)
    hr = _blocks_to_state(hr).reshape(b, n_seg, S5_GROUPS, S5_STATE)[:, -1]
    hi = _blocks_to_state(hi).reshape(b, n_seg, S5_GROUPS, S5_STATE)[:, -1]
    unperm = lambda a: a.reshape(seg, b, n_seg, -1).transpose(1, 2, 0, 3).reshape(b * t, -1)
    z = _mm(unperm(_gelu_rows(y)), w['w_glu'])
    return z, jnp.stack([hr, hi], axis=1)


def _s5_sample(x, h0, w):
    db, t, _ = x.shape
    assert db == S5_SEQS
    xr = x.transpose(1, 0, 2).reshape(t * db, D_MODEL)
    u = _mm(xr.astype(BF16), w['w_in'])
    y, hr, hi = _s5_scan(u, w, _state_to_blocks(h0[:, 0].astype(F32)), _state_to_blocks(h0[:, 1].astype(F32)))
    z = _mm(_gelu_rows(y), w['w_glu'])
    z = z.reshape(t, db, -1).transpose(1, 0, 2).reshape(db * t, -1)
    return z, jnp.stack([_blocks_to_state(hr), _blocks_to_state(hi)], axis=1).astype(h0.dtype)


def _gelu_kernel(y_ref, o_ref):
    o_ref[...] = _gelu_tanh(y_ref[...]).astype(o_ref.dtype)


def _gelu_rows(y):
    n, d = y.shape
    tm = min(256, n)
    row = pl.BlockSpec((tm, d), lambda i: (i, 0))
    return pl.pallas_call(_gelu_kernel, grid=(n // tm,), in_specs=[row], out_specs=row,
                          out_shape=jax.ShapeDtypeStruct((n, d), BF16),
                          compiler_params=_cparams("parallel"), name="gelu")(y)


ROUTER_TN = 256
N_RANKED = PEER_TOPK
_CAND_PAIRS = [(i, j) for i in range(N_RANKED) for j in range(N_RANKED) if (i + 1) * (j + 1) <= N_RANKED]


UNRANKED = 127.0


def _top_values(s, n, with_rank=False):
    vals = []
    cur = s
    rank = jnp.full(s.shape, UNRANKED, F32)
    for t in range(n):
        m = jnp.max(cur, axis=0, keepdims=True)
        vals.append(m)
        hit = cur == m
        if with_rank:
            rank = jnp.where(hit, float(t), rank)
        cur = jnp.where(hit, -jnp.inf, cur)
    return (vals, rank) if with_rank else vals


def _router_kernel(x_ref, wq_ref, keys_ref, w1z_ref, cnt_ref, rank_ref, w2_ref):
    q = _dot(x_ref[...].astype(BF16), wq_ref[...])
    for h in range(PEER_HEADS):
        s = [_dot_nt(keys_ref[2 * h + c], q[:, (2 * h + c) * PEER_HALF:(2 * h + c + 1) * PEER_HALF].astype(BF16))
             for c in range(2)]
        top1 = _top_values(s[0], N_RANKED)
        top2, rank2 = _top_values(s[1], N_RANKED, with_rank=True)
        cand = jnp.concatenate([top1[i] + top2[j] for i, j in _CAND_PAIRS], axis=0)
        tau = _top_values(cand, PEER_TOPK)[PEER_TOPK - 1]
        m1, m2 = top1[0], top2[0]
        z = jnp.sum(jnp.where(cand >= tau, jnp.exp(cand - (m1 + m2)), 0.0), axis=0, keepdims=True)
        cnt = jnp.zeros(s[0].shape, F32)
        for j in range(N_RANKED):
            cnt = cnt + jnp.where(s[0] + top2[j] >= tau, 1.0, 0.0)
        w1z_ref[h] = jnp.exp(s[0] - m1) / z
        cnt_ref[h] = cnt
        rank_ref[h] = rank2.astype(rank_ref.dtype)
        w2_ref[h] = jnp.exp(s[1] - m2).astype(w2_ref.dtype)


def _router(x, wq, keys):
    n = x.shape[0]
    tn = min(ROUTER_TN, n)
    assert n % tn == 0
    shape = (PEER_HEADS, PEER_N_KEYS, n)
    ospec = pl.BlockSpec((PEER_HEADS, PEER_N_KEYS, tn), lambda i: (0, 0, i))
    return pl.pallas_call(
        _router_kernel, grid=(n // tn,),
        in_specs=[pl.BlockSpec((tn, D_MODEL), lambda i: (i, 0)),
                  pl.BlockSpec(wq.shape, lambda i: (0, 0)),
                  pl.BlockSpec(keys.shape, lambda i: (0, 0, 0))],
        out_specs=[ospec] * 4,
        out_shape=[jax.ShapeDtypeStruct(shape, F32), jax.ShapeDtypeStruct(shape, F32),
                   jax.ShapeDtypeStruct(shape, BF16), jax.ShapeDtypeStruct(shape, BF16)],
        compiler_params=_cparams("parallel"), name="peer_router",
    )(x, wq, keys)


EXPERT_TN = 512
EXPERT_TE = 1024
EXPERT_NC = EXPERT_TE // PEER_N_KEYS
EXPERT_NT = PEER_N_EXPERTS // EXPERT_TE


def _expert_gate(w1z_ref, cnt_ref, rank_ref, w2_ref, c):
    g = None
    for h in range(PEER_HEADS):
        w2 = w2_ref[h]
        w1 = w1z_ref[h, pl.ds(c, 1), :].astype(BF16)
        cnt = cnt_ref[h, pl.ds(c, 1), :].astype(BF16)
        term = w1 * jnp.where(rank_ref[h] < cnt, w2, jnp.zeros_like(w2))
        g = term if g is None else g + term
    return g


def _expert_kernel(xt_ref, u_ref, v_ref, w1z_ref, cnt_ref, rank_ref, w2_ref, o_ref):
    j = pl.program_id(1)

    @pl.when(j == 0)
    def _():
        o_ref[...] = jnp.zeros_like(o_ref)

    ht = _dot(u_ref[...], xt_ref[...])
    parts = []
    for cc in range(EXPERT_NC):
        g = _expert_gate(w1z_ref, cnt_ref, rank_ref, w2_ref, j * EXPERT_NC + cc)
        parts.append(_gelu_tanh(ht[cc * PEER_N_KEYS:(cc + 1) * PEER_N_KEYS].astype(BF16)) * g)
    o_ref[...] += lax.dot_general(v_ref[...], jnp.concatenate(parts, axis=0), (((0,), (0,)), ((), ())),
                                  preferred_element_type=F32)


def _expert(xt, u, v, layer, w1z, cnt, rank, w2):
    n = xt.shape[1]
    tn = min(EXPERT_TN, n)
    assert n % tn == 0
    rspec = pl.BlockSpec((PEER_HEADS, PEER_N_KEYS, tn), lambda i, j: (0, 0, i))
    wspec = pl.BlockSpec((None, EXPERT_TE, D_MODEL), lambda i, j: (layer, j, 0))
    return pl.pallas_call(
        _expert_kernel, grid=(n // tn, EXPERT_NT),
        in_specs=[pl.BlockSpec((D_MODEL, tn), lambda i, j: (0, i)), wspec, wspec, rspec, rspec, rspec, rspec],
        out_specs=pl.BlockSpec((D_MODEL, tn), lambda i, j: (0, i)),
        out_shape=jax.ShapeDtypeStruct((D_MODEL, n), F32),
        compiler_params=_cparams("parallel", "arbitrary"), name="peer_expert",
    )(xt, u, v, w1z, cnt, rank, w2)


def _peer_tables(peer_u, peer_v):
    return peer_u.astype(BF16), peer_v.astype(BF16)


def _peer_t(x, xt, w_q, sub_keys, tables, layer):
    u, v = tables
    keys = sub_keys.reshape(2 * PEER_HEADS, PEER_N_KEYS, PEER_HALF).astype(BF16)
    w1z, cnt, rank, w2 = _router(x, w_q.astype(BF16), keys)
    return _expert(xt, u, v, layer, w1z, cnt, rank, w2)


def _peer(x, w_q, sub_keys, tables, layer):
    n = x.shape[0]
    xp = jnp.pad(x, ((0, -(-n // LANES) * LANES - n), (0, 0)))
    return _peer_t(xp, xp.T.astype(BF16), w_q, sub_keys, tables, layer).T[:n]


def kernel(x_prompt, x_sample, cache_nsa, state_win, state_s5, page_table, nsa_w_in, nsa_cmp_pool, nsa_cmp_pe,
           nsa_cmp_phi, nsa_w_out, s5_w_in, s5_a_re, s5_a_im, s5_log_dt, s5_b_re, s5_b_im, s5_c_re, s5_c_im, s5_d,
           s5_w_glu, peer_w_q, peer_sub_keys, peer_u, peer_v, ln_g, ln_b):
    b, s, _ = x_prompt.shape
    db, t, _ = x_sample.shape
    xp = x_prompt.reshape(b * s, D_MODEL)
    xs = x_sample.reshape(db * t, D_MODEL)
    rows_p, rows_s, win_p, win_s, s5_p, s5_s = [], [], [], [], [], []
    tables = _peer_tables(peer_u, peer_v)
    for layer in range(DEPTH):
        j = layer // N_MIXERS
        if layer % N_MIXERS == 0:
            w = _nsa_weights(nsa_w_in[j], nsa_cmp_pool[j], nsa_cmp_pe[j], nsa_cmp_phi[j], nsa_w_out[j])
            mp, rp, wp = _nsa_prompt(xp.reshape(b, s, D_MODEL), w)
            ms, rs, ws = _nsa_sample(xs.reshape(db, t, D_MODEL), cache_nsa, j, state_win[j], page_table, w)
            rows_p.append(rp)
            rows_s.append(rs)
            win_p.append(wp)
            win_s.append(ws)
            xp, xpt = _ln_res(xp, mp.reshape(b * s, D_MODEL), ln_g[layer, 0], ln_b[layer, 0], with_t=True)
            xs = _ln_res(xs, ms.reshape(db * t, D_MODEL), ln_g[layer, 0], ln_b[layer, 0])
        else:
            w = _s5_weights(s5_w_in[j], s5_a_re[j], s5_a_im[j], s5_log_dt[j], s5_b_re[j], s5_b_im[j],
                            s5_c_re[j], s5_c_im[j], s5_d[j], s5_w_glu[j])
            zp, hp = _s5_prompt(xp.reshape(b, s, D_MODEL), w)
            zs, hs = _s5_sample(xs.reshape(db, t, D_MODEL), state_s5[j], w)
            s5_p.append(hp.astype(state_s5.dtype))
            s5_s.append(hs)
            xp, xpt = _ln_res(xp, zp, ln_g[layer, 0], ln_b[layer, 0], mixer='glu', with_t=True)
            xs = _ln_res(xs, zs, ln_g[layer, 0], ln_b[layer, 0], mixer='glu')
        xp = _ln_res(xp, _peer_t(xp, xpt, peer_w_q[layer], peer_sub_keys[layer], tables, layer),
                     ln_g[layer, 1], ln_b[layer, 1], mixer='cols')
        xs = _ln_res(xs, _peer(xs, peer_w_q[layer], peer_sub_keys[layer], tables, layer),
                     ln_g[layer, 1], ln_b[layer, 1])
    return (xp.reshape(b, s, D_MODEL), xs.reshape(db, t, D_MODEL), jnp.stack(rows_p), jnp.stack(rows_s),
            jnp.stack(win_p), jnp.stack(win_s), jnp.stack(s5_p), jnp.stack(s5_s))
```

```python
import functools
import math

import numpy as np
import jax
import jax.numpy as jnp
from jax import lax
from jax.experimental import pallas as pl
from jax.experimental.pallas import tpu as pltpu

F32 = jnp.float32
BF16 = jnp.bfloat16

D_MODEL = 2048
DEPTH = 4
PAGE_SIZE = 128
N_MIXERS = 2
NSA_HEADS = 16
NSA_KV_HEADS = 4
NSA_GROUP = NSA_HEADS // NSA_KV_HEADS
NSA_HEAD_DIM = D_MODEL // NSA_HEADS
NSA_Q_DIM = NSA_HEADS * NSA_HEAD_DIM
NSA_KV_DIM = NSA_KV_HEADS * NSA_HEAD_DIM
NSA_MAIN_DIM = NSA_Q_DIM + 6 * NSA_KV_DIM
NSA_N_GATES = 3 * NSA_HEADS
NSA_SCALE = NSA_HEAD_DIM ** -0.5
CMP_BLOCK = 32
CMP_STRIDE = 16
SLC_BLOCK = 64
SLC_TOP_N = 16
WINDOW = 512
FORCE_BONUS = 1.0e4
S5_GROUP_CH = 16
S5_GROUPS = D_MODEL // S5_GROUP_CH
S5_STATE = 64
PEER_HEADS = 8
PEER_N_KEYS = 128
PEER_N_EXPERTS = PEER_N_KEYS ** 2
PEER_HALF = 128
PEER_TOPK = 16
LN_EPS = 1e-5
NEG_INF = -1.0e30
ALPHA = (2.0 * DEPTH) ** 0.25

LANES = 128
SUBLANES = 8
VMEM_LIMIT_BYTES = 56 * 1024 * 1024

CMP_CHUNK = 2048
CMP_ROWS = CMP_CHUNK // CMP_STRIDE
SAMPLE_PAGES_PER_STEP = CMP_CHUNK // PAGE_SIZE

_SLOPES = [2.0 ** (-8.0 * (h + 1) / NSA_HEADS) for h in range(NSA_HEADS)]


def _cparams(*sem):
    return pltpu.CompilerParams(dimension_semantics=sem, vmem_limit_bytes=VMEM_LIMIT_BYTES)


def _dot(a, b):
    return jnp.dot(a, b, preferred_element_type=F32)


def _dot_nt(a, b):
    return lax.dot_general(a, b, (((1,), (1,)), ((), ())), preferred_element_type=F32)


def _split3(x):
    hi = x.astype(BF16)
    r1 = x - hi.astype(F32)
    mid = r1.astype(BF16)
    lo = (r1 - mid.astype(F32)).astype(BF16)
    return hi, mid, lo


def _gelu_tanh(x):
    c = math.sqrt(2.0 / math.pi)
    return 0.5 * x * (1.0 + jnp.tanh(c * (x + 0.044715 * (x * x * x))))


def _mm_kernel(a_ref, b_ref, o_ref):
    o_ref[...] = _dot(a_ref[...], b_ref[...]).astype(o_ref.dtype)


def _mm(a, b, tm=1024, tn=1024, out_dtype=F32):
    m, k = a.shape
    n = b.shape[1]
    tm = min(tm, m)
    tn = min(tn, n)
    assert m % tm == 0 and n % tn == 0
    return pl.pallas_call(
        _mm_kernel,
        grid=(n // tn, m // tm),
        in_specs=[pl.BlockSpec((tm, k), lambda j, i: (i, 0)),
                  pl.BlockSpec((k, tn), lambda j, i: (0, j))],
        out_specs=pl.BlockSpec((tm, tn), lambda j, i: (i, j)),
        out_shape=jax.ShapeDtypeStruct((m, n), out_dtype),
        compiler_params=_cparams("parallel", "parallel"),
        name="mm",
    )(a, b)


def _ln_kernel(x_ref, *refs, mixer, with_t):
    n_in = 2 if mixer == 'glu' else 1
    g_ref, b_ref = refs[n_in:n_in + 2]
    outs = refs[n_in + 2:]
    if mixer == 'glu':
        mix = refs[0][...] * (1.0 / (1.0 + jnp.exp(-refs[1][...])))
    elif mixer == 'cols':
        mix = refs[0][...].T
    else:
        mix = refs[0][...]
    y = ALPHA * x_ref[...] + mix
    mu = jnp.mean(y, axis=-1, keepdims=True)
    yc = y - mu
    var = jnp.mean(yc * yc, axis=-1, keepdims=True)
    out = yc * lax.rsqrt(var + LN_EPS) * g_ref[...] + b_ref[...]
    outs[0][...] = out
    if with_t:
        outs[1][...] = out.T.astype(outs[1].dtype)


def _ln_res(x, m, g, b, mixer='rows', with_t=False):
    n, d = x.shape
    tm = min(256, n)
    assert n % tm == 0
    row = pl.BlockSpec((tm, d), lambda i: (i, 0))
    col = pl.BlockSpec((d, tm), lambda i: (0, i))
    vec = pl.BlockSpec((1, d), lambda i: (0, 0))
    if mixer == 'glu':
        m_specs, m_args = [row, pl.BlockSpec((tm, d), lambda i: (i, 1))], (m, m)
    elif mixer == 'cols':
        m_specs, m_args = [col], (m,)
    else:
        m_specs, m_args = [row], (m,)
    out_specs = [row] + ([col] if with_t else [])
    out_shape = [jax.ShapeDtypeStruct((n, d), F32)] + ([jax.ShapeDtypeStruct((d, n), BF16)] if with_t else [])
    res = pl.pallas_call(
        functools.partial(_ln_kernel, mixer=mixer, with_t=with_t),
        grid=(n // tm,), in_specs=[row] + m_specs + [vec, vec], out_specs=out_specs, out_shape=out_shape,
        compiler_params=_cparams("parallel"), name="ln_res",
    )(x, *m_args, g.reshape(1, d), b.reshape(1, d))
    return res if with_t else res[0]


def _compress_consts(pool, pe, phi):
    eye_cur = jnp.eye(CMP_ROWS, dtype=F32)
    eye_prev = jnp.eye(CMP_ROWS, k=-1, dtype=F32)
    lo = pool[:, :, None, None, :CMP_STRIDE]
    hi = pool[:, :, None, None, CMP_STRIDE:]
    pb = (eye_prev[None, None, :, :, None] * lo + eye_cur[None, None, :, :, None] * hi).reshape(
        2, NSA_KV_HEADS, CMP_ROWS, CMP_CHUNK)
    first = jnp.asarray((np.arange(CMP_ROWS) == 0).astype(np.float32))[None, None, :, None]
    pbt = first * pool[:, :, None, :CMP_STRIDE]
    pe_term = jnp.einsum('skj,skjd->skd', pool, pe)
    return (pb.astype(BF16), pbt.astype(BF16), pe_term.reshape(2 * NSA_KV_HEADS, NSA_HEAD_DIM).astype(F32),
            phi.astype(BF16))


def _compress_chunk(x, tail, pb_ref, pbt_ref, pe_ref, phi_ref):
    outs = []
    for slot in range(2):
        for kv in range(NSA_KV_HEADS):
            lo = slot * NSA_KV_DIM + kv * NSA_HEAD_DIM
            xs = x[:, lo:lo + NSA_HEAD_DIM]
            pooled = _dot(pb_ref[slot, kv], xs) + _dot(pbt_ref[slot, kv], tail[:, lo:lo + NSA_HEAD_DIM])
            pooled = pooled + pe_ref[pl.ds(slot * NSA_KV_HEADS + kv, 1), :]
            outs.append(_dot(pooled.astype(BF16), phi_ref[slot, kv]))
    return jnp.concatenate(outs, axis=1)


def _slope_col(kv, rows, per):
    gi = lax.broadcasted_iota(jnp.int32, (rows, 1), 0) // per
    col = jnp.full((rows, 1), _SLOPES[kv * NSA_GROUP + NSA_GROUP - 1], F32)
    for g in range(NSA_GROUP - 1):
        col = jnp.where(gi == g, _SLOPES[kv * NSA_GROUP + g], col)
    return col


def _select_top(score, axis, n_entries):
    idx = lax.broadcasted_iota(jnp.int32, score.shape, axis).astype(F32)
    sel = jnp.zeros(score.shape, F32)
    cur = score
    for _ in range(SLC_TOP_N):
        m = jnp.max(cur, axis=axis, keepdims=True)
        first = jnp.min(jnp.where(cur == m, idx, float(n_entries)), axis=axis, keepdims=True)
        hit = idx == first
        sel = jnp.where(hit, 1.0, sel)
        cur = jnp.where(hit, -2.0, cur)
    return sel


def _compress_kernel(x_ref, pb_ref, pbt_ref, pe_ref, phi_ref, o_ref, tail_ref):
    @pl.when(pl.program_id(1) == 0)
    def _():
        tail_ref[...] = jnp.zeros_like(tail_ref)

    x = x_ref[0]
    o_ref[0] = _compress_chunk(x, tail_ref[...], pb_ref, pbt_ref, pe_ref, phi_ref).astype(o_ref.dtype)
    tail_ref[...] = x[CMP_CHUNK - CMP_STRIDE:, :]


def _compress(kvb, consts):
    b, s, _ = kvb.shape
    w = 2 * NSA_KV_DIM
    n_chunks = s // CMP_CHUNK
    pb, pbt, pe_term, phi = consts
    full = lambda a: pl.BlockSpec(a.shape, lambda i, c: (0,) * a.ndim)
    return pl.pallas_call(
        _compress_kernel,
        grid=(b, n_chunks),
        in_specs=[pl.BlockSpec((1, CMP_CHUNK, w), lambda i, c: (i, c, 0)),
                  full(pb), full(pbt), full(pe_term), full(phi)],
        out_specs=pl.BlockSpec((1, CMP_ROWS, w), lambda i, c: (i, c, 0)),
        out_shape=jax.ShapeDtypeStruct((b, s // CMP_STRIDE, w), BF16),
        scratch_shapes=[pltpu.VMEM((CMP_STRIDE, w), BF16)],
        compiler_params=_cparams("parallel", "arbitrary"),
        name="nsa_compress",
    )(kvb, pb, pbt, pe_term, phi)


CMP_TQ = 128


def _cmp_prompt_kernel(q_ref, ckv_ref, mt_ref, oc_ref, sel_ref):
    tq = CMP_TQ
    n_r = ckv_ref.shape[1]
    n_slc = mt_ref.shape[0]
    q0 = pl.program_id(1) * tq
    qpos = q0 + lax.broadcasted_iota(jnp.int32, (tq, n_r), 0)
    r = lax.broadcasted_iota(jnp.int32, (tq, n_r), 1)
    dist_i = qpos - (CMP_STRIDE * r + CMP_STRIDE - 1)
    mask = (dist_i >= 0) & (r >= 1)
    maskf = mask.astype(F32)
    dist = dist_i.astype(F32)
    q = q_ref[0]
    ckv = ckv_ref[0]
    blk = lax.broadcasted_iota(jnp.int32, (n_slc, tq), 0)
    cur = (q0 + lax.broadcasted_iota(jnp.int32, (n_slc, tq), 1)) // SLC_BLOCK
    avail = blk <= cur
    forced = (blk == 0) | (blk == cur) | (blk == cur - 1)
    for kv in range(NSA_KV_HEADS):
        ck = ckv[:, kv * NSA_HEAD_DIM:(kv + 1) * NSA_HEAD_DIM]
        cv = ckv[:, NSA_KV_DIM + kv * NSA_HEAD_DIM:NSA_KV_DIM + (kv + 1) * NSA_HEAD_DIM]
        qs = jnp.concatenate([q[:, (kv * NSA_GROUP + g) * NSA_HEAD_DIM:(kv * NSA_GROUP + g + 1) * NSA_HEAD_DIM]
                              for g in range(NSA_GROUP)], axis=0)
        s_all = _dot_nt(qs, ck)
        p_sum = jnp.zeros((tq, n_r), F32)
        for g in range(NSA_GROUP):
            h = kv * NSA_GROUP + g
            s = jnp.where(mask, s_all[g * tq:(g + 1) * tq] - _SLOPES[h] * dist, NEG_INF)
            e = jnp.exp(s - jnp.max(s, axis=-1, keepdims=True)) * maskf
            p = e / jnp.maximum(jnp.sum(e, axis=-1, keepdims=True), 1e-30)
            oc_ref[0, :, h * NSA_HEAD_DIM:(h + 1) * NSA_HEAD_DIM] = _dot(p.astype(BF16), cv)
            p_sum = p_sum + p
        mt = mt_ref[...]
        p_slc = sum(_dot_nt(mt, part) for part in _split3(p_sum))
        score = jnp.where(avail, p_slc + jnp.where(forced, FORCE_BONUS, 0.0), -1.0)
        sel_ref[0, kv] = _select_top(score, 0, n_slc)


def _slc_map(n_slc, n_r, n_pad):
    m = np.arange(n_pad)[:, None]
    r = np.arange(n_r)[None, :]
    ratio = SLC_BLOCK // CMP_STRIDE
    return jnp.asarray(((r >= ratio * m) & (r <= ratio * m + ratio) & (m < n_slc)).astype(np.float32), BF16)


def _cmp_prompt(qs, ckv):
    b, s, _ = qs.shape
    n_r = ckv.shape[1]
    n_slc = s // SLC_BLOCK
    mt = _slc_map(n_slc, n_r, n_slc)
    return pl.pallas_call(
        _cmp_prompt_kernel,
        grid=(b, s // CMP_TQ),
        in_specs=[pl.BlockSpec((1, CMP_TQ, NSA_Q_DIM), lambda i, t: (i, t, 0)),
                  pl.BlockSpec((1, n_r, 2 * NSA_KV_DIM), lambda i, t: (i, 0, 0)),
                  pl.BlockSpec(mt.shape, lambda i, t: (0, 0))],
        out_specs=[pl.BlockSpec((1, CMP_TQ, NSA_Q_DIM), lambda i, t: (i, t, 0)),
                   pl.BlockSpec((1, NSA_KV_HEADS, n_slc, CMP_TQ), lambda i, t: (i, 0, 0, t))],
        out_shape=[jax.ShapeDtypeStruct((b, s, NSA_Q_DIM), F32),
                   jax.ShapeDtypeStruct((b, NSA_KV_HEADS, n_slc, s), F32)],
        compiler_params=_cparams("parallel", "parallel"),
        name="nsa_cmp_prompt",
    )(qs, ckv, mt)


SLC_TQ = 256
SLC_TK = 1024


def _slc_prompt_kernel(q_ref, k_ref, v_ref, sel_ref, o_ref, m_ref, l_ref, acc_ref):
    tq, tk = SLC_TQ, SLC_TK
    qi = pl.program_id(1)
    kj = pl.program_id(2)
    n_sel = sel_ref.shape[3]

    @pl.when(kj == 0)
    def _():
        m_ref[...] = jnp.full_like(m_ref, NEG_INF)
        l_ref[...] = jnp.zeros_like(l_ref)
        acc_ref[...] = jnp.zeros_like(acc_ref)

    @pl.when(kj * tk <= qi * tq + tq - 1)
    def _():
        q = q_ref[0]
        kt = k_ref[0]
        vt = v_ref[0]
        qpos = qi * tq + lax.broadcasted_iota(jnp.int32, (tq, tk), 0)
        kpos = kj * tk + lax.broadcasted_iota(jnp.int32, (tq, tk), 1)
        dist_i = qpos - kpos
        krel = (kj * tk - qi * tq + lax.broadcasted_iota(jnp.int32, (1, tk), 1)).astype(F32)
        eb = lax.broadcasted_iota(jnp.int32, (n_sel, tk), 0)
        ec = (kj * tk + lax.broadcasted_iota(jnp.int32, (n_sel, tk), 1)) // SLC_BLOCK
        expand = jnp.where(eb == ec, 1.0, 0.0).astype(BF16)
        for kv in range(NSA_KV_HEADS):
            picked = _dot(sel_ref[0, kv], expand)
            mbias = jnp.where((picked > 0.5) & (dist_i >= 0), 0.0, NEG_INF)
            qs = jnp.concatenate([q[:, (kv * NSA_GROUP + g) * NSA_HEAD_DIM:(kv * NSA_GROUP + g + 1) * NSA_HEAD_DIM]
                                  for g in range(NSA_GROUP)], axis=0)
            s_all = _dot_nt(qs, kt[:, kv * NSA_HEAD_DIM:(kv + 1) * NSA_HEAD_DIM])
            vv = vt[:, kv * NSA_HEAD_DIM:(kv + 1) * NSA_HEAD_DIM]
            for g in range(NSA_GROUP):
                h = kv * NSA_GROUP + g
                rows = slice(g * tq, (g + 1) * tq)
                s = s_all[rows] + (mbias + _SLOPES[h] * krel)
                m_old = m_ref[kv, rows]
                m_new = jnp.maximum(m_old, jnp.max(s, axis=-1, keepdims=True))
                a = jnp.exp(m_old - m_new)
                e = jnp.exp(s - m_new)
                l_ref[kv, rows] = a * l_ref[kv, rows] + jnp.sum(e, axis=-1, keepdims=True)
                acc_ref[kv, rows] = a * acc_ref[kv, rows] + _dot(e.astype(BF16), vv)
                m_ref[kv, rows] = m_new

    @pl.when(kj == pl.num_programs(2) - 1)
    def _():
        for kv in range(NSA_KV_HEADS):
            for g in range(NSA_GROUP):
                h = kv * NSA_GROUP + g
                rows = slice(g * tq, (g + 1) * tq)
                o_ref[0, :, h * NSA_HEAD_DIM:(h + 1) * NSA_HEAD_DIM] = (
                    acc_ref[kv, rows] / jnp.maximum(l_ref[kv, rows], 1e-30))


def _slc_prompt(qs, kvb, sel):
    b, s, _ = qs.shape
    tq, tk = min(SLC_TQ, s), min(SLC_TK, s)
    assert tq == SLC_TQ and tk == SLC_TK
    n_sel = sel.shape[3]
    last = lambda t: (t * tq + tq - 1) // tk
    return pl.pallas_call(
        _slc_prompt_kernel,
        grid=(b, s // tq, s // tk),
        in_specs=[pl.BlockSpec((1, tq, NSA_Q_DIM), lambda i, t, j: (i, t, 0)),
                  pl.BlockSpec((1, tk, NSA_KV_DIM), lambda i, t, j: (i, jnp.minimum(j, last(t)), 2)),
                  pl.BlockSpec((1, tk, NSA_KV_DIM), lambda i, t, j: (i, jnp.minimum(j, last(t)), 3)),
                  pl.BlockSpec((1, NSA_KV_HEADS, tq, n_sel), lambda i, t, j: (i, 0, t, 0))],
        out_specs=pl.BlockSpec((1, tq, NSA_Q_DIM), lambda i, t, j: (i, t, 0)),
        out_shape=jax.ShapeDtypeStruct((b, s, NSA_Q_DIM), F32),
        scratch_shapes=[pltpu.VMEM((NSA_KV_HEADS, NSA_GROUP * tq, 1), F32),
                        pltpu.VMEM((NSA_KV_HEADS, NSA_GROUP * tq, 1), F32),
                        pltpu.VMEM((NSA_KV_HEADS, NSA_GROUP * tq, NSA_HEAD_DIM), F32)],
        compiler_params=_cparams("parallel", "parallel", "arbitrary"),
        name="nsa_slc_prompt",
    )(qs, kvb, kvb, sel)


WIN_TQ = 256
WIN_NT = WINDOW // WIN_TQ + 1


def _win_prompt_kernel(q_ref, *refs):
    tq = WIN_TQ
    k_refs = refs[:WIN_NT]
    v_refs = refs[WIN_NT:2 * WIN_NT]
    o_ref = refs[2 * WIN_NT]
    qi = pl.program_id(1)
    nk = WIN_NT * tq
    row = lax.broadcasted_iota(jnp.int32, (tq, nk), 0)
    col = lax.broadcasted_iota(jnp.int32, (tq, nk), 1)
    dist_i = WINDOW + row - col
    kpos = (qi - (WIN_NT - 1)) * tq + col
    mbias = jnp.where((dist_i >= 0) & (dist_i <= WINDOW) & (kpos >= 0), 0.0, NEG_INF)
    crel = (lax.broadcasted_iota(jnp.int32, (1, nk), 1) - WINDOW).astype(F32)
    q = q_ref[0]
    kcat = jnp.concatenate([r[0] for r in k_refs], axis=0)
    vcat = jnp.concatenate([r[0] for r in v_refs], axis=0)
    for kv in range(NSA_KV_HEADS):
        qs = jnp.concatenate([q[:, (kv * NSA_GROUP + g) * NSA_HEAD_DIM:(kv * NSA_GROUP + g + 1) * NSA_HEAD_DIM]
                              for g in range(NSA_GROUP)], axis=0)
        s_all = _dot_nt(qs, kcat[:, kv * NSA_HEAD_DIM:(kv + 1) * NSA_HEAD_DIM])
        vv = vcat[:, kv * NSA_HEAD_DIM:(kv + 1) * NSA_HEAD_DIM]
        for g in range(NSA_GROUP):
            h = kv * NSA_GROUP + g
            s = s_all[g * tq:(g + 1) * tq] + (mbias + _SLOPES[h] * crel)
            e = jnp.exp(s - jnp.max(s, axis=-1, keepdims=True))
            den = jnp.maximum(jnp.sum(e, axis=-1, keepdims=True), 1e-30)
            o_ref[0, :, h * NSA_HEAD_DIM:(h + 1) * NSA_HEAD_DIM] = _dot(e.astype(BF16), vv) / den


def _win_prompt(qs, kvb):
    b, s, _ = qs.shape
    tq = WIN_TQ
    kspec = lambda d, col: pl.BlockSpec((1, tq, NSA_KV_DIM),
                                        lambda i, t: (i, jnp.maximum(t - (WIN_NT - 1) + d, 0), col))
    return pl.pallas_call(
        _win_prompt_kernel,
        grid=(b, s // tq),
        in_specs=[pl.BlockSpec((1, tq, NSA_Q_DIM), lambda i, t: (i, t, 0))]
                 + [kspec(d, 4) for d in range(WIN_NT)] + [kspec(d, 5) for d in range(WIN_NT)],
        out_specs=pl.BlockSpec((1, tq, NSA_Q_DIM), lambda i, t: (i, t, 0)),
        out_shape=jax.ShapeDtypeStruct((b, s, NSA_Q_DIM), F32),
        compiler_params=_cparams("parallel", "parallel"),
        name="nsa_win_prompt",
    )(qs, *([kvb] * (2 * WIN_NT)))


def _merge_kernel(oc_ref, os_ref, ow_ref, hg_ref, ex_ref, o_ref):
    gate = 1.0 / (1.0 + jnp.exp(-hg_ref[...]))
    parts = _split3(gate)[:2]
    out = None
    for br, ref in enumerate((oc_ref, os_ref, ow_ref)):
        gx = sum(_dot(p, ex_ref[br]) for p in parts)
        term = gx * ref[...]
        out = term if out is None else out + term
    o_ref[...] = out.astype(o_ref.dtype)


def _gate_expand():
    e = np.zeros((3, LANES, NSA_Q_DIM), np.float32)
    for br in range(3):
        for h in range(NSA_HEADS):
            e[br, h * 3 + br, h * NSA_HEAD_DIM:(h + 1) * NSA_HEAD_DIM] = 1.0
    return jnp.asarray(e, BF16)


def _merge(oc, os_, ow, hg):
    n = oc.shape[0]
    tm = min(256, n)
    ex = _gate_expand()
    row = pl.BlockSpec((tm, NSA_Q_DIM), lambda i: (i, 0))
    return pl.pallas_call(
        _merge_kernel, grid=(n // tm,),
        in_specs=[row, row, row, pl.BlockSpec((tm, LANES), lambda i: (i, 0)),
                  pl.BlockSpec(ex.shape, lambda i: (0, 0, 0))],
        out_specs=row,
        out_shape=jax.ShapeDtypeStruct((n, NSA_Q_DIM), BF16),
        compiler_params=_cparams("parallel"), name="nsa_merge",
    )(oc, os_, ow, hg, ex)


def _nsa_weights(w_in, pool, pe, phi, w_out):
    w_main = w_in[:, :NSA_MAIN_DIM].astype(BF16)
    w_gate = jnp.pad(w_in[:, NSA_MAIN_DIM:], ((0, 0), (0, LANES - NSA_N_GATES))).astype(BF16)
    return w_main, w_gate, _compress_consts(pool, pe, phi), w_out.astype(BF16)


def _nsa_prompt(x, weights):
    b, s, _ = x.shape
    w_main, w_gate, consts, w_out = weights
    xb = x.reshape(b * s, D_MODEL).astype(BF16)
    h = _mm(xb, w_main)
    hg = _mm(xb, w_gate)
    h3 = h.reshape(b, s, NSA_MAIN_DIM)
    qs = (h3[..., :NSA_Q_DIM] * NSA_SCALE).astype(BF16)
    kvb = h3[..., NSA_Q_DIM:].astype(BF16)
    ckv = _compress(kvb, consts)
    o_c, sel_t = _cmp_prompt(qs, ckv)
    sel = jnp.swapaxes(sel_t, 2, 3).astype(BF16)
    o_s = _slc_prompt(qs, kvb, sel)
    o_w = _win_prompt(qs, kvb)
    o = _merge(o_c.reshape(b * s, NSA_Q_DIM), o_s.reshape(b * s, NSA_Q_DIM), o_w.reshape(b * s, NSA_Q_DIM), hg)
    y = _mm(o, w_out).reshape(b, s, D_MODEL)
    kv = h3[..., NSA_Q_DIM:].reshape(b, s, 6, NSA_KV_HEADS, NSA_HEAD_DIM)
    rows = kv[:, :, :4].reshape(b, s // PAGE_SIZE, PAGE_SIZE, 4, NSA_KV_HEADS, NSA_HEAD_DIM)
    win = kv[:, s - min(WINDOW, s):, 4:]
    return y, rows, win


def _gather_pages(page_refs):
    n_sk = 2 * NSA_KV_HEADS
    pages = []
    for r in page_refs:
        rows = r.reshape(PAGE_SIZE * n_sk, NSA_HEAD_DIM)
        pages.append(jnp.concatenate(
            [rows[pl.ds(sk, PAGE_SIZE, stride=n_sk), :].astype(BF16) for sk in range(n_sk)], axis=1))
    return jnp.concatenate(pages, axis=0)


def _cmp_sample_kernel(pt_ref, *refs, t_new, p_len):
    del pt_ref
    npg = SAMPLE_PAGES_PER_STEP
    page_refs = refs[:npg]
    q_ref, pb_ref, pbt_ref, pe_ref, phi_ref, mt_ref, oc_ref, sel_ref, tail_ref, s_ref, cv_ref = refs[npg:]
    c = pl.program_id(1)
    rows = NSA_GROUP * t_new
    n_r = s_ref.shape[2]

    @pl.when(c == 0)
    def _():
        tail_ref[...] = jnp.zeros_like(tail_ref)

    x = _gather_pages(page_refs)
    ckv = _compress_chunk(x, tail_ref[...], pb_ref, pbt_ref, pe_ref, phi_ref)
    tail_ref[...] = x[CMP_CHUNK - CMP_STRIDE:, :]
    r0 = pl.multiple_of(c * CMP_ROWS, CMP_ROWS)
    cv_ref[pl.ds(r0, CMP_ROWS), :] = ckv[:, NSA_KV_DIM:].astype(BF16)
    for kv in range(NSA_KV_HEADS):
        ck = ckv[:, kv * NSA_HEAD_DIM:(kv + 1) * NSA_HEAD_DIM].astype(BF16)
        s_ref[kv, :, pl.ds(r0, CMP_ROWS)] = _dot_nt(q_ref[0, kv], ck)

    @pl.when(c == pl.num_programs(1) - 1)
    def _():
        ri = lax.broadcasted_iota(jnp.int32, (rows, n_r), 1)
        ti = lax.broadcasted_iota(jnp.int32, (rows, n_r), 0) % t_new
        dist_i = (p_len + ti) - (CMP_STRIDE * ri + CMP_STRIDE - 1)
        mask = (dist_i >= 0) & (ri >= 1)
        maskf = mask.astype(F32)
        dist = dist_i.astype(F32)
        n_pad = mt_ref.shape[1]
        blk = lax.broadcasted_iota(jnp.int32, (rows, n_pad), 1)
        cur = (p_len + lax.broadcasted_iota(jnp.int32, (rows, n_pad), 0) % t_new) // SLC_BLOCK
        n_slc = -(-(p_len + t_new) // SLC_BLOCK)
        avail = blk <= cur
        forced = (blk == 0) | (blk == cur) | (blk == cur - 1)
        gi = lax.broadcasted_iota(jnp.int32, (rows, rows), 0) % t_new
        gj = lax.broadcasted_iota(jnp.int32, (rows, rows), 1) % t_new
        gsum = jnp.where(gi == gj, 1.0, 0.0).astype(BF16)
        for kv in range(NSA_KV_HEADS):
            s = jnp.where(mask, s_ref[kv] - _slope_col(kv, rows, t_new) * dist, NEG_INF)
            e = jnp.exp(s - jnp.max(s, axis=-1, keepdims=True)) * maskf
            p = e / jnp.maximum(jnp.sum(e, axis=-1, keepdims=True), 1e-30)
            oc_ref[0, kv] = _dot(p.astype(BF16), cv_ref[:, kv * NSA_HEAD_DIM:(kv + 1) * NSA_HEAD_DIM])
            p_sum = sum(_dot(gsum, part) for part in _split3(p))
            p_slc = sum(_dot(part, mt_ref[...]) for part in _split3(p_sum))
            score = jnp.where(avail, p_slc + jnp.where(forced, FORCE_BONUS, 0.0), -1.0)
            score = jnp.where(blk < n_slc, score, -3.0)
            sel_ref[0, kv] = _select_top(score, 1, n_pad)


def _page_spec(layer, half, d):
    npg = SAMPLE_PAGES_PER_STEP
    return pl.BlockSpec((1, 1, PAGE_SIZE, 2 * NSA_KV_HEADS, NSA_HEAD_DIM),
                        lambda i, c, pt: (layer, pt[i, c * npg + d], 0, half, 0))


def _cmp_sample(cache, layer, page_table, q16, consts, t_new):
    db, n_pages = page_table.shape
    p_len = n_pages * PAGE_SIZE
    n_chunks = p_len // CMP_CHUNK
    n_r = p_len // CMP_STRIDE
    n_slc = -(-(p_len + t_new) // SLC_BLOCK)
    n_pad = -(-n_slc // LANES) * LANES
    mt = _slc_map(n_slc, n_r, n_pad).T
    pb, pbt, pe_term, phi = consts
    rows = NSA_GROUP * t_new
    full = lambda a: pl.BlockSpec(a.shape, lambda i, c, pt: (0,) * a.ndim)
    gs = pltpu.PrefetchScalarGridSpec(
        num_scalar_prefetch=1,
        grid=(db, n_chunks),
        in_specs=[_page_spec(layer, 0, d) for d in range(SAMPLE_PAGES_PER_STEP)]
                 + [pl.BlockSpec((1, NSA_KV_HEADS, rows, NSA_HEAD_DIM), lambda i, c, pt: (i, 0, 0, 0)),
                    full(pb), full(pbt), full(pe_term), full(phi), full(mt)],
        out_specs=[pl.BlockSpec((1, NSA_KV_HEADS, rows, NSA_HEAD_DIM), lambda i, c, pt: (i, 0, 0, 0)),
                   pl.BlockSpec((1, NSA_KV_HEADS, rows, n_pad), lambda i, c, pt: (i, 0, 0, 0))],
        scratch_shapes=[pltpu.VMEM((CMP_STRIDE, 2 * NSA_KV_DIM), BF16),
                        pltpu.VMEM((NSA_KV_HEADS, rows, n_r), F32),
                        pltpu.VMEM((n_r, NSA_KV_DIM), BF16)],
    )
    return pl.pallas_call(
        functools.partial(_cmp_sample_kernel, t_new=t_new, p_len=p_len),
        grid_spec=gs,
        out_shape=[jax.ShapeDtypeStruct((db, NSA_KV_HEADS, rows, NSA_HEAD_DIM), F32),
                   jax.ShapeDtypeStruct((db, NSA_KV_HEADS, rows, n_pad), F32)],
        compiler_params=_cparams("parallel", "arbitrary"),
        name="nsa_cmp_sample",
    )(page_table, *([cache] * SAMPLE_PAGES_PER_STEP), q16, pb, pbt, pe_term, phi, mt)


def _online_update(s, maskf, vv, m_ref, l_ref, acc_ref, kv):
    m_old = m_ref[kv]
    m_new = jnp.maximum(m_old, jnp.max(s, axis=-1, keepdims=True))
    a = jnp.exp(m_old - m_new)
    e = jnp.exp(s - m_new) * maskf
    l_ref[kv] = a * l_ref[kv] + jnp.sum(e, axis=-1, keepdims=True)
    acc_ref[kv] = a * acc_ref[kv] + _dot(e.astype(BF16), vv)
    m_ref[kv] = m_new


def _slc_sample_kernel(pt_ref, *refs, t_new, p_len):
    del pt_ref
    npg = SAMPLE_PAGES_PER_STEP
    page_refs = refs[:npg]
    q_ref, sel_ref, kn_ref, vn_ref, o_ref, m_ref, l_ref, acc_ref = refs[npg:]
    c = pl.program_id(1)
    rows = NSA_GROUP * t_new
    n_pad = sel_ref.shape[3]

    @pl.when(c == 0)
    def _():
        m_ref[...] = jnp.full_like(m_ref, NEG_INF)
        l_ref[...] = jnp.zeros_like(l_ref)
        acc_ref[...] = jnp.zeros_like(acc_ref)

    x = _gather_pages(page_refs)
    eb = lax.broadcasted_iota(jnp.int32, (n_pad, CMP_CHUNK), 0)
    ec = (c * CMP_CHUNK + lax.broadcasted_iota(jnp.int32, (n_pad, CMP_CHUNK), 1)) // SLC_BLOCK
    expand = jnp.where(eb == ec, 1.0, 0.0).astype(BF16)
    ti = lax.broadcasted_iota(jnp.int32, (rows, CMP_CHUNK), 0) % t_new
    kpos = c * CMP_CHUNK + lax.broadcasted_iota(jnp.int32, (rows, CMP_CHUNK), 1)
    dist = ((p_len + ti) - kpos).astype(F32)
    for kv in range(NSA_KV_HEADS):
        mask = _dot(sel_ref[0, kv], expand) > 0.5
        s = _dot_nt(q_ref[0, kv], x[:, kv * NSA_HEAD_DIM:(kv + 1) * NSA_HEAD_DIM])
        s = jnp.where(mask, s - _slope_col(kv, rows, t_new) * dist, NEG_INF)
        _online_update(s, mask.astype(F32), x[:, NSA_KV_DIM + kv * NSA_HEAD_DIM:NSA_KV_DIM + (kv + 1) * NSA_HEAD_DIM],
                       m_ref, l_ref, acc_ref, kv)

    @pl.when(c == pl.num_programs(1) - 1)
    def _():
        n_new = kn_ref.shape[1]
        ti2 = lax.broadcasted_iota(jnp.int32, (rows, n_new), 0) % t_new
        ci = lax.broadcasted_iota(jnp.int32, (rows, n_new), 1)
        mask2 = (ci <= ti2) & (ci < t_new)
        dist2 = (ti2 - ci).astype(F32)
        for kv in range(NSA_KV_HEADS):
            s = _dot_nt(q_ref[0, kv], kn_ref[0, :, kv * NSA_HEAD_DIM:(kv + 1) * NSA_HEAD_DIM])
            s = jnp.where(mask2, s - _slope_col(kv, rows, t_new) * dist2, NEG_INF)
            _online_update(s, mask2.astype(F32), vn_ref[0, :, kv * NSA_HEAD_DIM:(kv + 1) * NSA_HEAD_DIM],
                           m_ref, l_ref, acc_ref, kv)
            o_ref[0, kv] = acc_ref[kv] / jnp.maximum(l_ref[kv], 1e-30)


def _slc_sample(cache, layer, page_table, q16, sel16, k_new, v_new, t_new):
    db, n_pages = page_table.shape
    p_len = n_pages * PAGE_SIZE
    assert p_len % SLC_BLOCK == 0
    n_chunks = p_len // CMP_CHUNK
    rows = NSA_GROUP * t_new
    n_pad = sel16.shape[3]
    per_b = lambda a: pl.BlockSpec((1,) + a.shape[1:], lambda i, c, pt: (i,) + (0,) * (a.ndim - 1))
    gs = pltpu.PrefetchScalarGridSpec(
        num_scalar_prefetch=1,
        grid=(db, n_chunks),
        in_specs=[_page_spec(layer, 1, d) for d in range(SAMPLE_PAGES_PER_STEP)]
                 + [per_b(q16), per_b(sel16), per_b(k_new), per_b(v_new)],
        out_specs=pl.BlockSpec((1, NSA_KV_HEADS, rows, NSA_HEAD_DIM), lambda i, c, pt: (i, 0, 0, 0)),
        scratch_shapes=[pltpu.VMEM((NSA_KV_HEADS, rows, 1), F32),
                        pltpu.VMEM((NSA_KV_HEADS, rows, 1), F32),
                        pltpu.VMEM((NSA_KV_HEADS, rows, NSA_HEAD_DIM), F32)],
    )
    return pl.pallas_call(
        functools.partial(_slc_sample_kernel, t_new=t_new, p_len=p_len),
        grid_spec=gs,
        out_shape=jax.ShapeDtypeStruct((db, NSA_KV_HEADS, rows, NSA_HEAD_DIM), F32),
        compiler_params=_cparams("parallel", "arbitrary"),
        name="nsa_slc_sample",
    )(page_table, *([cache] * SAMPLE_PAGES_PER_STEP), q16, sel16, k_new, v_new)


def _win_sample_kernel(q_ref, wk_ref, wv_ref, kn_ref, vn_ref, o_ref, *, t_new):
    rows = NSA_GROUP * t_new
    wb = wk_ref.shape[1]
    n_new = kn_ref.shape[1]
    t1 = lax.broadcasted_iota(jnp.int32, (rows, wb), 0) % t_new
    c1 = lax.broadcasted_iota(jnp.int32, (rows, wb), 1)
    d1 = wb + t1 - c1
    mask1 = (d1 >= 0) & (d1 <= WINDOW)
    t2 = lax.broadcasted_iota(jnp.int32, (rows, n_new), 0) % t_new
    c2 = lax.broadcasted_iota(jnp.int32, (rows, n_new), 1)
    d2 = t2 - c2
    mask2 = (d2 >= 0) & (d2 <= WINDOW) & (c2 < t_new)
    for kv in range(NSA_KV_HEADS):
        lanes = slice(kv * NSA_HEAD_DIM, (kv + 1) * NSA_HEAD_DIM)
        slope = _slope_col(kv, rows, t_new)
        s1 = jnp.where(mask1, _dot_nt(q_ref[0, kv], wk_ref[0, :, lanes]) - slope * d1.astype(F32), NEG_INF)
        s2 = jnp.where(mask2, _dot_nt(q_ref[0, kv], kn_ref[0, :, lanes]) - slope * d2.astype(F32), NEG_INF)
        m = jnp.maximum(jnp.max(s1, axis=-1, keepdims=True), jnp.max(s2, axis=-1, keepdims=True))
        e1 = jnp.exp(s1 - m) * mask1.astype(F32)
        e2 = jnp.exp(s2 - m) * mask2.astype(F32)
        den = jnp.maximum(jnp.sum(e1, axis=-1, keepdims=True) + jnp.sum(e2, axis=-1, keepdims=True), 1e-30)
        o_ref[0, kv] = (_dot(e1.astype(BF16), wv_ref[0, :, lanes]) + _dot(e2.astype(BF16), vn_ref[0, :, lanes])) / den


def _win_sample(q16, wk, wv, k_new, v_new, t_new):
    db = q16.shape[0]
    rows = NSA_GROUP * t_new
    per_b = lambda a: pl.BlockSpec((1,) + a.shape[1:], lambda i: (i,) + (0,) * (a.ndim - 1))
    return pl.pallas_call(
        functools.partial(_win_sample_kernel, t_new=t_new),
        grid=(db,),
        in_specs=[per_b(q16), per_b(wk), per_b(wv), per_b(k_new), per_b(v_new)],
        out_specs=pl.BlockSpec((1, NSA_KV_HEADS, rows, NSA_HEAD_DIM), lambda i: (i, 0, 0, 0)),
        out_shape=jax.ShapeDtypeStruct((db, NSA_KV_HEADS, rows, NSA_HEAD_DIM), F32),
        compiler_params=_cparams("parallel"),
        name="nsa_win_sample",
    )(q16, wk, wv, k_new, v_new)


def _pad_rows(a, n):
    return jnp.pad(a, ((0, 0), (0, n - a.shape[1]), (0, 0)))


def _nsa_sample(x, cache, layer, win_l, page_table, weights):
    db, t, _ = x.shape
    w_main, w_gate, consts, w_out = weights
    xb = x.reshape(db * t, D_MODEL).astype(BF16)
    h = _mm(xb, w_main)
    hg = _mm(xb, w_gate)
    h3 = h.reshape(db, t, NSA_MAIN_DIM)
    q16 = (h3[..., :NSA_Q_DIM] * NSA_SCALE).astype(BF16).reshape(db, t, NSA_KV_HEADS, NSA_GROUP, NSA_HEAD_DIM)
    q16 = q16.transpose(0, 2, 3, 1, 4).reshape(db, NSA_KV_HEADS, NSA_GROUP * t, NSA_HEAD_DIM)
    kvn = h3[..., NSA_Q_DIM:].reshape(db, t, 6, NSA_KV_DIM)
    new = lambda slot: _pad_rows(kvn[:, :, slot].astype(BF16), LANES)
    cache = cache.reshape(cache.shape[:3] + (4 * NSA_KV_HEADS, NSA_HEAD_DIM))
    o_c, sel16 = _cmp_sample(cache, layer, page_table, q16, consts, t)
    o_s = _slc_sample(cache, layer, page_table, q16, sel16.astype(BF16), new(2), new(3), t)
    wb = win_l.shape[1]
    wkv = win_l.astype(BF16).reshape(db, wb, 2, NSA_KV_DIM)
    o_w = _win_sample(q16, wkv[:, :, 0], wkv[:, :, 1], new(4), new(5), t)
    unrow = lambda o: o.reshape(db, NSA_KV_HEADS, NSA_GROUP, t, NSA_HEAD_DIM).transpose(0, 3, 1, 2, 4).reshape(
        db * t, NSA_Q_DIM)
    o = _merge(unrow(o_c), unrow(o_s), unrow(o_w), hg)
    y = _mm(o, w_out).reshape(db, t, D_MODEL)
    kv = h3[..., NSA_Q_DIM:].reshape(db, t, 6, NSA_KV_HEADS, NSA_HEAD_DIM)
    wbuf = jnp.concatenate([win_l, kv[:, :, 4:]], axis=1)
    return y, kv[:, :, :4], wbuf[:, wbuf.shape[1] - wb:]


S5_SEQS = SUBLANES
S5_CB_GROUPS = 16
S5_CB_STATES = S5_CB_GROUPS * S5_STATE
S5_CB_CH = S5_CB_GROUPS * S5_GROUP_CH
S5_N_CB = S5_GROUPS // S5_CB_GROUPS
S5_MAX_STEPS = 128


def _s5_discretize(a_re, a_im, log_dt, b_re, b_im):
    dt = jnp.exp(log_dt.astype(F32))[:, None]
    mag = jnp.exp(a_re * dt)
    ab_re = mag * jnp.cos(a_im * dt)
    ab_im = mag * jnp.sin(a_im * dt)
    den = a_re * a_re + a_im * a_im
    nr = ab_re - 1.0
    f_re = (nr * a_re + ab_im * a_im) / den
    f_im = (ab_im * a_re - nr * a_im) / den
    bb_re = f_re[..., None] * b_re - f_im[..., None] * b_im
    bb_im = f_re[..., None] * b_im + f_im[..., None] * b_re
    return ab_re, ab_im, bb_re, bb_im


def _s5_weights(w_in, a_re, a_im, log_dt, b_re, b_im, c_re, c_im, d, w_glu):
    ab_re, ab_im, bb_re, bb_im = _s5_discretize(a_re, a_im, log_dt, b_re, b_im)
    eye = jnp.eye(S5_CB_GROUPS, dtype=F32)

    def bd_in(bb):
        x = bb.reshape(S5_N_CB, S5_CB_GROUPS, S5_STATE, S5_GROUP_CH)
        return jnp.einsum('ngpc,gh->ngchp', x, eye).reshape(S5_N_CB, S5_CB_CH, S5_CB_STATES)

    def bd_out(cc):
        x = cc.reshape(S5_N_CB, S5_CB_GROUPS, S5_GROUP_CH, S5_STATE)
        return jnp.einsum('ngcp,gh->ngphc', x, eye).reshape(S5_N_CB, S5_CB_STATES, S5_CB_CH)

    bmat = jnp.concatenate([bd_in(bb_re), bd_in(bb_im)], axis=2).astype(BF16)
    cmat = jnp.concatenate([bd_out(c_re.astype(F32)), -bd_out(c_im.astype(F32))], axis=1).astype(BF16)
    rep = lambda a: jnp.broadcast_to(a.reshape(S5_N_CB, 1, S5_CB_STATES), (S5_N_CB, S5_SEQS, S5_CB_STATES))
    return dict(w_in=w_in.astype(BF16), w_glu=w_glu.astype(BF16), bmat=bmat, cmat=cmat,
                ar=rep(ab_re), ai=rep(ab_im), ab_re=ab_re, ab_im=ab_im, d=d.reshape(1, D_MODEL).astype(F32))


def _s5_scan_kernel(u_ref, b_ref, c_ref, ar_ref, ai_ref, d_ref, h0r_ref, h0i_ref, *rest, steps, with_y):
    if with_y:
        y_ref, hr_ref, hi_ref, bu_ref = rest
    else:
        hr_ref, hi_ref, bu_ref = rest
    ns = S5_CB_STATES

    @pl.when(pl.program_id(1) == 0)
    def _():
        hr_ref[...] = h0r_ref[...]
        hi_ref[...] = h0i_ref[...]

    u = u_ref[...]
    bu_ref[...] = _dot(u.astype(BF16), b_ref[0])
    ar = ar_ref[0]
    ai = ai_ref[0]

    def step(s, carry):
        hr, hi = carry
        r0 = pl.multiple_of(s * S5_SEQS, S5_SEQS)
        br = bu_ref[pl.ds(r0, S5_SEQS), :ns]
        bi = bu_ref[pl.ds(r0, S5_SEQS), ns:]
        nr = ar * hr - ai * hi + br
        ni = ar * hi + ai * hr + bi
        if with_y:
            bu_ref[pl.ds(r0, S5_SEQS), :ns] = nr
            bu_ref[pl.ds(r0, S5_SEQS), ns:] = ni
        return nr, ni

    hr, hi = lax.fori_loop(0, steps, step, (hr_ref[0], hi_ref[0]))
    hr_ref[0] = hr
    hi_ref[0] = hi
    if with_y:
        y_ref[...] = _dot(bu_ref[...].astype(BF16), c_ref[0]) + d_ref[...] * u


def _s5_scan(u_rows, w, h0r, h0i, with_y=True):
    n_rows = u_rows.shape[0]
    n_steps = n_rows // S5_SEQS
    steps = min(S5_MAX_STEPS, n_steps)
    assert n_steps % steps == 0
    tr = steps * S5_SEQS
    cb_spec = lambda a: pl.BlockSpec((1,) + a.shape[1:], lambda cb, t: (cb,) + (0,) * (a.ndim - 1))
    y_spec = [pl.BlockSpec((tr, S5_CB_CH), lambda cb, t: (t, cb))] if with_y else []
    y_shape = [jax.ShapeDtypeStruct((n_rows, D_MODEL), F32)] if with_y else []
    return pl.pallas_call(
        functools.partial(_s5_scan_kernel, steps=steps, with_y=with_y),
        grid=(S5_N_CB, n_steps // steps),
        in_specs=[pl.BlockSpec((tr, S5_CB_CH), lambda cb, t: (t, cb)),
                  cb_spec(w['bmat']), cb_spec(w['cmat']), cb_spec(w['ar']), cb_spec(w['ai']),
                  pl.BlockSpec((1, S5_CB_CH), lambda cb, t: (0, cb)),
                  cb_spec(h0r), cb_spec(h0i)],
        out_specs=y_spec + [cb_spec(h0r), cb_spec(h0i)],
        out_shape=y_shape + [jax.ShapeDtypeStruct(h0r.shape, F32), jax.ShapeDtypeStruct(h0i.shape, F32)],
        scratch_shapes=[pltpu.VMEM((tr, 2 * S5_CB_STATES), F32)],
        compiler_params=_cparams("parallel", "arbitrary"),
        name="s5_scan",
    )(u_rows, w['bmat'], w['cmat'], w['ar'], w['ai'], w['d'], h0r, h0i)


def _state_to_blocks(h):
    return h.reshape(h.shape[0], S5_N_CB, S5_CB_STATES).transpose(1, 0, 2)


def _blocks_to_state(h):
    return h.transpose(1, 0, 2).reshape(h.shape[1], S5_GROUPS, S5_STATE)


def _cpow2(re, im, n):
    for _ in range(n):
        re, im = re * re - im * im, 2.0 * re * im
    return re, im


def _s5_prompt(x, w):
    b, t, _ = x.shape
    n_seg = S5_SEQS // b
    seg = t // n_seg
    assert n_seg * b == S5_SEQS and seg & (seg - 1) == 0
    xr = x.reshape(b, n_seg, seg, D_MODEL).transpose(2, 0, 1, 3).reshape(t * b, D_MODEL)
    u = _mm(xr.astype(BF16), w['w_in'])
    zero = jnp.zeros((S5_N_CB, S5_SEQS, S5_CB_STATES), F32)
    er, ei = _s5_scan(u, w, zero, zero, with_y=False)
    er = _blocks_to_state(er).reshape(b, n_seg, S5_GROUPS, S5_STATE)
    ei = _blocks_to_state(ei).reshape(b, n_seg, S5_GROUPS, S5_STATE)
    pr, pi = _cpow2(w['ab_re'], w['ab_im'], int(math.log2(seg)))
    sr = [jnp.zeros((b, S5_GROUPS, S5_STATE), F32)]
    si = [jnp.zeros((b, S5_GROUPS, S5_STATE), F32)]
    for k in range(n_seg - 1):
        sr.append(er[:, k] + pr * sr[k] - pi * si[k])
        si.append(ei[:, k] + pr * si[k] + pi * sr[k])
    h0r = _state_to_blocks(jnp.stack(sr, axis=1).reshape(S5_SEQS, S5_GROUPS, S5_STATE))
    h0i = _state_to_blocks(jnp.stack(si, axis=1).reshape(S5_SEQS, S5_GROUPS, S5_STATE))
    y, hr, hi = _s5_scan(u, w, h0r, h0i)
    hr = _blocks_to_state(hr).reshape(b, n_seg, S5_GROUPS, S5_STATE)[:, -1]
    hi = _blocks_to_state(hi).reshape(b, n_seg, S5_GROUPS, S5_STATE)[:, -1]
    unperm = lambda a: a.reshape(seg, b, n_seg, -1).transpose(1, 2, 0, 3).reshape(b * t, -1)
    z = _mm(unperm(_gelu_rows(y)), w['w_glu'])
    return z, jnp.stack([hr, hi], axis=1)


def _s5_sample(x, h0, w):
    db, t, _ = x.shape
    assert db == S5_SEQS
    xr = x.transpose(1, 0, 2).reshape(t * db, D_MODEL)
    u = _mm(xr.astype(BF16), w['w_in'])
    y, hr, hi = _s5_scan(u, w, _state_to_blocks(h0[:, 0].astype(F32)), _state_to_blocks(h0[:, 1].astype(F32)))
    z = _mm(_gelu_rows(y), w['w_glu'])
    z = z.reshape(t, db, -1).transpose(1, 0, 2).reshape(db * t, -1)
    return z, jnp.stack([_blocks_to_state(hr), _blocks_to_state(hi)], axis=1).astype(h0.dtype)


def _gelu_kernel(y_ref, o_ref):
    o_ref[...] = _gelu_tanh(y_ref[...]).astype(o_ref.dtype)


def _gelu_rows(y):
    n, d = y.shape
    tm = min(256, n)
    row = pl.BlockSpec((tm, d), lambda i: (i, 0))
    return pl.pallas_call(_gelu_kernel, grid=(n // tm,), in_specs=[row], out_specs=row,
                          out_shape=jax.ShapeDtypeStruct((n, d), BF16),
                          compiler_params=_cparams("parallel"), name="gelu")(y)


ROUTER_TN = 256
N_RANKED = PEER_TOPK
_CAND_PAIRS = [(i, j) for i in range(N_RANKED) for j in range(N_RANKED) if (i + 1) * (j + 1) <= N_RANKED]


UNRANKED = 127.0


def _top_values(s, n, with_rank=False):
    vals = []
    cur = s
    rank = jnp.full(s.shape, UNRANKED, F32)
    for t in range(n):
        m = jnp.max(cur, axis=0, keepdims=True)
        vals.append(m)
        hit = cur == m
        if with_rank:
            rank = jnp.where(hit, float(t), rank)
        cur = jnp.where(hit, -jnp.inf, cur)
    return (vals, rank) if with_rank else vals


def _router_kernel(x_ref, wq_ref, keys_ref, w1z_ref, cnt_ref, rank_ref, w2_ref):
    q = _dot(x_ref[...].astype(BF16), wq_ref[...])
    for h in range(PEER_HEADS):
        s = [_dot_nt(keys_ref[2 * h + c], q[:, (2 * h + c) * PEER_HALF:(2 * h + c + 1) * PEER_HALF].astype(BF16))
             for c in range(2)]
        top1 = _top_values(s[0], N_RANKED)
        top2, rank2 = _top_values(s[1], N_RANKED, with_rank=True)
        cand = jnp.concatenate([top1[i] + top2[j] for i, j in _CAND_PAIRS], axis=0)
        tau = _top_values(cand, PEER_TOPK)[PEER_TOPK - 1]
        m1, m2 = top1[0], top2[0]
        z = jnp.sum(jnp.where(cand >= tau, jnp.exp(cand - (m1 + m2)), 0.0), axis=0, keepdims=True)
        cnt = jnp.zeros(s[0].shape, F32)
        for j in range(N_RANKED):
            cnt = cnt + jnp.where(s[0] + top2[j] >= tau, 1.0, 0.0)
        w1z_ref[h] = jnp.exp(s[0] - m1) / z
        cnt_ref[h] = cnt
        rank_ref[h] = rank2.astype(rank_ref.dtype)
        w2_ref[h] = jnp.exp(s[1] - m2).astype(w2_ref.dtype)


def _router(x, wq, keys):
    n = x.shape[0]
    tn = min(ROUTER_TN, n)
    assert n % tn == 0
    shape = (PEER_HEADS, PEER_N_KEYS, n)
    ospec = pl.BlockSpec((PEER_HEADS, PEER_N_KEYS, tn), lambda i: (0, 0, i))
    return pl.pallas_call(
        _router_kernel, grid=(n // tn,),
        in_specs=[pl.BlockSpec((tn, D_MODEL), lambda i: (i, 0)),
                  pl.BlockSpec(wq.shape, lambda i: (0, 0)),
                  pl.BlockSpec(keys.shape, lambda i: (0, 0, 0))],
        out_specs=[ospec] * 4,
        out_shape=[jax.ShapeDtypeStruct(shape, F32), jax.ShapeDtypeStruct(shape, F32),
                   jax.ShapeDtypeStruct(shape, BF16), jax.ShapeDtypeStruct(shape, BF16)],
        compiler_params=_cparams("parallel"), name="peer_router",
    )(x, wq, keys)


EXPERT_TN = 512
EXPERT_TE = 1024
EXPERT_NC = EXPERT_TE // PEER_N_KEYS
EXPERT_NT = PEER_N_EXPERTS // EXPERT_TE


def _expert_gate(w1z_ref, cnt_ref, rank_ref, w2_ref, c):
    g = None
    for h in range(PEER_HEADS):
        w2 = w2_ref[h]
        w1 = w1z_ref[h, pl.ds(c, 1), :].astype(BF16)
        cnt = cnt_ref[h, pl.ds(c, 1), :].astype(BF16)
        term = w1 * jnp.where(rank_ref[h] < cnt, w2, jnp.zeros_like(w2))
        g = term if g is None else g + term
    return g


def _expert_kernel(xt_ref, u_ref, v_ref, w1z_ref, cnt_ref, rank_ref, w2_ref, o_ref):
    j = pl.program_id(1)

    @pl.when(j == 0)
    def _():
        o_ref[...] = jnp.zeros_like(o_ref)

    ht = _dot(u_ref[...], xt_ref[...])
    parts = []
    for cc in range(EXPERT_NC):
        g = _expert_gate(w1z_ref, cnt_ref, rank_ref, w2_ref, j * EXPERT_NC + cc)
        parts.append(_gelu_tanh(ht[cc * PEER_N_KEYS:(cc + 1) * PEER_N_KEYS].astype(BF16)) * g)
    o_ref[...] += lax.dot_general(v_ref[...], jnp.concatenate(parts, axis=0), (((0,), (0,)), ((), ())),
                                  preferred_element_type=F32)


def _expert(xt, u, v, layer, w1z, cnt, rank, w2):
    n = xt.shape[1]
    tn = min(EXPERT_TN, n)
    assert n % tn == 0
    rspec = pl.BlockSpec((PEER_HEADS, PEER_N_KEYS, tn), lambda i, j: (0, 0, i))
    wspec = pl.BlockSpec((None, EXPERT_TE, D_MODEL), lambda i, j: (layer, j, 0))
    return pl.pallas_call(
        _expert_kernel, grid=(n // tn, EXPERT_NT),
        in_specs=[pl.BlockSpec((D_MODEL, tn), lambda i, j: (0, i)), wspec, wspec, rspec, rspec, rspec, rspec],
        out_specs=pl.BlockSpec((D_MODEL, tn), lambda i, j: (0, i)),
        out_shape=jax.ShapeDtypeStruct((D_MODEL, n), F32),
        compiler_params=_cparams("parallel", "arbitrary"), name="peer_expert",
    )(xt, u, v, w1z, cnt, rank, w2)


def _peer_tables(peer_u, peer_v):
    return peer_u.astype(BF16), peer_v.astype(BF16)


def _peer_t(x, xt, w_q, sub_keys, tables, layer):
    u, v = tables
    keys = sub_keys.reshape(2 * PEER_HEADS, PEER_N_KEYS, PEER_HALF).astype(BF16)
    w1z, cnt, rank, w2 = _router(x, w_q.astype(BF16), keys)
    return _expert(xt, u, v, layer, w1z, cnt, rank, w2)


def _peer(x, w_q, sub_keys, tables, layer):
    n = x.shape[0]
    xp = jnp.pad(x, ((0, -(-n // LANES) * LANES - n), (0, 0)))
    return _peer_t(xp, xp.T.astype(BF16), w_q, sub_keys, tables, layer).T[:n]


def kernel(x_prompt, x_sample, cache_nsa, state_win, state_s5, page_table, nsa_w_in, nsa_cmp_pool, nsa_cmp_pe,
           nsa_cmp_phi, nsa_w_out, s5_w_in, s5_a_re, s5_a_im, s5_log_dt, s5_b_re, s5_b_im, s5_c_re, s5_c_im, s5_d,
           s5_w_glu, peer_w_q, peer_sub_keys, peer_u, peer_v, ln_g, ln_b):
    b, s, _ = x_prompt.shape
    db, t, _ = x_sample.shape
    xp = x_prompt.reshape(b * s, D_MODEL)
    xs = x_sample.reshape(db * t, D_MODEL)
    rows_p, rows_s, win_p, win_s, s5_p, s5_s = [], [], [], [], [], []
    tables = _peer_tables(peer_u, peer_v)
    for layer in range(DEPTH):
        j = layer // N_MIXERS
        if layer % N_MIXERS == 0:
            w = _nsa_weights(nsa_w_in[j], nsa_cmp_pool[j], nsa_cmp_pe[j], nsa_cmp_phi[j], nsa_w_out[j])
            mp, rp, wp = _nsa_prompt(xp.reshape(b, s, D_MODEL), w)
            ms, rs, ws = _nsa_sample(xs.reshape(db, t, D_MODEL), cache_nsa, j, state_win[j], page_table, w)
            rows_p.append(rp)
            rows_s.append(rs)
            win_p.append(wp)
            win_s.append(ws)
            xp, xpt = _ln_res(xp, mp.reshape(b * s, D_MODEL), ln_g[layer, 0], ln_b[layer, 0], with_t=True)
            xs = _ln_res(xs, ms.reshape(db * t, D_MODEL), ln_g[layer, 0], ln_b[layer, 0])
        else:
            w = _s5_weights(s5_w_in[j], s5_a_re[j], s5_a_im[j], s5_log_dt[j], s5_b_re[j], s5_b_im[j],
                            s5_c_re[j], s5_c_im[j], s5_d[j], s5_w_glu[j])
            zp, hp = _s5_prompt(xp.reshape(b, s, D_MODEL), w)
            zs, hs = _s5_sample(xs.reshape(db, t, D_MODEL), state_s5[j], w)
            s5_p.append(hp.astype(state_s5.dtype))
            s5_s.append(hs)
            xp, xpt = _ln_res(xp, zp, ln_g[layer, 0], ln_b[layer, 0], mixer='glu', with_t=True)
            xs = _ln_res(xs, zs, ln_g[layer, 0], ln_b[layer, 0], mixer='glu')
        xp = _ln_res(xp, _peer_t(xp, xpt, peer_w_q[layer], peer_sub_keys[layer], tables, layer),
                     ln_g[layer, 1], ln_b[layer, 1], mixer='cols')
        xs = _ln_res(xs, _peer(xs, peer_w_q[layer], peer_sub_keys[layer], tables, layer),
                     ln_g[layer, 1], ln_b[layer, 1])
    return (xp.reshape(b, s, D_MODEL), xs.reshape(db, t, D_MODEL), jnp.stack(rows_p), jnp.stack(rows_s),
            jnp.stack(win_p), jnp.stack(win_s), jnp.stack(s5_p), jnp.stack(s5_s))
```

```python
import functools
import math

import numpy as np
import jax
import jax.numpy as jnp
from jax import lax
from jax.experimental import pallas as pl
from jax.experimental.pallas import tpu as pltpu

F32 = jnp.float32
BF16 = jnp.bfloat16

D_MODEL = 2048
DEPTH = 4
PAGE_SIZE = 128
N_MIXERS = 2
NSA_HEADS = 16
NSA_KV_HEADS = 4
NSA_GROUP = NSA_HEADS // NSA_KV_HEADS
NSA_HEAD_DIM = D_MODEL // NSA_HEADS
NSA_Q_DIM = NSA_HEADS * NSA_HEAD_DIM
NSA_KV_DIM = NSA_KV_HEADS * NSA_HEAD_DIM
NSA_MAIN_DIM = NSA_Q_DIM + 6 * NSA_KV_DIM
NSA_N_GATES = 3 * NSA_HEADS
NSA_SCALE = NSA_HEAD_DIM ** -0.5
CMP_BLOCK = 32
CMP_STRIDE = 16
SLC_BLOCK = 64
SLC_TOP_N = 16
WINDOW = 512
FORCE_BONUS = 1.0e4
S5_GROUP_CH = 16
S5_GROUPS = D_MODEL // S5_GROUP_CH
S5_STATE = 64
PEER_HEADS = 8
PEER_N_KEYS = 128
PEER_N_EXPERTS = PEER_N_KEYS ** 2
PEER_HALF = 128
PEER_TOPK = 16
LN_EPS = 1e-5
NEG_INF = -1.0e30
ALPHA = (2.0 * DEPTH) ** 0.25

LANES = 128
SUBLANES = 8
VMEM_LIMIT_BYTES = 56 * 1024 * 1024

CMP_CHUNK = 2048
CMP_ROWS = CMP_CHUNK // CMP_STRIDE
SAMPLE_PAGES_PER_STEP = CMP_CHUNK // PAGE_SIZE

_SLOPES = [2.0 ** (-8.0 * (h + 1) / NSA_HEADS) for h in range(NSA_HEADS)]


def _cparams(*sem):
    return pltpu.CompilerParams(dimension_semantics=sem, vmem_limit_bytes=VMEM_LIMIT_BYTES)


def _dot(a, b):
    return jnp.dot(a, b, preferred_element_type=F32)


def _dot_nt(a, b):
    return lax.dot_general(a, b, (((1,), (1,)), ((), ())), preferred_element_type=F32)


def _split3(x):
    hi = x.astype(BF16)
    r1 = x - hi.astype(F32)
    mid = r1.astype(BF16)
    lo = (r1 - mid.astype(F32)).astype(BF16)
    return hi, mid, lo


def _gelu_tanh(x):
    c = math.sqrt(2.0 / math.pi)
    return 0.5 * x * (1.0 + jnp.tanh(c * (x + 0.044715 * (x * x * x))))


def _mm_kernel(a_ref, b_ref, o_ref):
    o_ref[...] = _dot(a_ref[...], b_ref[...]).astype(o_ref.dtype)


def _mm(a, b, tm=1024, tn=1024, out_dtype=F32):
    m, k = a.shape
    n = b.shape[1]
    tm = min(tm, m)
    tn = min(tn, n)
    assert m % tm == 0 and n % tn == 0
    return pl.pallas_call(
        _mm_kernel,
        grid=(n // tn, m // tm),
        in_specs=[pl.BlockSpec((tm, k), lambda j, i: (i, 0)),
                  pl.BlockSpec((k, tn), lambda j, i: (0, j))],
        out_specs=pl.BlockSpec((tm, tn), lambda j, i: (i, j)),
        out_shape=jax.ShapeDtypeStruct((m, n), out_dtype),
        compiler_params=_cparams("parallel", "parallel"),
        name="mm",
    )(a, b)


def _ln_kernel(x_ref, *refs, mixer, with_t):
    n_in = 2 if mixer == 'glu' else 1
    g_ref, b_ref = refs[n_in:n_in + 2]
    outs = refs[n_in + 2:]
    if mixer == 'glu':
        mix = refs[0][...] * (1.0 / (1.0 + jnp.exp(-refs[1][...])))
    elif mixer == 'cols':
        mix = refs[0][...].T
    else:
        mix = refs[0][...]
    y = ALPHA * x_ref[...] + mix
    mu = jnp.mean(y, axis=-1, keepdims=True)
    yc = y - mu
    var = jnp.mean(yc * yc, axis=-1, keepdims=True)
    out = yc * lax.rsqrt(var + LN_EPS) * g_ref[...] + b_ref[...]
    outs[0][...] = out
    if with_t:
        outs[1][...] = out.T.astype(outs[1].dtype)


def _ln_res(x, m, g, b, mixer='rows', with_t=False):
    n, d = x.shape
    tm = min(256, n)
    assert n % tm == 0
    row = pl.BlockSpec((tm, d), lambda i: (i, 0))
    col = pl.BlockSpec((d, tm), lambda i: (0, i))
    vec = pl.BlockSpec((1, d), lambda i: (0, 0))
    if mixer == 'glu':
        m_specs, m_args = [row, pl.BlockSpec((tm, d), lambda i: (i, 1))], (m, m)
    elif mixer == 'cols':
        m_specs, m_args = [col], (m,)
    else:
        m_specs, m_args = [row], (m,)
    out_specs = [row] + ([col] if with_t else [])
    out_shape = [jax.ShapeDtypeStruct((n, d), F32)] + ([jax.ShapeDtypeStruct((d, n), BF16)] if with_t else [])
    res = pl.pallas_call(
        functools.partial(_ln_kernel, mixer=mixer, with_t=with_t),
        grid=(n // tm,), in_specs=[row] + m_specs + [vec, vec], out_specs=out_specs, out_shape=out_shape,
        compiler_params=_cparams("parallel"), name="ln_res",
    )(x, *m_args, g.reshape(1, d), b.reshape(1, d))
    return res if with_t else res[0]


def _compress_consts(pool, pe, phi):
    eye_cur = jnp.eye(CMP_ROWS, dtype=F32)
    eye_prev = jnp.eye(CMP_ROWS, k=-1, dtype=F32)
    lo = pool[:, :, None, None, :CMP_STRIDE]
    hi = pool[:, :, None, None, CMP_STRIDE:]
    pb = (eye_prev[None, None, :, :, None] * lo + eye_cur[None, None, :, :, None] * hi).reshape(
        2, NSA_KV_HEADS, CMP_ROWS, CMP_CHUNK)
    first = jnp.asarray((np.arange(CMP_ROWS) == 0).astype(np.float32))[None, None, :, None]
    pbt = first * pool[:, :, None, :CMP_STRIDE]
    pe_term = jnp.einsum('skj,skjd->skd', pool, pe)
    return (pb.astype(BF16), pbt.astype(BF16), pe_term.reshape(2 * NSA_KV_HEADS, NSA_HEAD_DIM).astype(F32),
            phi.astype(BF16))


def _compress_chunk(x, tail, pb_ref, pbt_ref, pe_ref, phi_ref):
    outs = []
    for slot in range(2):
        for kv in range(NSA_KV_HEADS):
            lo = slot * NSA_KV_DIM + kv * NSA_HEAD_DIM
            xs = x[:, lo:lo + NSA_HEAD_DIM]
            pooled = _dot(pb_ref[slot, kv], xs) + _dot(pbt_ref[slot, kv], tail[:, lo:lo + NSA_HEAD_DIM])
            pooled = pooled + pe_ref[pl.ds(slot * NSA_KV_HEADS + kv, 1), :]
            outs.append(_dot(pooled.astype(BF16), phi_ref[slot, kv]))
    return jnp.concatenate(outs, axis=1)


def _slope_col(kv, rows, per):
    gi = lax.broadcasted_iota(jnp.int32, (rows, 1), 0) // per
    col = jnp.full((rows, 1), _SLOPES[kv * NSA_GROUP + NSA_GROUP - 1], F32)
    for g in range(NSA_GROUP - 1):
        col = jnp.where(gi == g, _SLOPES[kv * NSA_GROUP + g], col)
    return col


def _select_top(score, axis, n_entries):
    idx = lax.broadcasted_iota(jnp.int32, score.shape, axis).astype(F32)
    sel = jnp.zeros(score.shape, F32)
    cur = score
    for _ in range(SLC_TOP_N):
        m = jnp.max(cur, axis=axis, keepdims=True)
        first = jnp.min(jnp.where(cur == m, idx, float(n_entries)), axis=axis, keepdims=True)
        hit = idx == first
        sel = jnp.where(hit, 1.0, sel)
        cur = jnp.where(hit, -2.0, cur)
    return sel


def _compress_kernel(x_ref, pb_ref, pbt_ref, pe_ref, phi_ref, o_ref, tail_ref):
    @pl.when(pl.program_id(1) == 0)
    def _():
        tail_ref[...] = jnp.zeros_like(tail_ref)

    x = x_ref[0]
    o_ref[0] = _compress_chunk(x, tail_ref[...], pb_ref, pbt_ref, pe_ref, phi_ref).astype(o_ref.dtype)
    tail_ref[...] = x[CMP_CHUNK - CMP_STRIDE:, :]


def _compress(kvb, consts):
    b, s, _ = kvb.shape
    w = 2 * NSA_KV_DIM
    n_chunks = s // CMP_CHUNK
    pb, pbt, pe_term, phi = consts
    full = lambda a: pl.BlockSpec(a.shape, lambda i, c: (0,) * a.ndim)
    return pl.pallas_call(
        _compress_kernel,
        grid=(b, n_chunks),
        in_specs=[pl.BlockSpec((1, CMP_CHUNK, w), lambda i, c: (i, c, 0)),
                  full(pb), full(pbt), full(pe_term), full(phi)],
        out_specs=pl.BlockSpec((1, CMP_ROWS, w), lambda i, c: (i, c, 0)),
        out_shape=jax.ShapeDtypeStruct((b, s // CMP_STRIDE, w), BF16),
        scratch_shapes=[pltpu.VMEM((CMP_STRIDE, w), BF16)],
        compiler_params=_cparams("parallel", "arbitrary"),
        name="nsa_compress",
    )(kvb, pb, pbt, pe_term, phi)


CMP_TQ = 256


def _cmp_prompt_kernel(q_ref, ckv_ref, mt_ref, oc_ref, sel_ref):
    tq = CMP_TQ
    n_r = ckv_ref.shape[1]
    n_slc = mt_ref.shape[0]
    q0 = pl.program_id(1) * tq
    qpos = q0 + lax.broadcasted_iota(jnp.int32, (tq, n_r), 0)
    r = lax.broadcasted_iota(jnp.int32, (tq, n_r), 1)
    dist_i = qpos - (CMP_STRIDE * r + CMP_STRIDE - 1)
    mask = (dist_i >= 0) & (r >= 1)
    maskf = mask.astype(F32)
    dist = dist_i.astype(F32)
    q = q_ref[0]
    ckv = ckv_ref[0]
    blk = lax.broadcasted_iota(jnp.int32, (n_slc, tq), 0)
    cur = (q0 + lax.broadcasted_iota(jnp.int32, (n_slc, tq), 1)) // SLC_BLOCK
    avail = blk <= cur
    forced = (blk == 0) | (blk == cur) | (blk == cur - 1)
    for kv in range(NSA_KV_HEADS):
        ck = ckv[:, kv * NSA_HEAD_DIM:(kv + 1) * NSA_HEAD_DIM]
        cv = ckv[:, NSA_KV_DIM + kv * NSA_HEAD_DIM:NSA_KV_DIM + (kv + 1) * NSA_HEAD_DIM]
        qs = jnp.concatenate([q[:, (kv * NSA_GROUP + g) * NSA_HEAD_DIM:(kv * NSA_GROUP + g + 1) * NSA_HEAD_DIM]
                              for g in range(NSA_GROUP)], axis=0)
        s_all = _dot_nt(qs, ck)
        p_sum = jnp.zeros((tq, n_r), F32)
        for g in range(NSA_GROUP):
            h = kv * NSA_GROUP + g
            s = jnp.where(mask, s_all[g * tq:(g + 1) * tq] - _SLOPES[h] * dist, NEG_INF)
            e = jnp.exp(s - jnp.max(s, axis=-1, keepdims=True)) * maskf
            p = e / jnp.maximum(jnp.sum(e, axis=-1, keepdims=True), 1e-30)
            oc_ref[0, :, h * NSA_HEAD_DIM:(h + 1) * NSA_HEAD_DIM] = _dot(p.astype(BF16), cv)
            p_sum = p_sum + p
        mt = mt_ref[...]
        p_slc = sum(_dot_nt(mt, part) for part in _split3(p_sum))
        score = jnp.where(avail, p_slc + jnp.where(forced, FORCE_BONUS, 0.0), -1.0)
        sel_ref[0, kv] = _select_top(score, 0, n_slc)


def _slc_map(n_slc, n_r, n_pad):
    m = np.arange(n_pad)[:, None]
    r = np.arange(n_r)[None, :]
    ratio = SLC_BLOCK // CMP_STRIDE
    return jnp.asarray(((r >= ratio * m) & (r <= ratio * m + ratio) & (m < n_slc)).astype(np.float32), BF16)


def _cmp_prompt(qs, ckv):
    b, s, _ = qs.shape
    n_r = ckv.shape[1]
    n_slc = s // SLC_BLOCK
    mt = _slc_map(n_slc, n_r, n_slc)
    return pl.pallas_call(
        _cmp_prompt_kernel,
        grid=(b, s // CMP_TQ),
        in_specs=[pl.BlockSpec((1, CMP_TQ, NSA_Q_DIM), lambda i, t: (i, t, 0)),
                  pl.BlockSpec((1, n_r, 2 * NSA_KV_DIM), lambda i, t: (i, 0, 0)),
                  pl.BlockSpec(mt.shape, lambda i, t: (0, 0))],
        out_specs=[pl.BlockSpec((1, CMP_TQ, NSA_Q_DIM), lambda i, t: (i, t, 0)),
                   pl.BlockSpec((1, NSA_KV_HEADS, n_slc, CMP_TQ), lambda i, t: (i, 0, 0, t))],
        out_shape=[jax.ShapeDtypeStruct((b, s, NSA_Q_DIM), F32),
                   jax.ShapeDtypeStruct((b, NSA_KV_HEADS, n_slc, s), F32)],
        compiler_params=_cparams("parallel", "parallel"),
        name="nsa_cmp_prompt",
    )(qs, ckv, mt)


SLC_TQ = 256
SLC_TK = 1024


def _slc_prompt_kernel(q_ref, k_ref, v_ref, sel_ref, o_ref, m_ref, l_ref, acc_ref):
    tq, tk = SLC_TQ, SLC_TK
    qi = pl.program_id(1)
    kj = pl.program_id(2)
    n_sel = sel_ref.shape[3]

    @pl.when(kj == 0)
    def _():
        m_ref[...] = jnp.full_like(m_ref, NEG_INF)
        l_ref[...] = jnp.zeros_like(l_ref)
        acc_ref[...] = jnp.zeros_like(acc_ref)

    @pl.when(kj * tk <= qi * tq + tq - 1)
    def _():
        q = q_ref[0]
        kt = k_ref[0]
        vt = v_ref[0]
        qpos = qi * tq + lax.broadcasted_iota(jnp.int32, (tq, tk), 0)
        kpos = kj * tk + lax.broadcasted_iota(jnp.int32, (tq, tk), 1)
        dist_i = qpos - kpos
        krel = (kj * tk - qi * tq + lax.broadcasted_iota(jnp.int32, (1, tk), 1)).astype(F32)
        eb = lax.broadcasted_iota(jnp.int32, (n_sel, tk), 0)
        ec = (kj * tk + lax.broadcasted_iota(jnp.int32, (n_sel, tk), 1)) // SLC_BLOCK
        expand = jnp.where(eb == ec, 1.0, 0.0).astype(BF16)
        for kv in range(NSA_KV_HEADS):
            picked = _dot(sel_ref[0, kv], expand)
            mbias = jnp.where((picked > 0.5) & (dist_i >= 0), 0.0, NEG_INF)
            qs = jnp.concatenate([q[:, (kv * NSA_GROUP + g) * NSA_HEAD_DIM:(kv * NSA_GROUP + g + 1) * NSA_HEAD_DIM]
                                  for g in range(NSA_GROUP)], axis=0)
            s_all = _dot_nt(qs, kt[:, kv * NSA_HEAD_DIM:(kv + 1) * NSA_HEAD_DIM])
            vv = vt[:, kv * NSA_HEAD_DIM:(kv + 1) * NSA_HEAD_DIM]
            for g in range(NSA_GROUP):
                h = kv * NSA_GROUP + g
                rows = slice(g * tq, (g + 1) * tq)
                s = s_all[rows] + (mbias + _SLOPES[h] * krel)
                m_old = m_ref[kv, rows]
                m_new = jnp.maximum(m_old, jnp.max(s, axis=-1, keepdims=True))
                a = jnp.exp(m_old - m_new)
                e = jnp.exp(s - m_new)
                l_ref[kv, rows] = a * l_ref[kv, rows] + jnp.sum(e, axis=-1, keepdims=True)
                acc_ref[kv, rows] = a * acc_ref[kv, rows] + _dot(e.astype(BF16), vv)
                m_ref[kv, rows] = m_new

    @pl.when(kj == pl.num_programs(2) - 1)
    def _():
        for kv in range(NSA_KV_HEADS):
            for g in range(NSA_GROUP):
                h = kv * NSA_GROUP + g
                rows = slice(g * tq, (g + 1) * tq)
                o_ref[0, :, h * NSA_HEAD_DIM:(h + 1) * NSA_HEAD_DIM] = (
                    acc_ref[kv, rows] / jnp.maximum(l_ref[kv, rows], 1e-30))


def _slc_prompt(qs, kvb, sel):
    b, s, _ = qs.shape
    tq, tk = min(SLC_TQ, s), min(SLC_TK, s)
    assert tq == SLC_TQ and tk == SLC_TK
    n_sel = sel.shape[3]
    last = lambda t: (t * tq + tq - 1) // tk
    return pl.pallas_call(
        _slc_prompt_kernel,
        grid=(b, s // tq, s // tk),
        in_specs=[pl.BlockSpec((1, tq, NSA_Q_DIM), lambda i, t, j: (i, t, 0)),
                  pl.BlockSpec((1, tk, NSA_KV_DIM), lambda i, t, j: (i, jnp.minimum(j, last(t)), 2)),
                  pl.BlockSpec((1, tk, NSA_KV_DIM), lambda i, t, j: (i, jnp.minimum(j, last(t)), 3)),
                  pl.BlockSpec((1, NSA_KV_HEADS, tq, n_sel), lambda i, t, j: (i, 0, t, 0))],
        out_specs=pl.BlockSpec((1, tq, NSA_Q_DIM), lambda i, t, j: (i, t, 0)),
        out_shape=jax.ShapeDtypeStruct((b, s, NSA_Q_DIM), F32),
        scratch_shapes=[pltpu.VMEM((NSA_KV_HEADS, NSA_GROUP * tq, 1), F32),
                        pltpu.VMEM((NSA_KV_HEADS, NSA_GROUP * tq, 1), F32),
                        pltpu.VMEM((NSA_KV_HEADS, NSA_GROUP * tq, NSA_HEAD_DIM), F32)],
        compiler_params=_cparams("parallel", "parallel", "arbitrary"),
        name="nsa_slc_prompt",
    )(qs, kvb, kvb, sel)


WIN_TQ = 256
WIN_NT = WINDOW // WIN_TQ + 1


def _win_prompt_kernel(q_ref, *refs):
    tq = WIN_TQ
    k_refs = refs[:WIN_NT]
    v_refs = refs[WIN_NT:2 * WIN_NT]
    o_ref = refs[2 * WIN_NT]
    qi = pl.program_id(1)
    nk = WIN_NT * tq
    row = lax.broadcasted_iota(jnp.int32, (tq, nk), 0)
    col = lax.broadcasted_iota(jnp.int32, (tq, nk), 1)
    dist_i = WINDOW + row - col
    kpos = (qi - (WIN_NT - 1)) * tq + col
    mbias = jnp.where((dist_i >= 0) & (dist_i <= WINDOW) & (kpos >= 0), 0.0, NEG_INF)
    crel = (lax.broadcasted_iota(jnp.int32, (1, nk), 1) - WINDOW).astype(F32)
    q = q_ref[0]
    kcat = jnp.concatenate([r[0] for r in k_refs], axis=0)
    vcat = jnp.concatenate([r[0] for r in v_refs], axis=0)
    for kv in range(NSA_KV_HEADS):
        qs = jnp.concatenate([q[:, (kv * NSA_GROUP + g) * NSA_HEAD_DIM:(kv * NSA_GROUP + g + 1) * NSA_HEAD_DIM]
                              for g in range(NSA_GROUP)], axis=0)
        s_all = _dot_nt(qs, kcat[:, kv * NSA_HEAD_DIM:(kv + 1) * NSA_HEAD_DIM])
        vv = vcat[:, kv * NSA_HEAD_DIM:(kv + 1) * NSA_HEAD_DIM]
        for g in range(NSA_GROUP):
            h = kv * NSA_GROUP + g
            s = s_all[g * tq:(g + 1) * tq] + (mbias + _SLOPES[h] * crel)
            e = jnp.exp(s - jnp.max(s, axis=-1, keepdims=True))
            den = jnp.maximum(jnp.sum(e, axis=-1, keepdims=True), 1e-30)
            o_ref[0, :, h * NSA_HEAD_DIM:(h + 1) * NSA_HEAD_DIM] = _dot(e.astype(BF16), vv) / den


def _win_prompt(qs, kvb):
    b, s, _ = qs.shape
    tq = WIN_TQ
    kspec = lambda d, col: pl.BlockSpec((1, tq, NSA_KV_DIM),
                                        lambda i, t: (i, jnp.maximum(t - (WIN_NT - 1) + d, 0), col))
    return pl.pallas_call(
        _win_prompt_kernel,
        grid=(b, s // tq),
        in_specs=[pl.BlockSpec((1, tq, NSA_Q_DIM), lambda i, t: (i, t, 0))]
                 + [kspec(d, 4) for d in range(WIN_NT)] + [kspec(d, 5) for d in range(WIN_NT)],
        out_specs=pl.BlockSpec((1, tq, NSA_Q_DIM), lambda i, t: (i, t, 0)),
        out_shape=jax.ShapeDtypeStruct((b, s, NSA_Q_DIM), F32),
        compiler_params=_cparams("parallel", "parallel"),
        name="nsa_win_prompt",
    )(qs, *([kvb] * (2 * WIN_NT)))


def _merge_kernel(oc_ref, os_ref, ow_ref, hg_ref, ex_ref, o_ref):
    gate = 1.0 / (1.0 + jnp.exp(-hg_ref[...]))
    parts = _split3(gate)[:2]
    out = None
    for br, ref in enumerate((oc_ref, os_ref, ow_ref)):
        gx = sum(_dot(p, ex_ref[br]) for p in parts)
        term = gx * ref[...]
        out = term if out is None else out + term
    o_ref[...] = out.astype(o_ref.dtype)


def _gate_expand():
    e = np.zeros((3, LANES, NSA_Q_DIM), np.float32)
    for br in range(3):
        for h in range(NSA_HEADS):
            e[br, h * 3 + br, h * NSA_HEAD_DIM:(h + 1) * NSA_HEAD_DIM] = 1.0
    return jnp.asarray(e, BF16)


def _merge(oc, os_, ow, hg):
    n = oc.shape[0]
    tm = min(256, n)
    ex = _gate_expand()
    row = pl.BlockSpec((tm, NSA_Q_DIM), lambda i: (i, 0))
    return pl.pallas_call(
        _merge_kernel, grid=(n // tm,),
        in_specs=[row, row, row, pl.BlockSpec((tm, LANES), lambda i: (i, 0)),
                  pl.BlockSpec(ex.shape, lambda i: (0, 0, 0))],
        out_specs=row,
        out_shape=jax.ShapeDtypeStruct((n, NSA_Q_DIM), BF16),
        compiler_params=_cparams("parallel"), name="nsa_merge",
    )(oc, os_, ow, hg, ex)


def _nsa_weights(w_in, pool, pe, phi, w_out):
    w_main = w_in[:, :NSA_MAIN_DIM].astype(BF16)
    w_gate = jnp.pad(w_in[:, NSA_MAIN_DIM:], ((0, 0), (0, LANES - NSA_N_GATES))).astype(BF16)
    return w_main, w_gate, _compress_consts(pool, pe, phi), w_out.astype(BF16)


def _nsa_prompt(x, weights):
    b, s, _ = x.shape
    w_main, w_gate, consts, w_out = weights
    xb = x.reshape(b * s, D_MODEL).astype(BF16)
    h = _mm(xb, w_main)
    hg = _mm(xb, w_gate)
    h3 = h.reshape(b, s, NSA_MAIN_DIM)
    qs = (h3[..., :NSA_Q_DIM] * NSA_SCALE).astype(BF16)
    kvb = h3[..., NSA_Q_DIM:].astype(BF16)
    ckv = _compress(kvb, consts)
    o_c, sel_t = _cmp_prompt(qs, ckv)
    sel = jnp.swapaxes(sel_t, 2, 3).astype(BF16)
    o_s = _slc_prompt(qs, kvb, sel)
    o_w = _win_prompt(qs, kvb)
    o = _merge(o_c.reshape(b * s, NSA_Q_DIM), o_s.reshape(b * s, NSA_Q_DIM), o_w.reshape(b * s, NSA_Q_DIM), hg)
    y = _mm(o, w_out).reshape(b, s, D_MODEL)
    kv = h3[..., NSA_Q_DIM:].reshape(b, s, 6, NSA_KV_HEADS, NSA_HEAD_DIM)
    rows = kv[:, :, :4].reshape(b, s // PAGE_SIZE, PAGE_SIZE, 4, NSA_KV_HEADS, NSA_HEAD_DIM)
    win = kv[:, s - min(WINDOW, s):, 4:]
    return y, rows, win


def _gather_pages(page_refs):
    n_sk = 2 * NSA_KV_HEADS
    pages = []
    for r in page_refs:
        rows = r.reshape(PAGE_SIZE * n_sk, NSA_HEAD_DIM)
        pages.append(jnp.concatenate(
            [rows[pl.ds(sk, PAGE_SIZE, stride=n_sk), :].astype(BF16) for sk in range(n_sk)], axis=1))
    return jnp.concatenate(pages, axis=0)


def _cmp_sample_kernel(pt_ref, *refs, t_new, p_len):
    del pt_ref
    npg = SAMPLE_PAGES_PER_STEP
    page_refs = refs[:npg]
    q_ref, pb_ref, pbt_ref, pe_ref, phi_ref, mt_ref, oc_ref, sel_ref, tail_ref, s_ref, cv_ref = refs[npg:]
    c = pl.program_id(1)
    rows = NSA_GROUP * t_new
    n_r = s_ref.shape[2]

    @pl.when(c == 0)
    def _():
        tail_ref[...] = jnp.zeros_like(tail_ref)

    x = _gather_pages(page_refs)
    ckv = _compress_chunk(x, tail_ref[...], pb_ref, pbt_ref, pe_ref, phi_ref)
    tail_ref[...] = x[CMP_CHUNK - CMP_STRIDE:, :]
    r0 = pl.multiple_of(c * CMP_ROWS, CMP_ROWS)
    cv_ref[pl.ds(r0, CMP_ROWS), :] = ckv[:, NSA_KV_DIM:].astype(BF16)
    for kv in range(NSA_KV_HEADS):
        ck = ckv[:, kv * NSA_HEAD_DIM:(kv + 1) * NSA_HEAD_DIM].astype(BF16)
        s_ref[kv, :, pl.ds(r0, CMP_ROWS)] = _dot_nt(q_ref[0, kv], ck)

    @pl.when(c == pl.num_programs(1) - 1)
    def _():
        ri = lax.broadcasted_iota(jnp.int32, (rows, n_r), 1)
        ti = lax.broadcasted_iota(jnp.int32, (rows, n_r), 0) % t_new
        dist_i = (p_len + ti) - (CMP_STRIDE * ri + CMP_STRIDE - 1)
        mask = (dist_i >= 0) & (ri >= 1)
        maskf = mask.astype(F32)
        dist = dist_i.astype(F32)
        n_pad = mt_ref.shape[1]
        blk = lax.broadcasted_iota(jnp.int32, (rows, n_pad), 1)
        cur = (p_len + lax.broadcasted_iota(jnp.int32, (rows, n_pad), 0) % t_new) // SLC_BLOCK
        n_slc = -(-(p_len + t_new) // SLC_BLOCK)
        avail = blk <= cur
        forced = (blk == 0) | (blk == cur) | (blk == cur - 1)
        gi = lax.broadcasted_iota(jnp.int32, (rows, rows), 0) % t_new
        gj = lax.broadcasted_iota(jnp.int32, (rows, rows), 1) % t_new
        gsum = jnp.where(gi == gj, 1.0, 0.0).astype(BF16)
        for kv in range(NSA_KV_HEADS):
            s = jnp.where(mask, s_ref[kv] - _slope_col(kv, rows, t_new) * dist, NEG_INF)
            e = jnp.exp(s - jnp.max(s, axis=-1, keepdims=True)) * maskf
            p = e / jnp.maximum(jnp.sum(e, axis=-1, keepdims=True), 1e-30)
            oc_ref[0, kv] = _dot(p.astype(BF16), cv_ref[:, kv * NSA_HEAD_DIM:(kv + 1) * NSA_HEAD_DIM])
            p_sum = sum(_dot(gsum, part) for part in _split3(p))
            p_slc = sum(_dot(part, mt_ref[...]) for part in _split3(p_sum))
            score = jnp.where(avail, p_slc + jnp.where(forced, FORCE_BONUS, 0.0), -1.0)
            score = jnp.where(blk < n_slc, score, -3.0)
            sel_ref[0, kv] = _select_top(score, 1, n_pad)


def _page_spec(layer, half, d):
    npg = SAMPLE_PAGES_PER_STEP
    return pl.BlockSpec((1, 1, PAGE_SIZE, 2 * NSA_KV_HEADS, NSA_HEAD_DIM),
                        lambda i, c, pt: (layer, pt[i, c * npg + d], 0, half, 0))


def _cmp_sample(cache, layer, page_table, q16, consts, t_new):
    db, n_pages = page_table.shape
    p_len = n_pages * PAGE_SIZE
    n_chunks = p_len // CMP_CHUNK
    n_r = p_len // CMP_STRIDE
    n_slc = -(-(p_len + t_new) // SLC_BLOCK)
    n_pad = -(-n_slc // LANES) * LANES
    mt = _slc_map(n_slc, n_r, n_pad).T
    pb, pbt, pe_term, phi = consts
    rows = NSA_GROUP * t_new
    full = lambda a: pl.BlockSpec(a.shape, lambda i, c, pt: (0,) * a.ndim)
    gs = pltpu.PrefetchScalarGridSpec(
        num_scalar_prefetch=1,
        grid=(db, n_chunks),
        in_specs=[_page_spec(layer, 0, d) for d in range(SAMPLE_PAGES_PER_STEP)]
                 + [pl.BlockSpec((1, NSA_KV_HEADS, rows, NSA_HEAD_DIM), lambda i, c, pt: (i, 0, 0, 0)),
                    full(pb), full(pbt), full(pe_term), full(phi), full(mt)],
        out_specs=[pl.BlockSpec((1, NSA_KV_HEADS, rows, NSA_HEAD_DIM), lambda i, c, pt: (i, 0, 0, 0)),
                   pl.BlockSpec((1, NSA_KV_HEADS, rows, n_pad), lambda i, c, pt: (i, 0, 0, 0))],
        scratch_shapes=[pltpu.VMEM((CMP_STRIDE, 2 * NSA_KV_DIM), BF16),
                        pltpu.VMEM((NSA_KV_HEADS, rows, n_r), F32),
                        pltpu.VMEM((n_r, NSA_KV_DIM), BF16)],
    )
    return pl.pallas_call(
        functools.partial(_cmp_sample_kernel, t_new=t_new, p_len=p_len),
        grid_spec=gs,
        out_shape=[jax.ShapeDtypeStruct((db, NSA_KV_HEADS, rows, NSA_HEAD_DIM), F32),
                   jax.ShapeDtypeStruct((db, NSA_KV_HEADS, rows, n_pad), F32)],
        compiler_params=_cparams("parallel", "arbitrary"),
        name="nsa_cmp_sample",
    )(page_table, *([cache] * SAMPLE_PAGES_PER_STEP), q16, pb, pbt, pe_term, phi, mt)


def _online_update(s, maskf, vv, m_ref, l_ref, acc_ref, kv):
    m_old = m_ref[kv]
    m_new = jnp.maximum(m_old, jnp.max(s, axis=-1, keepdims=True))
    a = jnp.exp(m_old - m_new)
    e = jnp.exp(s - m_new) * maskf
    l_ref[kv] = a * l_ref[kv] + jnp.sum(e, axis=-1, keepdims=True)
    acc_ref[kv] = a * acc_ref[kv] + _dot(e.astype(BF16), vv)
    m_ref[kv] = m_new


def _slc_sample_kernel(pt_ref, *refs, t_new, p_len):
    del pt_ref
    npg = SAMPLE_PAGES_PER_STEP
    page_refs = refs[:npg]
    q_ref, sel_ref, kn_ref, vn_ref, o_ref, m_ref, l_ref, acc_ref = refs[npg:]
    c = pl.program_id(1)
    rows = NSA_GROUP * t_new
    n_pad = sel_ref.shape[3]

    @pl.when(c == 0)
    def _():
        m_ref[...] = jnp.full_like(m_ref, NEG_INF)
        l_ref[...] = jnp.zeros_like(l_ref)
        acc_ref[...] = jnp.zeros_like(acc_ref)

    x = _gather_pages(page_refs)
    eb = lax.broadcasted_iota(jnp.int32, (n_pad, CMP_CHUNK), 0)
    ec = (c * CMP_CHUNK + lax.broadcasted_iota(jnp.int32, (n_pad, CMP_CHUNK), 1)) // SLC_BLOCK
    expand = jnp.where(eb == ec, 1.0, 0.0).astype(BF16)
    ti = lax.broadcasted_iota(jnp.int32, (rows, CMP_CHUNK), 0) % t_new
    kpos = c * CMP_CHUNK + lax.broadcasted_iota(jnp.int32, (rows, CMP_CHUNK), 1)
    dist = ((p_len + ti) - kpos).astype(F32)
    for kv in range(NSA_KV_HEADS):
        mask = _dot(sel_ref[0, kv], expand) > 0.5
        s = _dot_nt(q_ref[0, kv], x[:, kv * NSA_HEAD_DIM:(kv + 1) * NSA_HEAD_DIM])
        s = jnp.where(mask, s - _slope_col(kv, rows, t_new) * dist, NEG_INF)
        _online_update(s, mask.astype(F32), x[:, NSA_KV_DIM + kv * NSA_HEAD_DIM:NSA_KV_DIM + (kv + 1) * NSA_HEAD_DIM],
                       m_ref, l_ref, acc_ref, kv)

    @pl.when(c == pl.num_programs(1) - 1)
    def _():
        n_new = kn_ref.shape[1]
        ti2 = lax.broadcasted_iota(jnp.int32, (rows, n_new), 0) % t_new
        ci = lax.broadcasted_iota(jnp.int32, (rows, n_new), 1)
        mask2 = (ci <= ti2) & (ci < t_new)
        dist2 = (ti2 - ci).astype(F32)
        for kv in range(NSA_KV_HEADS):
            s = _dot_nt(q_ref[0, kv], kn_ref[0, :, kv * NSA_HEAD_DIM:(kv + 1) * NSA_HEAD_DIM])
            s = jnp.where(mask2, s - _slope_col(kv, rows, t_new) * dist2, NEG_INF)
            _online_update(s, mask2.astype(F32), vn_ref[0, :, kv * NSA_HEAD_DIM:(kv + 1) * NSA_HEAD_DIM],
                           m_ref, l_ref, acc_ref, kv)
            o_ref[0, kv] = acc_ref[kv] / jnp.maximum(l_ref[kv], 1e-30)


def _slc_sample(cache, layer, page_table, q16, sel16, k_new, v_new, t_new):
    db, n_pages = page_table.shape
    p_len = n_pages * PAGE_SIZE
    assert p_len % SLC_BLOCK == 0
    n_chunks = p_len // CMP_CHUNK
    rows = NSA_GROUP * t_new
    n_pad = sel16.shape[3]
    per_b = lambda a: pl.BlockSpec((1,) + a.shape[1:], lambda i, c, pt: (i,) + (0,) * (a.ndim - 1))
    gs = pltpu.PrefetchScalarGridSpec(
        num_scalar_prefetch=1,
        grid=(db, n_chunks),
        in_specs=[_page_spec(layer, 1, d) for d in range(SAMPLE_PAGES_PER_STEP)]
                 + [per_b(q16), per_b(sel16), per_b(k_new), per_b(v_new)],
        out_specs=pl.BlockSpec((1, NSA_KV_HEADS, rows, NSA_HEAD_DIM), lambda i, c, pt: (i, 0, 0, 0)),
        scratch_shapes=[pltpu.VMEM((NSA_KV_HEADS, rows, 1), F32),
                        pltpu.VMEM((NSA_KV_HEADS, rows, 1), F32),
                        pltpu.VMEM((NSA_KV_HEADS, rows, NSA_HEAD_DIM), F32)],
    )
    return pl.pallas_call(
        functools.partial(_slc_sample_kernel, t_new=t_new, p_len=p_len),
        grid_spec=gs,
        out_shape=jax.ShapeDtypeStruct((db, NSA_KV_HEADS, rows, NSA_HEAD_DIM), F32),
        compiler_params=_cparams("parallel", "arbitrary"),
        name="nsa_slc_sample",
    )(page_table, *([cache] * SAMPLE_PAGES_PER_STEP), q16, sel16, k_new, v_new)


def _win_sample_kernel(q_ref, wk_ref, wv_ref, kn_ref, vn_ref, o_ref, *, t_new):
    rows = NSA_GROUP * t_new
    wb = wk_ref.shape[1]
    n_new = kn_ref.shape[1]
    t1 = lax.broadcasted_iota(jnp.int32, (rows, wb), 0) % t_new
    c1 = lax.broadcasted_iota(jnp.int32, (rows, wb), 1)
    d1 = wb + t1 - c1
    mask1 = (d1 >= 0) & (d1 <= WINDOW)
    t2 = lax.broadcasted_iota(jnp.int32, (rows, n_new), 0) % t_new
    c2 = lax.broadcasted_iota(jnp.int32, (rows, n_new), 1)
    d2 = t2 - c2
    mask2 = (d2 >= 0) & (d2 <= WINDOW) & (c2 < t_new)
    for kv in range(NSA_KV_HEADS):
        lanes = slice(kv * NSA_HEAD_DIM, (kv + 1) * NSA_HEAD_DIM)
        slope = _slope_col(kv, rows, t_new)
        s1 = jnp.where(mask1, _dot_nt(q_ref[0, kv], wk_ref[0, :, lanes]) - slope * d1.astype(F32), NEG_INF)
        s2 = jnp.where(mask2, _dot_nt(q_ref[0, kv], kn_ref[0, :, lanes]) - slope * d2.astype(F32), NEG_INF)
        m = jnp.maximum(jnp.max(s1, axis=-1, keepdims=True), jnp.max(s2, axis=-1, keepdims=True))
        e1 = jnp.exp(s1 - m) * mask1.astype(F32)
        e2 = jnp.exp(s2 - m) * mask2.astype(F32)
        den = jnp.maximum(jnp.sum(e1, axis=-1, keepdims=True) + jnp.sum(e2, axis=-1, keepdims=True), 1e-30)
        o_ref[0, kv] = (_dot(e1.astype(BF16), wv_ref[0, :, lanes]) + _dot(e2.astype(BF16), vn_ref[0, :, lanes])) / den


def _win_sample(q16, wk, wv, k_new, v_new, t_new):
    db = q16.shape[0]
    rows = NSA_GROUP * t_new
    per_b = lambda a: pl.BlockSpec((1,) + a.shape[1:], lambda i: (i,) + (0,) * (a.ndim - 1))
    return pl.pallas_call(
        functools.partial(_win_sample_kernel, t_new=t_new),
        grid=(db,),
        in_specs=[per_b(q16), per_b(wk), per_b(wv), per_b(k_new), per_b(v_new)],
        out_specs=pl.BlockSpec((1, NSA_KV_HEADS, rows, NSA_HEAD_DIM), lambda i: (i, 0, 0, 0)),
        out_shape=jax.ShapeDtypeStruct((db, NSA_KV_HEADS, rows, NSA_HEAD_DIM), F32),
        compiler_params=_cparams("parallel"),
        name="nsa_win_sample",
    )(q16, wk, wv, k_new, v_new)


def _pad_rows(a, n):
    return jnp.pad(a, ((0, 0), (0, n - a.shape[1]), (0, 0)))


def _nsa_sample(x, cache, layer, win_l, page_table, weights):
    db, t, _ = x.shape
    w_main, w_gate, consts, w_out = weights
    xb = x.reshape(db * t, D_MODEL).astype(BF16)
    h = _mm(xb, w_main)
    hg = _mm(xb, w_gate)
    h3 = h.reshape(db, t, NSA_MAIN_DIM)
    q16 = (h3[..., :NSA_Q_DIM] * NSA_SCALE).astype(BF16).reshape(db, t, NSA_KV_HEADS, NSA_GROUP, NSA_HEAD_DIM)
    q16 = q16.transpose(0, 2, 3, 1, 4).reshape(db, NSA_KV_HEADS, NSA_GROUP * t, NSA_HEAD_DIM)
    kvn = h3[..., NSA_Q_DIM:].reshape(db, t, 6, NSA_KV_DIM)
    new = lambda slot: _pad_rows(kvn[:, :, slot].astype(BF16), LANES)
    cache = cache.reshape(cache.shape[:3] + (4 * NSA_KV_HEADS, NSA_HEAD_DIM))
    o_c, sel16 = _cmp_sample(cache, layer, page_table, q16, consts, t)
    o_s = _slc_sample(cache, layer, page_table, q16, sel16.astype(BF16), new(2), new(3), t)
    wb = win_l.shape[1]
    wkv = win_l.astype(BF16).reshape(db, wb, 2, NSA_KV_DIM)
    o_w = _win_sample(q16, wkv[:, :, 0], wkv[:, :, 1], new(4), new(5), t)
    unrow = lambda o: o.reshape(db, NSA_KV_HEADS, NSA_GROUP, t, NSA_HEAD_DIM).transpose(0, 3, 1, 2, 4).reshape(
        db * t, NSA_Q_DIM)
    o = _merge(unrow(o_c), unrow(o_s), unrow(o_w), hg)
    y = _mm(o, w_out).reshape(db, t, D_MODEL)
    kv = h3[..., NSA_Q_DIM:].reshape(db, t, 6, NSA_KV_HEADS, NSA_HEAD_DIM)
    wbuf = jnp.concatenate([win_l, kv[:, :, 4:]], axis=1)
    return y, kv[:, :, :4], wbuf[:, wbuf.shape[1] - wb:]


S5_SEQS = SUBLANES
S5_CB_GROUPS = 16
S5_CB_STATES = S5_CB_GROUPS * S5_STATE
S5_CB_CH = S5_CB_GROUPS * S5_GROUP_CH
S5_N_CB = S5_GROUPS // S5_CB_GROUPS
S5_MAX_STEPS = 128


def _s5_discretize(a_re, a_im, log_dt, b_re, b_im):
    dt = jnp.exp(log_dt.astype(F32))[:, None]
    mag = jnp.exp(a_re * dt)
    ab_re = mag * jnp.cos(a_im * dt)
    ab_im = mag * jnp.sin(a_im * dt)
    den = a_re * a_re + a_im * a_im
    nr = ab_re - 1.0
    f_re = (nr * a_re + ab_im * a_im) / den
    f_im = (ab_im * a_re - nr * a_im) / den
    bb_re = f_re[..., None] * b_re - f_im[..., None] * b_im
    bb_im = f_re[..., None] * b_im + f_im[..., None] * b_re
    return ab_re, ab_im, bb_re, bb_im


def _s5_weights(w_in, a_re, a_im, log_dt, b_re, b_im, c_re, c_im, d, w_glu):
    ab_re, ab_im, bb_re, bb_im = _s5_discretize(a_re, a_im, log_dt, b_re, b_im)
    eye = jnp.eye(S5_CB_GROUPS, dtype=F32)

    def bd_in(bb):
        x = bb.reshape(S5_N_CB, S5_CB_GROUPS, S5_STATE, S5_GROUP_CH)
        return jnp.einsum('ngpc,gh->ngchp', x, eye).reshape(S5_N_CB, S5_CB_CH, S5_CB_STATES)

    def bd_out(cc):
        x = cc.reshape(S5_N_CB, S5_CB_GROUPS, S5_GROUP_CH, S5_STATE)
        return jnp.einsum('ngcp,gh->ngphc', x, eye).reshape(S5_N_CB, S5_CB_STATES, S5_CB_CH)

    bmat = jnp.concatenate([bd_in(bb_re), bd_in(bb_im)], axis=2).astype(BF16)
    cmat = jnp.concatenate([bd_out(c_re.astype(F32)), -bd_out(c_im.astype(F32))], axis=1).astype(BF16)
    rep = lambda a: jnp.broadcast_to(a.reshape(S5_N_CB, 1, S5_CB_STATES), (S5_N_CB, S5_SEQS, S5_CB_STATES))
    return dict(w_in=w_in.astype(BF16), w_glu=w_glu.astype(BF16), bmat=bmat, cmat=cmat,
                ar=rep(ab_re), ai=rep(ab_im), ab_re=ab_re, ab_im=ab_im, d=d.reshape(1, D_MODEL).astype(F32))


def _s5_scan_kernel(u_ref, b_ref, c_ref, ar_ref, ai_ref, d_ref, h0r_ref, h0i_ref, *rest, steps, with_y):
    if with_y:
        y_ref, hr_ref, hi_ref, bu_ref = rest
    else:
        hr_ref, hi_ref, bu_ref = rest
    ns = S5_CB_STATES

    @pl.when(pl.program_id(1) == 0)
    def _():
        hr_ref[...] = h0r_ref[...]
        hi_ref[...] = h0i_ref[...]

    u = u_ref[...]
    bu_ref[...] = _dot(u.astype(BF16), b_ref[0])
    ar = ar_ref[0]
    ai = ai_ref[0]

    def step(s, carry):
        hr, hi = carry
        r0 = pl.multiple_of(s * S5_SEQS, S5_SEQS)
        br = bu_ref[pl.ds(r0, S5_SEQS), :ns]
        bi = bu_ref[pl.ds(r0, S5_SEQS), ns:]
        nr = ar * hr - ai * hi + br
        ni = ar * hi + ai * hr + bi
        if with_y:
            bu_ref[pl.ds(r0, S5_SEQS), :ns] = nr
            bu_ref[pl.ds(r0, S5_SEQS), ns:] = ni
        return nr, ni

    hr, hi = lax.fori_loop(0, steps, step, (hr_ref[0], hi_ref[0]))
    hr_ref[0] = hr
    hi_ref[0] = hi
    if with_y:
        y_ref[...] = _dot(bu_ref[...].astype(BF16), c_ref[0]) + d_ref[...] * u


def _s5_scan(u_rows, w, h0r, h0i, with_y=True):
    n_rows = u_rows.shape[0]
    n_steps = n_rows // S5_SEQS
    steps = min(S5_MAX_STEPS, n_steps)
    assert n_steps % steps == 0
    tr = steps * S5_SEQS
    cb_spec = lambda a: pl.BlockSpec((1,) + a.shape[1:], lambda cb, t: (cb,) + (0,) * (a.ndim - 1))
    y_spec = [pl.BlockSpec((tr, S5_CB_CH), lambda cb, t: (t, cb))] if with_y else []
    y_shape = [jax.ShapeDtypeStruct((n_rows, D_MODEL), F32)] if with_y else []
    return pl.pallas_call(
        functools.partial(_s5_scan_kernel, steps=steps, with_y=with_y),
        grid=(S5_N_CB, n_steps // steps),
        in_specs=[pl.BlockSpec((tr, S5_CB_CH), lambda cb, t: (t, cb)),
                  cb_spec(w['bmat']), cb_spec(w['cmat']), cb_spec(w['ar']), cb_spec(w['ai']),
                  pl.BlockSpec((1, S5_CB_CH), lambda cb, t: (0, cb)),
                  cb_spec(h0r), cb_spec(h0i)],
        out_specs=y_spec + [cb_spec(h0r), cb_spec(h0i)],
        out_shape=y_shape + [jax.ShapeDtypeStruct(h0r.shape, F32), jax.ShapeDtypeStruct(h0i.shape, F32)],
        scratch_shapes=[pltpu.VMEM((tr, 2 * S5_CB_STATES), F32)],
        compiler_params=_cparams("parallel", "arbitrary"),
        name="s5_scan",
    )(u_rows, w['bmat'], w['cmat'], w['ar'], w['ai'], w['d'], h0r, h0i)


def _state_to_blocks(h):
    return h.reshape(h.shape[0], S5_N_CB, S5_CB_STATES).transpose(1, 0, 2)


def _blocks_to_state(h):
    return h.transpose(1, 0, 2).reshape(h.shape[1], S5_GROUPS, S5_STATE)


def _cpow2(re, im, n):
    for _ in range(n):
        re, im = re * re - im * im, 2.0 * re * im
    return re, im


def _s5_prompt(x, w):
    b, t, _ = x.shape
    n_seg = S5_SEQS // b
    seg = t // n_seg
    assert n_seg * b == S5_SEQS and seg & (seg - 1) == 0
    xr = x.reshape(b, n_seg, seg, D_MODEL).transpose(2, 0, 1, 3).reshape(t * b, D_MODEL)
    u = _mm(xr.astype(BF16), w['w_in'])
    zero = jnp.zeros((S5_N_CB, S5_SEQS, S5_CB_STATES), F32)
    er, ei = _s5_scan(u, w, zero, zero, with_y=False)
    er = _blocks_to_state(er).reshape(b, n_seg, S5_GROUPS, S5_STATE)
    ei = _blocks_to_state(ei).reshape(b, n_seg, S5_GROUPS, S5_STATE)
    pr, pi = _cpow2(w['ab_re'], w['ab_im'], int(math.log2(seg)))
    sr = [jnp.zeros((b, S5_GROUPS, S5_STATE), F32)]
    si = [jnp.zeros((b, S5_GROUPS, S5_STATE), F32)]
    for k in range(n_seg - 1):
        sr.append(er[:, k] + pr * sr[k] - pi * si[k])
        si.append(ei[:, k] + pr * si[k] + pi * sr[k])
    h0r = _state_to_blocks(jnp.stack(sr, axis=1).reshape(S5_SEQS, S5_GROUPS, S5_STATE))
    h0i = _state_to_blocks(jnp.stack(si, axis=1).reshape(S5_SEQS, S5_GROUPS, S5_STATE))
    y, hr, hi = _s5_scan(u, w, h0r, h0i)
    hr = _blocks_to_state(hr).reshape(b, n_seg, S5_GROUPS, S5_STATE)[:, -1]
    hi = _blocks_to_state(hi).reshape(b, n_seg, S5_GROUPS, S5_STATE)[:, -1]
    unperm = lambda a: a.reshape(seg, b, n_seg, -1).transpose(1, 2, 0, 3).reshape(b * t, -1)
    z = _mm(unperm(_gelu_rows(y)), w['w_glu'])
    return z, jnp.stack([hr, hi], axis=1)


def _s5_sample(x, h0, w):
    db, t, _ = x.shape
    assert db == S5_SEQS
    xr = x.transpose(1, 0, 2).reshape(t * db, D_MODEL)
    u = _mm(xr.astype(BF16), w['w_in'])
    y, hr, hi = _s5_scan(u, w, _state_to_blocks(h0[:, 0].astype(F32)), _state_to_blocks(h0[:, 1].astype(F32)))
    z = _mm(_gelu_rows(y), w['w_glu'])
    z = z.reshape(t, db, -1).transpose(1, 0, 2).reshape(db * t, -1)
    return z, jnp.stack([_blocks_to_state(hr), _blocks_to_state(hi)], axis=1).astype(h0.dtype)


def _gelu_kernel(y_ref, o_ref):
    o_ref[...] = _gelu_tanh(y_ref[...]).astype(o_ref.dtype)


def _gelu_rows(y):
    n, d = y.shape
    tm = min(256, n)
    row = pl.BlockSpec((tm, d), lambda i: (i, 0))
    return pl.pallas_call(_gelu_kernel, grid=(n // tm,), in_specs=[row], out_specs=row,
                          out_shape=jax.ShapeDtypeStruct((n, d), BF16),
                          compiler_params=_cparams("parallel"), name="gelu")(y)


ROUTER_TN = 256
N_RANKED = PEER_TOPK
_CAND_PAIRS = [(i, j) for i in range(N_RANKED) for j in range(N_RANKED) if (i + 1) * (j + 1) <= N_RANKED]


UNRANKED = 127.0


def _top_values(s, n, with_rank=False):
    vals = []
    cur = s
    rank = jnp.full(s.shape, UNRANKED, F32)
    for t in range(n):
        m = jnp.max(cur, axis=0, keepdims=True)
        vals.append(m)
        hit = cur == m
        if with_rank:
            rank = jnp.where(hit, float(t), rank)
        cur = jnp.where(hit, -jnp.inf, cur)
    return (vals, rank) if with_rank else vals


def _router_kernel(x_ref, wq_ref, keys_ref, w1z_ref, cnt_ref, rank_ref, w2_ref):
    q = _dot(x_ref[...].astype(BF16), wq_ref[...])
    for h in range(PEER_HEADS):
        s = [_dot_nt(keys_ref[2 * h + c], q[:, (2 * h + c) * PEER_HALF:(2 * h + c + 1) * PEER_HALF].astype(BF16))
             for c in range(2)]
        top1 = _top_values(s[0], N_RANKED)
        top2, rank2 = _top_values(s[1], N_RANKED, with_rank=True)
        cand = jnp.concatenate([top1[i] + top2[j] for i, j in _CAND_PAIRS], axis=0)
        tau = _top_values(cand, PEER_TOPK)[PEER_TOPK - 1]
        m1, m2 = top1[0], top2[0]
        z = jnp.sum(jnp.where(cand >= tau, jnp.exp(cand - (m1 + m2)), 0.0), axis=0, keepdims=True)
        cnt = jnp.zeros(s[0].shape, F32)
        for j in range(N_RANKED):
            cnt = cnt + jnp.where(s[0] + top2[j] >= tau, 1.0, 0.0)
        w1z_ref[h] = jnp.exp(s[0] - m1) / z
        cnt_ref[h] = cnt
        rank_ref[h] = rank2.astype(rank_ref.dtype)
        w2_ref[h] = jnp.exp(s[1] - m2).astype(w2_ref.dtype)


def _router(x, wq, keys):
    n = x.shape[0]
    tn = min(ROUTER_TN, n)
    assert n % tn == 0
    shape = (PEER_HEADS, PEER_N_KEYS, n)
    ospec = pl.BlockSpec((PEER_HEADS, PEER_N_KEYS, tn), lambda i: (0, 0, i))
    return pl.pallas_call(
        _router_kernel, grid=(n // tn,),
        in_specs=[pl.BlockSpec((tn, D_MODEL), lambda i: (i, 0)),
                  pl.BlockSpec(wq.shape, lambda i: (0, 0)),
                  pl.BlockSpec(keys.shape, lambda i: (0, 0, 0))],
        out_specs=[ospec] * 4,
        out_shape=[jax.ShapeDtypeStruct(shape, F32), jax.ShapeDtypeStruct(shape, F32),
                   jax.ShapeDtypeStruct(shape, BF16), jax.ShapeDtypeStruct(shape, BF16)],
        compiler_params=_cparams("parallel"), name="peer_router",
    )(x, wq, keys)


EXPERT_TN = 512
EXPERT_TE = 1024
EXPERT_NC = EXPERT_TE // PEER_N_KEYS
EXPERT_NT = PEER_N_EXPERTS // EXPERT_TE


def _expert_gate(w1z_ref, cnt_ref, rank_ref, w2_ref, c):
    g = None
    for h in range(PEER_HEADS):
        w2 = w2_ref[h]
        w1 = w1z_ref[h, pl.ds(c, 1), :].astype(BF16)
        cnt = cnt_ref[h, pl.ds(c, 1), :].astype(BF16)
        term = w1 * jnp.where(rank_ref[h] < cnt, w2, jnp.zeros_like(w2))
        g = term if g is None else g + term
    return g


def _expert_kernel(xt_ref, u_ref, v_ref, w1z_ref, cnt_ref, rank_ref, w2_ref, o_ref):
    j = pl.program_id(1)

    @pl.when(j == 0)
    def _():
        o_ref[...] = jnp.zeros_like(o_ref)

    ht = _dot(u_ref[...], xt_ref[...])
    parts = []
    for cc in range(EXPERT_NC):
        g = _expert_gate(w1z_ref, cnt_ref, rank_ref, w2_ref, j * EXPERT_NC + cc)
        parts.append(_gelu_tanh(ht[cc * PEER_N_KEYS:(cc + 1) * PEER_N_KEYS].astype(BF16)) * g)
    o_ref[...] += lax.dot_general(v_ref[...], jnp.concatenate(parts, axis=0), (((0,), (0,)), ((), ())),
                                  preferred_element_type=F32)


def _expert(xt, u, v, layer, w1z, cnt, rank, w2):
    n = xt.shape[1]
    tn = min(EXPERT_TN, n)
    assert n % tn == 0
    rspec = pl.BlockSpec((PEER_HEADS, PEER_N_KEYS, tn), lambda i, j: (0, 0, i))
    wspec = pl.BlockSpec((None, EXPERT_TE, D_MODEL), lambda i, j: (layer, j, 0))
    return pl.pallas_call(
        _expert_kernel, grid=(n // tn, EXPERT_NT),
        in_specs=[pl.BlockSpec((D_MODEL, tn), lambda i, j: (0, i)), wspec, wspec, rspec, rspec, rspec, rspec],
        out_specs=pl.BlockSpec((D_MODEL, tn), lambda i, j: (0, i)),
        out_shape=jax.ShapeDtypeStruct((D_MODEL, n), F32),
        compiler_params=_cparams("parallel", "arbitrary"), name="peer_expert",
    )(xt, u, v, w1z, cnt, rank, w2)


def _peer_tables(peer_u, peer_v):
    return peer_u.astype(BF16), peer_v.astype(BF16)


def _peer_t(x, xt, w_q, sub_keys, tables, layer):
    u, v = tables
    keys = sub_keys.reshape(2 * PEER_HEADS, PEER_N_KEYS, PEER_HALF).astype(BF16)
    w1z, cnt, rank, w2 = _router(x, w_q.astype(BF16), keys)
    return _expert(xt, u, v, layer, w1z, cnt, rank, w2)


def _peer(x, w_q, sub_keys, tables, layer):
    n = x.shape[0]
    xp = jnp.pad(x, ((0, -(-n // LANES) * LANES - n), (0, 0)))
    return _peer_t(xp, xp.T.astype(BF16), w_q, sub_keys, tables, layer).T[:n]


def kernel(x_prompt, x_sample, cache_nsa, state_win, state_s5, page_table, nsa_w_in, nsa_cmp_pool, nsa_cmp_pe,
           nsa_cmp_phi, nsa_w_out, s5_w_in, s5_a_re, s5_a_im, s5_log_dt, s5_b_re, s5_b_im, s5_c_re, s5_c_im, s5_d,
           s5_w_glu, peer_w_q, peer_sub_keys, peer_u, peer_v, ln_g, ln_b):
    b, s, _ = x_prompt.shape
    db, t, _ = x_sample.shape
    xp = x_prompt.reshape(b * s, D_MODEL)
    xs = x_sample.reshape(db * t, D_MODEL)
    rows_p, rows_s, win_p, win_s, s5_p, s5_s = [], [], [], [], [], []
    tables = _peer_tables(peer_u, peer_v)
    for layer in range(DEPTH):
        j = layer // N_MIXERS
        if layer % N_MIXERS == 0:
            w = _nsa_weights(nsa_w_in[j], nsa_cmp_pool[j], nsa_cmp_pe[j], nsa_cmp_phi[j], nsa_w_out[j])
            mp, rp, wp = _nsa_prompt(xp.reshape(b, s, D_MODEL), w)
            ms, rs, ws = _nsa_sample(xs.reshape(db, t, D_MODEL), cache_nsa, j, state_win[j], page_table, w)
            rows_p.append(rp)
            rows_s.append(rs)
            win_p.append(wp)
            win_s.append(ws)
            xp, xpt = _ln_res(xp, mp.reshape(b * s, D_MODEL), ln_g[layer, 0], ln_b[layer, 0], with_t=True)
            xs = _ln_res(xs, ms.reshape(db * t, D_MODEL), ln_g[layer, 0], ln_b[layer, 0])
        else:
            w = _s5_weights(s5_w_in[j], s5_a_re[j], s5_a_im[j], s5_log_dt[j], s5_b_re[j], s5_b_im[j],
                            s5_c_re[j], s5_c_im[j], s5_d[j], s5_w_glu[j])
            zp, hp = _s5_prompt(xp.reshape(b, s, D_MODEL), w)
            zs, hs = _s5_sample(xs.reshape(db, t, D_MODEL), state_s5[j], w)
            s5_p.append(hp.astype(state_s5.dtype))
            s5_s.append(hs)
            xp, xpt = _ln_res(xp, zp, ln_g[layer, 0], ln_b[layer, 0], mixer='glu', with_t=True)
            xs = _ln_res(xs, zs, ln_g[layer, 0], ln_b[layer, 0], mixer='glu')
        xp = _ln_res(xp, _peer_t(xp, xpt, peer_w_q[layer], peer_sub_keys[layer], tables, layer),
                     ln_g[layer, 1], ln_b[layer, 1], mixer='cols')
        xs = _ln_res(xs, _peer(xs, peer_w_q[layer], peer_sub_keys[layer], tables, layer),
                     ln_g[layer, 1], ln_b[layer, 1])
    return (xp.reshape(b, s, D_MODEL), xs.reshape(db, t, D_MODEL), jnp.stack(rows_p), jnp.stack(rows_s),
            jnp.stack(win_p), jnp.stack(win_s), jnp.stack(s5_p), jnp.stack(s5_s))
```
